```python
import jax, jax.numpy as jnp
from jax import lax
import numpy as np

D_MODEL = 1024
BATCH = 16
SEQ = 4096
DEPTH = 2

N_META = 16
HEAD_DIM = 64
CONF_HEADS = 8
SC_HEADS = 8
CONF_W = CONF_HEADS * HEAD_DIM
SC_W = SC_HEADS * HEAD_DIM
MIX_W = CONF_W + SC_W
PROJ_W = 2 * CONF_W + 3 * SC_W
CONF_KERNEL = 31
SC_KERNEL = 3
N_GROUPS = 4
EXPERTS_PER_GROUP = 8
N_EXPERTS = N_GROUPS * EXPERTS_PER_GROUP
TOP_K_EXPERT = 2
D_EXPERT = 512
MOE_BLOCK = 256
EPS = 1e-6

kernel_name = "hymba_conformer_shortconv_hier_moe"


def rms_norm(x, g):
    xf = x.astype(jnp.float32)
    y = xf * lax.rsqrt(jnp.mean(xf * xf, axis=-1, keepdims=True) + EPS)
    return (y * g.astype(jnp.float32)).astype(x.dtype)


def layer_norm(x, g, b):
    xf = x.astype(jnp.float32)
    mu = jnp.mean(xf, axis=-1, keepdims=True)
    xc = xf - mu
    y = xc * lax.rsqrt(jnp.mean(xc * xc, axis=-1, keepdims=True) + EPS)
    return (y * g.astype(jnp.float32) + b.astype(jnp.float32)).astype(x.dtype)


def causal_depthwise_conv(x, w):
    k, c = w.shape
    return lax.conv_general_dilated(
        x, w[:, None, :].astype(x.dtype), window_strides=(1,), padding=[(k - 1, 0)],
        dimension_numbers=("NWC", "WIO", "NWC"), feature_group_count=c)


def hierarchical_route(xf, w_rg, b_rg, w_re, b_re):
    x32 = xf.astype(jnp.float32)
    g_logits = x32 @ w_rg.astype(jnp.float32) + b_rg.astype(jnp.float32)
    g_prob = jax.nn.softmax(g_logits, axis=-1)
    g_top, g_idx = lax.top_k(g_prob, 1)
    e_logits = (x32 @ w_re.astype(jnp.float32) + b_re.astype(jnp.float32)).reshape(
        -1, N_GROUPS, EXPERTS_PER_GROUP)
    e_logits = jnp.take_along_axis(e_logits, g_idx[:, :, None], axis=1)[:, 0]
    e_prob = jax.nn.softmax(e_logits, axis=-1)
    e_top, e_loc = lax.top_k(e_prob, TOP_K_EXPERT)
    e_top = e_top / jnp.sum(e_top, axis=-1, keepdims=True)
    expert_id = g_idx * EXPERTS_PER_GROUP + e_loc
    expert_w = g_top * e_top
    return expert_id.astype(jnp.int32), expert_w


def routed_experts(xf, expert_id, expert_w, w_gate, w_up, w_down):
    n_tok, d = xf.shape
    n_asg = n_tok * TOP_K_EXPERT
    n_blocks = -(-n_asg // MOE_BLOCK) + N_EXPERTS
    flat_e = expert_id.reshape(-1)
    order = jnp.argsort(flat_e)
    sorted_e = flat_e[order]
    counts = jnp.bincount(flat_e, length=N_EXPERTS)
    padded = (counts + MOE_BLOCK - 1) // MOE_BLOCK * MOE_BLOCK
    pad_end = jnp.cumsum(padded)
    pad_start = pad_end - padded
    start = jnp.cumsum(counts) - counts
    dest_sorted = pad_start[sorted_e] + jnp.arange(n_asg, dtype=jnp.int32) - start[sorted_e]
    dest = jnp.zeros((n_asg,), jnp.int32).at[order].set(dest_sorted.astype(jnp.int32))
    src_tok = jnp.arange(n_asg, dtype=jnp.int32) // TOP_K_EXPERT
    row_tok = jnp.full((n_blocks * MOE_BLOCK,), n_tok, jnp.int32).at[dest].set(src_tok)
    block_e = jnp.minimum(
        jnp.searchsorted(pad_end, jnp.arange(n_blocks, dtype=jnp.int32) * MOE_BLOCK, side="right"),
        N_EXPERTS - 1)
    x_pad = jnp.concatenate([xf, jnp.zeros((1, d), xf.dtype)], axis=0)
    x_blocks = x_pad[row_tok].reshape(n_blocks, MOE_BLOCK, d)

    def expert_block(args):
        xb, e = args
        hid = jax.nn.silu(xb @ w_gate[e]) * (xb @ w_up[e])
        return hid @ w_down[e]

    y_blocks = lax.map(expert_block, (x_blocks, block_e))
    y_asg = y_blocks.reshape(-1, d)[dest].reshape(n_tok, TOP_K_EXPERT, d)
    return jnp.einsum("nk,nkd->nd", expert_w.astype(y_asg.dtype), y_asg)


def setup_inputs(seed: int = 0) -> dict:
    key = jax.random.key(seed)
    ks = jax.random.split(key, 20)
    f32 = jnp.float32
    nrm = lambda k, shape, scale: jax.random.normal(k, shape, f32) * scale
    return {
        "x": nrm(ks[0], (BATCH, SEQ, D_MODEL), 1.0),
        "meta_tokens": nrm(ks[1], (N_META, D_MODEL), 1.0),
        "norm_mix_g": 1.0 + nrm(ks[2], (DEPTH, D_MODEL), 0.02),
        "w_in": nrm(ks[3], (DEPTH, D_MODEL, PROJ_W), D_MODEL ** -0.5),
        "conf_dw_w": nrm(ks[4], (DEPTH, CONF_KERNEL, CONF_W), CONF_KERNEL ** -0.5),
        "conf_dw_b": nrm(ks[5], (DEPTH, CONF_W), 0.02),
        "conf_ln_g": 1.0 + nrm(ks[6], (DEPTH, CONF_W), 0.02),
        "conf_ln_b": nrm(ks[7], (DEPTH, CONF_W), 0.02),
        "sc_conv_w": nrm(ks[8], (DEPTH, SC_KERNEL, SC_W), SC_KERNEL ** -0.5),
        "w_out": nrm(ks[9], (DEPTH, MIX_W, D_MODEL), MIX_W ** -0.5),
        "norm_ffn_g": 1.0 + nrm(ks[10], (DEPTH, D_MODEL), 0.02),
        "w_router_group": nrm(ks[11], (DEPTH, D_MODEL, N_GROUPS), D_MODEL ** -0.5),
        "b_router_group": nrm(ks[12], (DEPTH, N_GROUPS), 0.01),
        "w_router_expert": nrm(ks[13], (DEPTH, D_MODEL, N_EXPERTS), D_MODEL ** -0.5),
        "b_router_expert": nrm(ks[14], (DEPTH, N_EXPERTS), 0.01),
        "w_exp_gate": nrm(ks[15], (DEPTH, N_EXPERTS, D_MODEL, D_EXPERT), D_MODEL ** -0.5),
        "w_exp_up": nrm(ks[16], (DEPTH, N_EXPERTS, D_MODEL, D_EXPERT), D_MODEL ** -0.5),
        "w_exp_down": nrm(ks[17], (DEPTH, N_EXPERTS, D_EXPERT, D_MODEL), D_EXPERT ** -0.5),
        "final_norm_g": 1.0 + nrm(ks[18], (D_MODEL,), 0.02),
    }


def reference(x, meta_tokens, norm_mix_g, w_in, conf_dw_w, conf_dw_b, conf_ln_g, conf_ln_b,
              sc_conv_w, w_out, norm_ffn_g, w_router_group, b_router_group, w_router_expert,
              b_router_expert, w_exp_gate, w_exp_up, w_exp_down, final_norm_g):
    bsz = x.shape[0]
    meta = jnp.broadcast_to(meta_tokens.astype(x.dtype)[None], (bsz, N_META, D_MODEL))
    h = jnp.concatenate([meta, x], axis=1)
    t_len = h.shape[1]
    for l in range(DEPTH):
        xn = rms_norm(h, norm_mix_g[l])
        proj = xn @ w_in[l]
        u, gl, sc_b, sc_c, v = jnp.split(
            proj, [CONF_W, 2 * CONF_W, 2 * CONF_W + SC_W, 2 * CONF_W + 2 * SC_W], axis=-1)
        a = u * jax.nn.sigmoid(gl)
        a = causal_depthwise_conv(a, conf_dw_w[l]) + conf_dw_b[l]
        a = jax.nn.silu(layer_norm(a, conf_ln_g[l], conf_ln_b[l]))
        s = sc_b * causal_depthwise_conv(sc_c * v, sc_conv_w[l])
        h = h + jnp.concatenate([a, s], axis=-1) @ w_out[l]
        hn = rms_norm(h, norm_ffn_g[l]).reshape(-1, D_MODEL)
        expert_id, expert_w = hierarchical_route(
            hn, w_router_group[l], b_router_group[l], w_router_expert[l], b_router_expert[l])
        moe = routed_experts(hn, expert_id, expert_w, w_exp_gate[l], w_exp_up[l], w_exp_down[l])
        h = h + moe.reshape(bsz, t_len, D_MODEL)
    out = rms_norm(h, final_norm_g)
    return out[:, N_META:]
```

```python
import functools

import jax
import jax.numpy as jnp
from jax import lax
from jax.experimental import pallas as pl
from jax.experimental.pallas import tpu as pltpu

D_MODEL = 1024
N_META = 16
CONF_W = 512
SC_W = 512
PROJ_W = 2 * CONF_W + 3 * SC_W
CONF_KERNEL = 31
SC_KERNEL = 3
N_GROUPS = 4
EXPERTS_PER_GROUP = 8
N_EXPERTS = N_GROUPS * EXPERTS_PER_GROUP
D_EXPERT = 512
EPS = 1e-6

LANES = 128
HALO_A = 32
HALO_C = 8
CONV_ROWS = 32
MIX_TILE = 512
MOE_ROWS = 256
N_CLASS_IDS = N_GROUPS * 64
N_CLASSES = N_GROUPS * (EXPERTS_PER_GROUP * (EXPERTS_PER_GROUP - 1) // 2)
VMEM_LIMIT = 56 * 1024 * 1024
ROUTER_COLS = LANES


def _rms(x, g):
    return x * lax.rsqrt(jnp.mean(x * x, axis=-1, keepdims=True) + EPS) * g


def _router_logits(hn, w_hi_ref, w_lo_ref, b_ref):
    hi = hn.astype(jnp.bfloat16)
    lo = (hn - hi.astype(jnp.float32)).astype(jnp.bfloat16)
    acc = jnp.dot(hi, w_hi_ref[...], preferred_element_type=jnp.float32)
    acc = acc + jnp.dot(lo, w_hi_ref[...], preferred_element_type=jnp.float32)
    acc = acc + jnp.dot(hi, w_lo_ref[...], preferred_element_type=jnp.float32)
    return acc + b_ref[...]


def _mixer_kernel(h_ref, halo_a_ref, halo_c_ref, g_mix_ref, w_in_ref, dw_w_ref, dw_b_ref, ln_g_ref,
                  ln_b_ref, sc_w_ref, w_out_ref, g_ffn_ref, wr_hi_ref, wr_lo_ref, br_ref,
                  hout_ref, cls_ref, halo_a_out, halo_c_out,
                  aext, cext, sb_buf, ybuf, *, tt):
    j = pl.program_id(1)

    @pl.when(j == 0)
    def _():
        aext[0:HALO_A, :] = halo_a_ref[...]
        cext[0:HALO_C, :] = halo_c_ref[...]

    h = h_ref[...]
    xn = _rms(h, g_mix_ref[...]).astype(jnp.bfloat16)
    ug = jnp.dot(xn, w_in_ref[:, 0:2 * CONF_W], preferred_element_type=jnp.float32)
    aext[HALO_A:HALO_A + tt, :] = ug[:, 0:CONF_W] * jax.nn.sigmoid(ug[:, CONF_W:2 * CONF_W])
    bcv = jnp.dot(xn, w_in_ref[:, 2 * CONF_W:PROJ_W], preferred_element_type=jnp.float32)
    sb_buf[...] = bcv[:, 0:SC_W]
    cext[HALO_C:HALO_C + tt, :] = bcv[:, SC_W:2 * SC_W] * bcv[:, 2 * SC_W:3 * SC_W]

    ch = min(tt, CONV_ROWS)
    for c0 in range(0, tt, ch):
        acc = jnp.broadcast_to(dw_b_ref[...], (ch, CONF_W))
        for k in range(CONF_KERNEL):
            acc = acc + aext[pl.ds(HALO_A - (CONF_KERNEL - 1) + c0 + k, ch), :] * dw_w_ref[k:k + 1, :]
        mu = jnp.mean(acc, axis=-1, keepdims=True)
        xc = acc - mu
        ln = xc * lax.rsqrt(jnp.mean(xc * xc, axis=-1, keepdims=True) + EPS) * ln_g_ref[...] + ln_b_ref[...]
        ybuf[c0:c0 + ch, 0:CONF_W] = jax.nn.silu(ln).astype(jnp.bfloat16)
        sacc = cext[pl.ds(HALO_C - (SC_KERNEL - 1) + c0, ch), :] * sc_w_ref[0:1, :]
        for k in range(1, SC_KERNEL):
            sacc = sacc + cext[pl.ds(HALO_C - (SC_KERNEL - 1) + c0 + k, ch), :] * sc_w_ref[k:k + 1, :]
        ybuf[c0:c0 + ch, CONF_W:CONF_W + SC_W] = (sb_buf[c0:c0 + ch, :] * sacc).astype(jnp.bfloat16)

    new_halo_a = aext[tt:tt + HALO_A, :]
    new_halo_c = cext[tt:tt + HALO_C, :]
    aext[0:HALO_A, :] = new_halo_a
    cext[0:HALO_C, :] = new_halo_c
    halo_a_out[0] = new_halo_a
    halo_c_out[0] = new_halo_c

    h_new = h + jnp.dot(ybuf[...], w_out_ref[...], preferred_element_type=jnp.float32)
    hout_ref[...] = h_new

    logits = _router_logits(_rms(h_new, g_ffn_ref[...]), wr_hi_ref, wr_lo_ref, br_ref)
    lane = lax.broadcasted_iota(jnp.int32, logits.shape, 1)
    neg = jnp.float32(-jnp.inf)
    big = jnp.int32(1 << 20)
    glog = jnp.where(lane < N_GROUPS, logits, neg)
    gmax = jnp.max(glog, axis=-1, keepdims=True)
    g_idx = jnp.min(jnp.where(glog == gmax, lane, big), axis=-1, keepdims=True)
    el = lane - N_GROUPS
    in_group = (el >= 0) & (el < N_EXPERTS) & ((el >> 3) == g_idx)
    l1 = jnp.where(in_group, logits, neg)
    m1 = jnp.max(l1, axis=-1, keepdims=True)
    i1 = jnp.min(jnp.where(l1 == m1, el, big), axis=-1, keepdims=True)
    l2 = jnp.where(el == i1, neg, l1)
    m2 = jnp.max(l2, axis=-1, keepdims=True)
    i2 = jnp.min(jnp.where(l2 == m2, el, big), axis=-1, keepdims=True)
    lo_e = jnp.minimum(i1, i2) & (EXPERTS_PER_GROUP - 1)
    hi_e = jnp.maximum(i1, i2) & (EXPERTS_PER_GROUP - 1)
    cls_ref[...] = g_idx * 64 + lo_e * 8 + hi_e


def _mixer_call(h_all, halo_a, halo_c, wts, *, row0, n_batch, n_tiles, tt):
    n_all = h_all.shape[0]
    blk0 = row0 // tt
    const = lambda shape: pl.BlockSpec(shape, lambda b, j: (0,) * len(shape))
    h_spec = pl.BlockSpec((tt, D_MODEL), lambda b, j: (blk0 + b * n_tiles + j, 0))
    n_rows = n_batch * n_tiles * tt
    return pl.pallas_call(
        functools.partial(_mixer_kernel, tt=tt),
        grid=(n_batch, n_tiles),
        in_specs=[
            h_spec,
            const((HALO_A, CONF_W)), const((HALO_C, SC_W)),
            const((1, D_MODEL)), const((D_MODEL, PROJ_W)),
            const((CONF_KERNEL, CONF_W)), const((1, CONF_W)), const((1, CONF_W)), const((1, CONF_W)),
            const((SC_KERNEL, SC_W)), const((CONF_W + SC_W, D_MODEL)), const((1, D_MODEL)),
            const((D_MODEL, ROUTER_COLS)), const((D_MODEL, ROUTER_COLS)), const((1, ROUTER_COLS)),
        ],
        out_specs=[
            h_spec,
            pl.BlockSpec((tt, 1), lambda b, j: (b * n_tiles + j, 0)),
            pl.BlockSpec((1, HALO_A, CONF_W), lambda b, j: (b, 0, 0)),
            pl.BlockSpec((1, HALO_C, SC_W), lambda b, j: (b, 0, 0)),
        ],
        out_shape=[
            jax.ShapeDtypeStruct((n_all, D_MODEL), jnp.float32),
            jax.ShapeDtypeStruct((n_rows, 1), jnp.int32),
            jax.ShapeDtypeStruct((n_batch, HALO_A, CONF_W), jnp.float32),
            jax.ShapeDtypeStruct((n_batch, HALO_C, SC_W), jnp.float32),
        ],
        scratch_shapes=[
            pltpu.VMEM((HALO_A + tt, CONF_W), jnp.float32),
            pltpu.VMEM((HALO_C + tt, SC_W), jnp.float32),
            pltpu.VMEM((tt, SC_W), jnp.float32),
            pltpu.VMEM((tt, CONF_W + SC_W), jnp.bfloat16),
        ],
        input_output_aliases={0: 0},
        compiler_params=pltpu.CompilerParams(
            dimension_semantics=("arbitrary", "arbitrary"), vmem_limit_bytes=VMEM_LIMIT),
        name=f"mixer_t{tt}",
    )(h_all, halo_a, halo_c, *wts)


def _wait_rows(src, dst, sem, n):
    n8 = pl.multiple_of((n >> 3) << 3, 8)

    @pl.when(n8 > 0)
    def _():
        pltpu.make_async_copy(src.at[pl.ds(0, n8), :], dst.at[pl.ds(0, n8), :], sem).wait()

    def one(r, c):
        pltpu.make_async_copy(src.at[pl.ds(0, 1), :], dst.at[pl.ds(0, 1), :], sem).wait()
        return c

    lax.fori_loop(0, n & 7, one, 0)


def _moe_kernel(ea_ref, eb_ref, nv_ref, nblk_ref,
                tok_ref, h_in, g_ffn_ref, wr_hi_ref, wr_lo_ref, br_ref,
                wg_a, wu_a, wd_a, wg_b, wu_b, wd_b,
                h_out,
                xbuf, obuf, wga_c, wua_c, wda_c, wgb_c, wub_c, wdb_c, sems, prev):
    i = pl.program_id(0)

    @pl.when(i == 0)
    def _():
        xbuf[...] = jnp.zeros_like(xbuf)
        prev[0] = -1
        prev[1] = -1

    @pl.when(i < nblk_ref[0])
    def _():
        ea = ea_ref[i]
        eb = eb_ref[i]
        nv = nv_ref[i]

        def row_gather(r):
            return pltpu.make_async_copy(
                h_in.at[pl.ds(tok_ref[0, 0, r], 1), :], xbuf.at[pl.ds(r, 1), :], sems.at[0])

        def row_scatter(r):
            return pltpu.make_async_copy(
                obuf.at[pl.ds(r, 1), :], h_out.at[pl.ds(tok_ref[0, 0, r], 1), :], sems.at[1])

        def start_gather(r, c):
            row_gather(r).start()
            return c

        lax.fori_loop(0, nv, start_gather, 0)

        @pl.when(ea != prev[0])
        def _():
            wga_c[...] = wg_a[0, 0].astype(jnp.bfloat16)
            wua_c[...] = wu_a[0, 0].astype(jnp.bfloat16)
            wda_c[...] = wd_a[0, 0].astype(jnp.bfloat16)
            prev[0] = ea

        @pl.when(eb != prev[1])
        def _():
            wgb_c[...] = wg_b[0, 0].astype(jnp.bfloat16)
            wub_c[...] = wu_b[0, 0].astype(jnp.bfloat16)
            wdb_c[...] = wd_b[0, 0].astype(jnp.bfloat16)
            prev[1] = eb

        _wait_rows(h_in, xbuf, sems.at[0], nv)

        x = xbuf[...]
        hn = _rms(x, g_ffn_ref[...])
        logits = _router_logits(hn, wr_hi_ref, wr_lo_ref, br_ref)
        lane = lax.broadcasted_iota(jnp.int32, logits.shape, 1)
        neg = jnp.float32(-jnp.inf)
        glog = jnp.where(lane < N_GROUPS, logits, neg)
        gmax = jnp.max(glog, axis=-1, keepdims=True)
        gsum = jnp.sum(jnp.exp(glog - gmax), axis=-1, keepdims=True)
        grp = ea >> 3
        l_g = jnp.sum(jnp.where(lane == grp, logits, 0.0), axis=-1, keepdims=True)
        p_g = jnp.exp(l_g - gmax) / gsum
        l_a = jnp.sum(jnp.where(lane == N_GROUPS + ea, logits, 0.0), axis=-1, keepdims=True)
        l_b = jnp.sum(jnp.where(lane == N_GROUPS + eb, logits, 0.0), axis=-1, keepdims=True)
        w_a = p_g / (1.0 + jnp.exp(l_b - l_a))
        w_b = p_g / (1.0 + jnp.exp(l_a - l_b))

        xb = hn.astype(jnp.bfloat16)

        def expert(wg_c, wu_c, wd_c):
            gate = jnp.dot(xb, wg_c[...], preferred_element_type=jnp.float32)
            up = jnp.dot(xb, wu_c[...], preferred_element_type=jnp.float32)
            hid = (jax.nn.silu(gate) * up).astype(jnp.bfloat16)
            return jnp.dot(hid, wd_c[...], preferred_element_type=jnp.float32)

        y = w_a * expert(wga_c, wua_c, wda_c) + w_b * expert(wgb_c, wub_c, wdb_c)
        obuf[...] = x + y

        def start_scatter(r, c):
            row_scatter(r).start()
            return c

        lax.fori_loop(0, nv, start_scatter, 0)
        _wait_rows(obuf, h_out, sems.at[1], nv)


def _moe_call(h_all, tok, ea, eb, nv, nblk, g_ffn, wr_hi, wr_lo, br, w_gate, w_up, w_down, *, layer):
    n_blocks = tok.shape[0]
    const = lambda shape: pl.BlockSpec(shape, lambda i, *_: (0,) * len(shape))
    w_in_spec = lambda which: pl.BlockSpec(
        (1, 1, D_MODEL, D_EXPERT), lambda i, ea_, eb_, nv_, nb_: (layer, (ea_, eb_)[which][i], 0, 0))
    w_out_spec = lambda which: pl.BlockSpec(
        (1, 1, D_EXPERT, D_MODEL), lambda i, ea_, eb_, nv_, nb_: (layer, (ea_, eb_)[which][i], 0, 0))
    grid_spec = pltpu.PrefetchScalarGridSpec(
        num_scalar_prefetch=4,
        grid=(n_blocks,),
        in_specs=[
            pl.BlockSpec((1, 1, MOE_ROWS), lambda i, *_: (i, 0, 0), memory_space=pltpu.SMEM),
            pl.BlockSpec(memory_space=pl.ANY),
            const((1, D_MODEL)), const((D_MODEL, ROUTER_COLS)), const((D_MODEL, ROUTER_COLS)),
            const((1, ROUTER_COLS)),
            w_in_spec(0), w_in_spec(0), w_out_spec(0),
            w_in_spec(1), w_in_spec(1), w_out_spec(1),
        ],
        out_specs=pl.BlockSpec(memory_space=pl.ANY),
        scratch_shapes=[
            pltpu.VMEM((MOE_ROWS, D_MODEL), jnp.float32),
            pltpu.VMEM((MOE_ROWS, D_MODEL), jnp.float32),
            pltpu.VMEM((D_MODEL, D_EXPERT), jnp.bfloat16),
            pltpu.VMEM((D_MODEL, D_EXPERT), jnp.bfloat16),
            pltpu.VMEM((D_EXPERT, D_MODEL), jnp.bfloat16),
            pltpu.VMEM((D_MODEL, D_EXPERT), jnp.bfloat16),
            pltpu.VMEM((D_MODEL, D_EXPERT), jnp.bfloat16),
            pltpu.VMEM((D_EXPERT, D_MODEL), jnp.bfloat16),
            pltpu.SemaphoreType.DMA((2,)),
            pltpu.SMEM((2,), jnp.int32),
        ],
    )
    return pl.pallas_call(
        _moe_kernel,
        grid_spec=grid_spec,
        out_shape=jax.ShapeDtypeStruct(h_all.shape, h_all.dtype),
        input_output_aliases={5: 0},
        compiler_params=pltpu.CompilerParams(
            dimension_semantics=("arbitrary",), vmem_limit_bytes=VMEM_LIMIT),
        name="moe_pairs",
    )(ea, eb, nv, nblk, tok, h_all, g_ffn, wr_hi, wr_lo, br,
      w_gate, w_up, w_down, w_gate, w_up, w_down)


def _final_norm_kernel(h_ref, g_ref, o_ref):
    o_ref[...] = _rms(h_ref[...], g_ref[...])


def _final_norm_call(h_all, g, n_rows):
    return pl.pallas_call(
        _final_norm_kernel,
        grid=(n_rows // MIX_TILE,),
        in_specs=[pl.BlockSpec((MIX_TILE, D_MODEL), lambda i: (i, 0)),
                  pl.BlockSpec((1, D_MODEL), lambda i: (0, 0))],
        out_specs=pl.BlockSpec((MIX_TILE, D_MODEL), lambda i: (i, 0)),
        out_shape=jax.ShapeDtypeStruct((n_rows, D_MODEL), jnp.float32),
        compiler_params=pltpu.CompilerParams(dimension_semantics=("arbitrary",)),
        name="final_norm",
    )(h_all, g)


def _dispatch_tables(cls, n_blocks):
    n_all = cls.shape[0]
    order = jnp.argsort(cls).astype(jnp.int32)
    sorted_cls = cls[order]
    ids = jnp.arange(N_CLASS_IDS, dtype=jnp.int32)
    start = jnp.searchsorted(sorted_cls, ids, side="left").astype(jnp.int32)
    counts = jnp.searchsorted(sorted_cls, ids, side="right").astype(jnp.int32) - start
    padded = (counts + MOE_ROWS - 1) // MOE_ROWS * MOE_ROWS
    pad_end = jnp.cumsum(padded)
    pad_start = pad_end - padded
    nblk = (pad_end[-1] // MOE_ROWS).astype(jnp.int32)

    blk = jnp.minimum(jnp.arange(n_blocks, dtype=jnp.int32), nblk - 1)
    p0 = blk * MOE_ROWS
    cb = jnp.minimum(jnp.searchsorted(pad_end, p0, side="right"), N_CLASS_IDS - 1).astype(jnp.int32)
    r0 = p0 - pad_start[cb]
    nv = jnp.clip(counts[cb] - r0, 0, MOE_ROWS).astype(jnp.int32)
    grp = cb >> 6
    ea = grp * EXPERTS_PER_GROUP + ((cb >> 3) & 7)
    eb = grp * EXPERTS_PER_GROUP + (cb & 7)

    r = jnp.arange(MOE_ROWS, dtype=jnp.int32)[None, :]
    src = jnp.clip(start[cb][:, None] + r0[:, None] + r, 0, n_all - 1)
    tok = jnp.where(r < nv[:, None], order[src], 0).astype(jnp.int32)
    return tok.reshape(n_blocks, 1, MOE_ROWS), ea, eb, nv, nblk.reshape(1)


def kernel(x, meta_tokens, norm_mix_g, w_in, conf_dw_w, conf_dw_b, conf_ln_g, conf_ln_b, sc_conv_w, w_out,
           norm_ffn_g, w_router_group, b_router_group, w_router_expert, b_router_expert, w_exp_gate,
           w_exp_up, w_exp_down, final_norm_g):
    bsz, seq, d = x.shape
    depth = w_in.shape[0]
    n_main = bsz * seq
    n_all = n_main + N_META
    n_blocks = n_all // MOE_ROWS + N_CLASSES
    f32 = jnp.float32

    h_all = jnp.concatenate([x.reshape(n_main, d), meta_tokens.astype(x.dtype)], axis=0)

    for l in range(depth):
        w_r = jnp.concatenate([w_router_group[l], w_router_expert[l]], axis=1).astype(f32)
        w_r = jnp.pad(w_r, ((0, 0), (0, ROUTER_COLS - w_r.shape[1])))
        wr_hi = w_r.astype(jnp.bfloat16)
        wr_lo = (w_r - wr_hi.astype(f32)).astype(jnp.bfloat16)
        b_r = jnp.concatenate([b_router_group[l], b_router_expert[l]]).astype(f32)
        b_r = jnp.pad(b_r, (0, ROUTER_COLS - b_r.shape[0])).reshape(1, ROUTER_COLS)
        g_ffn = norm_ffn_g[l].reshape(1, d)
        wts = (norm_mix_g[l].reshape(1, d), w_in[l].astype(jnp.bfloat16), conf_dw_w[l],
               conf_dw_b[l].reshape(1, CONF_W), conf_ln_g[l].reshape(1, CONF_W),
               conf_ln_b[l].reshape(1, CONF_W), sc_conv_w[l], w_out[l].astype(jnp.bfloat16),
               g_ffn, wr_hi, wr_lo, b_r)

        h_all, cls_meta, halo_a, halo_c = _mixer_call(
            h_all, jnp.zeros((HALO_A, CONF_W), f32), jnp.zeros((HALO_C, SC_W), f32), wts,
            row0=n_main, n_batch=1, n_tiles=1, tt=N_META)
        h_all, cls_main, _, _ = _mixer_call(
            h_all, halo_a[0], halo_c[0], wts, row0=0, n_batch=bsz, n_tiles=seq // MIX_TILE, tt=MIX_TILE)

        cls = jnp.concatenate([cls_main[:, 0], cls_meta[:, 0]])
        tok, ea, eb, nv, nblk = _dispatch_tables(cls, n_blocks)
        h_all = _moe_call(h_all, tok, ea, eb, nv, nblk, g_ffn, wr_hi, wr_lo, b_r,
                          w_exp_gate, w_exp_up, w_exp_down, layer=l)

    out = _final_norm_call(h_all, final_norm_g.reshape(1, d), n_main)
    return out.reshape(bsz, seq, d)
```

```python
import functools

import jax
import jax.numpy as jnp
from jax import lax
from jax.experimental import pallas as pl
from jax.experimental.pallas import tpu as pltpu

D_MODEL = 1024
N_META = 16
CONF_W = 512
SC_W = 512
PROJ_W = 2 * CONF_W + 3 * SC_W
CONF_KERNEL = 31
SC_KERNEL = 3
N_GROUPS = 4
EXPERTS_PER_GROUP = 8
N_EXPERTS = N_GROUPS * EXPERTS_PER_GROUP
D_EXPERT = 512
EPS = 1e-6

LANES = 128
HALO_A = 32
HALO_C = 8
CONV_ROWS = 32
MIX_TILE = 512
MOE_ROWS = 256
N_CLASS_IDS = N_GROUPS * 64
N_CLASSES = N_GROUPS * (EXPERTS_PER_GROUP * (EXPERTS_PER_GROUP - 1) // 2)
VMEM_LIMIT = 56 * 1024 * 1024
ROUTER_COLS = LANES
PAD_BANKS = 4
SINK_BANK = PAD_BANKS
SOURCE_BANK = PAD_BANKS + 1
SPARE_ROWS = (PAD_BANKS + 2) * MOE_ROWS


def _rms(x, g):
    return x * lax.rsqrt(jnp.mean(x * x, axis=-1, keepdims=True) + EPS) * g


def _router_logits(hn, w_hi_ref, w_lo_ref, b_ref):
    hi = hn.astype(jnp.bfloat16)
    lo = (hn - hi.astype(jnp.float32)).astype(jnp.bfloat16)
    acc = jnp.dot(hi, w_hi_ref[...], preferred_element_type=jnp.float32)
    acc = acc + jnp.dot(lo, w_hi_ref[...], preferred_element_type=jnp.float32)
    acc = acc + jnp.dot(hi, w_lo_ref[...], preferred_element_type=jnp.float32)
    return acc + b_ref[...]


def _mixer_kernel(h_ref, halo_a_ref, halo_c_ref, g_mix_ref, w_in_ref, dw_w_ref, dw_b_ref, ln_g_ref,
                  ln_b_ref, sc_w_ref, w_out_ref, g_ffn_ref, wr_hi_ref, wr_lo_ref, br_ref,
                  hout_ref, cls_ref, halo_a_out, halo_c_out,
                  aext, cext, sb_buf, ybuf, *, tt):
    j = pl.program_id(1)

    @pl.when(j == 0)
    def _():
        aext[0:HALO_A, :] = halo_a_ref[...]
        cext[0:HALO_C, :] = halo_c_ref[...]

    h = h_ref[...]
    xn = _rms(h, g_mix_ref[...]).astype(jnp.bfloat16)
    ug = jnp.dot(xn, w_in_ref[:, 0:2 * CONF_W], preferred_element_type=jnp.float32)
    aext[HALO_A:HALO_A + tt, :] = ug[:, 0:CONF_W] * jax.nn.sigmoid(ug[:, CONF_W:2 * CONF_W])
    bcv = jnp.dot(xn, w_in_ref[:, 2 * CONF_W:PROJ_W], preferred_element_type=jnp.float32)
    sb_buf[...] = bcv[:, 0:SC_W]
    cext[HALO_C:HALO_C + tt, :] = bcv[:, SC_W:2 * SC_W] * bcv[:, 2 * SC_W:3 * SC_W]

    ch = min(tt, CONV_ROWS)
    for c0 in range(0, tt, ch):
        acc = jnp.broadcast_to(dw_b_ref[...], (ch, CONF_W))
        for k in range(CONF_KERNEL):
            acc = acc + aext[pl.ds(HALO_A - (CONF_KERNEL - 1) + c0 + k, ch), :] * dw_w_ref[k:k + 1, :]
        mu = jnp.mean(acc, axis=-1, keepdims=True)
        xc = acc - mu
        ln = xc * lax.rsqrt(jnp.mean(xc * xc, axis=-1, keepdims=True) + EPS) * ln_g_ref[...] + ln_b_ref[...]
        ybuf[c0:c0 + ch, 0:CONF_W] = jax.nn.silu(ln).astype(jnp.bfloat16)
        sacc = cext[pl.ds(HALO_C - (SC_KERNEL - 1) + c0, ch), :] * sc_w_ref[0:1, :]
        for k in range(1, SC_KERNEL):
            sacc = sacc + cext[pl.ds(HALO_C - (SC_KERNEL - 1) + c0 + k, ch), :] * sc_w_ref[k:k + 1, :]
        ybuf[c0:c0 + ch, CONF_W:CONF_W + SC_W] = (sb_buf[c0:c0 + ch, :] * sacc).astype(jnp.bfloat16)

    new_halo_a = aext[tt:tt + HALO_A, :]
    new_halo_c = cext[tt:tt + HALO_C, :]
    aext[0:HALO_A, :] = new_halo_a
    cext[0:HALO_C, :] = new_halo_c
    halo_a_out[0] = new_halo_a
    halo_c_out[0] = new_halo_c

    h_new = h + jnp.dot(ybuf[...], w_out_ref[...], preferred_element_type=jnp.float32)
    hout_ref[...] = h_new

    logits = _router_logits(_rms(h_new, g_ffn_ref[...]), wr_hi_ref, wr_lo_ref, br_ref)
    lane = lax.broadcasted_iota(jnp.int32, logits.shape, 1)
    neg = jnp.float32(-jnp.inf)
    big = jnp.int32(1 << 20)
    glog = jnp.where(lane < N_GROUPS, logits, neg)
    gmax = jnp.max(glog, axis=-1, keepdims=True)
    g_idx = jnp.min(jnp.where(glog == gmax, lane, big), axis=-1, keepdims=True)
    el = lane - N_GROUPS
    in_group = (el >= 0) & (el < N_EXPERTS) & ((el >> 3) == g_idx)
    l1 = jnp.where(in_group, logits, neg)
    m1 = jnp.max(l1, axis=-1, keepdims=True)
    i1 = jnp.min(jnp.where(l1 == m1, el, big), axis=-1, keepdims=True)
    l2 = jnp.where(el == i1, neg, l1)
    m2 = jnp.max(l2, axis=-1, keepdims=True)
    i2 = jnp.min(jnp.where(l2 == m2, el, big), axis=-1, keepdims=True)
    lo_e = jnp.minimum(i1, i2) & (EXPERTS_PER_GROUP - 1)
    hi_e = jnp.maximum(i1, i2) & (EXPERTS_PER_GROUP - 1)
    cls_ref[...] = g_idx * 64 + lo_e * 8 + hi_e


def _mixer_call(h_all, halo_a, halo_c, wts, *, row0, n_batch, n_tiles, tt):
    n_all = h_all.shape[0]
    blk0 = row0 // tt
    const = lambda shape: pl.BlockSpec(shape, lambda b, j: (0,) * len(shape))
    h_spec = pl.BlockSpec((tt, D_MODEL), lambda b, j: (blk0 + b * n_tiles + j, 0))
    n_rows = n_batch * n_tiles * tt
    return pl.pallas_call(
        functools.partial(_mixer_kernel, tt=tt),
        grid=(n_batch, n_tiles),
        in_specs=[
            h_spec,
            const((HALO_A, CONF_W)), const((HALO_C, SC_W)),
            const((1, D_MODEL)), const((D_MODEL, PROJ_W)),
            const((CONF_KERNEL, CONF_W)), const((1, CONF_W)), const((1, CONF_W)), const((1, CONF_W)),
            const((SC_KERNEL, SC_W)), const((CONF_W + SC_W, D_MODEL)), const((1, D_MODEL)),
            const((D_MODEL, ROUTER_COLS)), const((D_MODEL, ROUTER_COLS)), const((1, ROUTER_COLS)),
        ],
        out_specs=[
            h_spec,
            pl.BlockSpec((tt, 1), lambda b, j: (b * n_tiles + j, 0)),
            pl.BlockSpec((1, HALO_A, CONF_W), lambda b, j: (b, 0, 0)),
            pl.BlockSpec((1, HALO_C, SC_W), lambda b, j: (b, 0, 0)),
        ],
        out_shape=[
            jax.ShapeDtypeStruct((n_all, D_MODEL), jnp.float32),
            jax.ShapeDtypeStruct((n_rows, 1), jnp.int32),
            jax.ShapeDtypeStruct((n_batch, HALO_A, CONF_W), jnp.float32),
            jax.ShapeDtypeStruct((n_batch, HALO_C, SC_W), jnp.float32),
        ],
        scratch_shapes=[
            pltpu.VMEM((HALO_A + tt, CONF_W), jnp.float32),
            pltpu.VMEM((HALO_C + tt, SC_W), jnp.float32),
            pltpu.VMEM((tt, SC_W), jnp.float32),
            pltpu.VMEM((tt, CONF_W + SC_W), jnp.bfloat16),
        ],
        input_output_aliases={0: 0},
        compiler_params=pltpu.CompilerParams(
            dimension_semantics=("arbitrary", "arbitrary"), vmem_limit_bytes=VMEM_LIMIT),
        name=f"mixer_t{tt}",
    )(h_all, halo_a, halo_c, *wts)


def _moe_kernel(ea_ref, eb_ref, nblk_ref,
                tok0_ref, gtok_ref, stok_ref, h_in, g_ffn_ref, wr_hi_ref, wr_lo_ref, br_ref,
                wg_a, wu_a, wd_a, wg_b, wu_b, wd_b,
                h_out,
                xbuf0, xbuf1, obuf0, obuf1, wga_c, wua_c, wda_c, wgb_c, wub_c, wdb_c, gsem, ssem, prev):
    i = pl.program_id(0)
    nblk = nblk_ref[0]
    xbuf = (xbuf0, xbuf1)
    obuf = (obuf0, obuf1)

    def gather_row(tok_ref, r, dst_slot):
        return pltpu.make_async_copy(
            h_in.at[pl.ds(tok_ref[0, 0, r], 1), :], xbuf[dst_slot].at[pl.ds(r, 1), :], gsem.at[dst_slot])

    def scatter_row(r, src_slot):
        return pltpu.make_async_copy(
            obuf[src_slot].at[pl.ds(r, 1), :], h_out.at[pl.ds(stok_ref[0, 0, r], 1), :], ssem.at[src_slot])

    def wait_gather(dst_slot):
        pltpu.make_async_copy(h_in.at[pl.ds(0, MOE_ROWS), :], xbuf[dst_slot], gsem.at[dst_slot]).wait()

    def wait_scatter(src_slot):
        pltpu.make_async_copy(obuf[src_slot], h_out.at[pl.ds(0, MOE_ROWS), :], ssem.at[src_slot]).wait()

    @pl.when(i == 0)
    def _():
        obuf1[...] = jnp.zeros_like(obuf1)
        prev[0] = -1
        prev[1] = -1

        def start(r, c):
            gather_row(tok0_ref, r, 0).start()
            return c

        lax.fori_loop(0, MOE_ROWS, start, 0)

    def step(slot):
        other = 1 - slot
        ea = ea_ref[i]
        eb = eb_ref[i]
        wait_gather(slot)

        @pl.when(i > 0)
        def _():
            wait_scatter(slot)

        @pl.when(ea != prev[0])
        def _():
            wga_c[...] = wg_a[0, 0].astype(jnp.bfloat16)
            wua_c[...] = wu_a[0, 0].astype(jnp.bfloat16)
            wda_c[...] = wd_a[0, 0].astype(jnp.bfloat16)
            prev[0] = ea

        @pl.when(eb != prev[1])
        def _():
            wgb_c[...] = wg_b[0, 0].astype(jnp.bfloat16)
            wub_c[...] = wu_b[0, 0].astype(jnp.bfloat16)
            wdb_c[...] = wd_b[0, 0].astype(jnp.bfloat16)
            prev[1] = eb

        for r in range(MOE_ROWS):
            gather_row(gtok_ref, r, other).start()
        for r in range(MOE_ROWS):
            scatter_row(r, other).start()

        x = xbuf[slot][...]
        hn = _rms(x, g_ffn_ref[...])
        logits = _router_logits(hn, wr_hi_ref, wr_lo_ref, br_ref)
        lane = lax.broadcasted_iota(jnp.int32, logits.shape, 1)
        neg = jnp.float32(-jnp.inf)
        glog = jnp.where(lane < N_GROUPS, logits, neg)
        gmax = jnp.max(glog, axis=-1, keepdims=True)
        gsum = jnp.sum(jnp.exp(glog - gmax), axis=-1, keepdims=True)
        grp = ea >> 3
        l_g = jnp.sum(jnp.where(lane == grp, logits, 0.0), axis=-1, keepdims=True)
        p_g = jnp.exp(l_g - gmax) / gsum
        l_a = jnp.sum(jnp.where(lane == N_GROUPS + ea, logits, 0.0), axis=-1, keepdims=True)
        l_b = jnp.sum(jnp.where(lane == N_GROUPS + eb, logits, 0.0), axis=-1, keepdims=True)
        w_a = p_g / (1.0 + jnp.exp(l_b - l_a))
        w_b = p_g / (1.0 + jnp.exp(l_a - l_b))

        xb = hn.astype(jnp.bfloat16)

        def expert(wg_c, wu_c, wd_c):
            gate = jnp.dot(xb, wg_c[...], preferred_element_type=jnp.float32)
            up = jnp.dot(xb, wu_c[...], preferred_element_type=jnp.float32)
            hid = (jax.nn.silu(gate) * up).astype(jnp.bfloat16)
            return jnp.dot(hid, wd_c[...], preferred_element_type=jnp.float32)

        y = w_a * expert(wga_c, wua_c, wda_c) + w_b * expert(wgb_c, wub_c, wdb_c)
        obuf[slot][...] = x + y

        @pl.when(i == nblk)
        def _():
            wait_gather(other)
            wait_scatter(other)

    for parity in (0, 1):
        pl.when((i <= nblk) & ((i & 1) == parity))(functools.partial(step, parity))


def _moe_call(h_all, tok0, gtok, stok, ea, eb, nblk, g_ffn, wr_hi, wr_lo, br, w_gate, w_up, w_down, *, layer):
    n_steps = gtok.shape[0]
    const = lambda shape: pl.BlockSpec(shape, lambda i, *_: (0,) * len(shape))
    w_in_spec = lambda which: pl.BlockSpec(
        (1, 1, D_MODEL, D_EXPERT), lambda i, ea_, eb_, nb_: (layer, (ea_, eb_)[which][i], 0, 0))
    w_out_spec = lambda which: pl.BlockSpec(
        (1, 1, D_EXPERT, D_MODEL), lambda i, ea_, eb_, nb_: (layer, (ea_, eb_)[which][i], 0, 0))
    tok_spec = lambda imap: pl.BlockSpec((1, 1, MOE_ROWS), imap, memory_space=pltpu.SMEM)
    grid_spec = pltpu.PrefetchScalarGridSpec(
        num_scalar_prefetch=3,
        grid=(n_steps,),
        in_specs=[
            tok_spec(lambda i, *_: (0, 0, 0)), tok_spec(lambda i, *_: (i, 0, 0)), tok_spec(lambda i, *_: (i, 0, 0)),
            pl.BlockSpec(memory_space=pl.ANY),
            const((1, D_MODEL)), const((D_MODEL, ROUTER_COLS)), const((D_MODEL, ROUTER_COLS)),
            const((1, ROUTER_COLS)),
            w_in_spec(0), w_in_spec(0), w_out_spec(0),
            w_in_spec(1), w_in_spec(1), w_out_spec(1),
        ],
        out_specs=pl.BlockSpec(memory_space=pl.ANY),
        scratch_shapes=[
            pltpu.VMEM((MOE_ROWS, D_MODEL), jnp.float32),
            pltpu.VMEM((MOE_ROWS, D_MODEL), jnp.float32),
            pltpu.VMEM((MOE_ROWS, D_MODEL), jnp.float32),
            pltpu.VMEM((MOE_ROWS, D_MODEL), jnp.float32),
            pltpu.VMEM((D_MODEL, D_EXPERT), jnp.bfloat16),
            pltpu.VMEM((D_MODEL, D_EXPERT), jnp.bfloat16),
            pltpu.VMEM((D_EXPERT, D_MODEL), jnp.bfloat16),
            pltpu.VMEM((D_MODEL, D_EXPERT), jnp.bfloat16),
            pltpu.VMEM((D_MODEL, D_EXPERT), jnp.bfloat16),
            pltpu.VMEM((D_EXPERT, D_MODEL), jnp.bfloat16),
            pltpu.SemaphoreType.DMA((2,)),
            pltpu.SemaphoreType.DMA((2,)),
            pltpu.SMEM((2,), jnp.int32),
        ],
    )
    return pl.pallas_call(
        _moe_kernel,
        grid_spec=grid_spec,
        out_shape=jax.ShapeDtypeStruct(h_all.shape, h_all.dtype),
        input_output_aliases={6: 0},
        compiler_params=pltpu.CompilerParams(
            dimension_semantics=("arbitrary",), vmem_limit_bytes=VMEM_LIMIT),
        name="moe_pairs",
    )(ea, eb, nblk, tok0, gtok, stok, h_all, g_ffn, wr_hi, wr_lo, br,
      w_gate, w_up, w_down, w_gate, w_up, w_down)


def _final_norm_kernel(h_ref, g_ref, o_ref):
    o_ref[...] = _rms(h_ref[...], g_ref[...])


def _final_norm_call(h_all, g, n_rows):
    return pl.pallas_call(
        _final_norm_kernel,
        grid=(n_rows // MIX_TILE,),
        in_specs=[pl.BlockSpec((MIX_TILE, D_MODEL), lambda i: (i, 0)),
                  pl.BlockSpec((1, D_MODEL), lambda i: (0, 0))],
        out_specs=pl.BlockSpec((MIX_TILE, D_MODEL), lambda i: (i, 0)),
        out_shape=jax.ShapeDtypeStruct((n_rows, D_MODEL), jnp.float32),
        compiler_params=pltpu.CompilerParams(dimension_semantics=("arbitrary",)),
        name="final_norm",
    )(h_all, g)


def _dispatch_tables(cls, n_blocks):
    n_all = cls.shape[0]
    order = jnp.argsort(cls).astype(jnp.int32)
    sorted_cls = cls[order]
    ids = jnp.arange(N_CLASS_IDS, dtype=jnp.int32)
    start = jnp.searchsorted(sorted_cls, ids, side="left").astype(jnp.int32)
    counts = jnp.searchsorted(sorted_cls, ids, side="right").astype(jnp.int32) - start
    padded = (counts + MOE_ROWS - 1) // MOE_ROWS * MOE_ROWS
    pad_end = jnp.cumsum(padded)
    pad_start = pad_end - padded
    nblk = (pad_end[-1] // MOE_ROWS).astype(jnp.int32)

    step = jnp.arange(n_blocks + 1, dtype=jnp.int32)
    blk = jnp.minimum(step, nblk - 1)
    p0 = blk * MOE_ROWS
    cb = jnp.minimum(jnp.searchsorted(pad_end, p0, side="right"), N_CLASS_IDS - 1).astype(jnp.int32)
    r0 = p0 - pad_start[cb]
    nv = jnp.clip(counts[cb] - r0, 0, MOE_ROWS).astype(jnp.int32)
    grp = cb >> 6
    ea = grp * EXPERTS_PER_GROUP + ((cb >> 3) & 7)
    eb = grp * EXPERTS_PER_GROUP + (cb & 7)

    r = jnp.arange(MOE_ROWS, dtype=jnp.int32)[None, :]
    src = jnp.clip(start[cb][:, None] + r0[:, None] + r, 0, n_all - 1)
    spare = n_all + (step[:, None] & (PAD_BANKS - 1)) * MOE_ROWS + r
    rows = jnp.where(r < nv[:, None], order[src], spare).astype(jnp.int32)
    sink = n_all + SINK_BANK * MOE_ROWS + r
    source = n_all + SOURCE_BANK * MOE_ROWS + r
    nxt = jnp.concatenate([rows[1:], rows[-1:]], axis=0)
    prv = jnp.concatenate([rows[:1], rows[:-1]], axis=0)
    gtok = jnp.where((step + 1 < nblk)[:, None], nxt, source)
    stok = jnp.where(((step >= 1) & (step <= nblk))[:, None], prv, sink)
    shape3 = (n_blocks + 1, 1, MOE_ROWS)
    return (rows[:1].reshape(1, 1, MOE_ROWS), gtok.reshape(shape3), stok.reshape(shape3), ea, eb,
            nblk.reshape(1))


def kernel(x, meta_tokens, norm_mix_g, w_in, conf_dw_w, conf_dw_b, conf_ln_g, conf_ln_b, sc_conv_w, w_out,
           norm_ffn_g, w_router_group, b_router_group, w_router_expert, b_router_expert, w_exp_gate,
           w_exp_up, w_exp_down, final_norm_g):
    bsz, seq, d = x.shape
    depth = w_in.shape[0]
    n_main = bsz * seq
    n_all = n_main + N_META
    n_blocks = n_all // MOE_ROWS + N_CLASSES
    f32 = jnp.float32

    h_all = jnp.concatenate(
        [x.reshape(n_main, d), meta_tokens.astype(x.dtype), jnp.zeros((SPARE_ROWS, d), x.dtype)], axis=0)

    for l in range(depth):
        w_r = jnp.concatenate([w_router_group[l], w_router_expert[l]], axis=1).astype(f32)
        w_r = jnp.pad(w_r, ((0, 0), (0, ROUTER_COLS - w_r.shape[1])))
        wr_hi = w_r.astype(jnp.bfloat16)
        wr_lo = (w_r - wr_hi.astype(f32)).astype(jnp.bfloat16)
        b_r = jnp.concatenate([b_router_group[l], b_router_expert[l]]).astype(f32)
        b_r = jnp.pad(b_r, (0, ROUTER_COLS - b_r.shape[0])).reshape(1, ROUTER_COLS)
        g_ffn = norm_ffn_g[l].reshape(1, d)
        wts = (norm_mix_g[l].reshape(1, d), w_in[l].astype(jnp.bfloat16), conf_dw_w[l],
               conf_dw_b[l].reshape(1, CONF_W), conf_ln_g[l].reshape(1, CONF_W),
               conf_ln_b[l].reshape(1, CONF_W), sc_conv_w[l], w_out[l].astype(jnp.bfloat16),
               g_ffn, wr_hi, wr_lo, b_r)

        h_all, cls_meta, halo_a, halo_c = _mixer_call(
            h_all, jnp.zeros((HALO_A, CONF_W), f32), jnp.zeros((HALO_C, SC_W), f32), wts,
            row0=n_main, n_batch=1, n_tiles=1, tt=N_META)
        h_all, cls_main, _, _ = _mixer_call(
            h_all, halo_a[0], halo_c[0], wts, row0=0, n_batch=bsz, n_tiles=seq // MIX_TILE, tt=MIX_TILE)

        cls = jnp.concatenate([cls_main[:, 0], cls_meta[:, 0]])
        tok0, gtok, stok, ea, eb, nblk = _dispatch_tables(cls, n_blocks)
        h_all = _moe_call(h_all, tok0, gtok, stok, ea, eb, nblk, g_ffn, wr_hi, wr_lo, b_r,
                          w_exp_gate, w_exp_up, w_exp_down, layer=l)

    out = _final_norm_call(h_all, final_norm_g.reshape(1, d), n_main)
    return out.reshape(bsz, seq, d)
```

```python
import functools

import jax
import jax.numpy as jnp
from jax import lax
from jax.experimental import pallas as pl
from jax.experimental.pallas import tpu as pltpu

D_MODEL = 1024
N_META = 16
CONF_W = 512
SC_W = 512
PROJ_W = 2 * CONF_W + 3 * SC_W
CONF_KERNEL = 31
SC_KERNEL = 3
N_GROUPS = 4
EXPERTS_PER_GROUP = 8
N_EXPERTS = N_GROUPS * EXPERTS_PER_GROUP
D_EXPERT = 512
EPS = 1e-6

LANES = 128
SUBLANES = 8
HALO_A = 32
HALO_C = 8
CONV_ROWS = 32
MIX_TILE = 512
MOE_ROWS = 256
N_CLASS_IDS = N_GROUPS * 64
N_CLASSES = N_GROUPS * (EXPERTS_PER_GROUP * (EXPERTS_PER_GROUP - 1) // 2)
VMEM_LIMIT = 56 * 1024 * 1024
ROUTER_COLS = LANES
PAD_BANKS = 4
SINK_BANK = PAD_BANKS
SOURCE_BANK = PAD_BANKS + 1
SPARE_ROWS = (PAD_BANKS + 2) * MOE_ROWS


def _rms(x, g):
    return x * lax.rsqrt(jnp.mean(x * x, axis=-1, keepdims=True) + EPS) * g


def _router_logits(hn, w_hi_ref, w_lo_ref, b_ref):
    hi = hn.astype(jnp.bfloat16)
    lo = (hn - hi.astype(jnp.float32)).astype(jnp.bfloat16)
    acc = jnp.dot(hi, w_hi_ref[...], preferred_element_type=jnp.float32)
    acc = acc + jnp.dot(lo, w_hi_ref[...], preferred_element_type=jnp.float32)
    acc = acc + jnp.dot(hi, w_lo_ref[...], preferred_element_type=jnp.float32)
    return acc + b_ref[...]


def _mixer_kernel(h_ref, halo_a_ref, halo_c_ref, g_mix_ref, w_in_ref, dw_w_ref, dw_b_ref, ln_g_ref,
                  ln_b_ref, sc_w_ref, w_out_ref, g_ffn_ref, wr_hi_ref, wr_lo_ref, br_ref,
                  hout_ref, cls_ref, halo_a_out, halo_c_out,
                  aext, ashift, cext, sb_buf, ybuf, *, tt):
    j = pl.program_id(1)

    @pl.when(j == 0)
    def _():
        aext[0:HALO_A, :] = halo_a_ref[...]
        cext[0:HALO_C, :] = halo_c_ref[...]

    h = h_ref[...]
    xn = _rms(h, g_mix_ref[...]).astype(jnp.bfloat16)
    ug = jnp.dot(xn, w_in_ref[:, 0:2 * CONF_W], preferred_element_type=jnp.float32)
    aext[HALO_A:HALO_A + tt, :] = ug[:, 0:CONF_W] * jax.nn.sigmoid(ug[:, CONF_W:2 * CONF_W])
    bcv = jnp.dot(xn, w_in_ref[:, 2 * CONF_W:PROJ_W], preferred_element_type=jnp.float32)
    sb_buf[...] = bcv[:, 0:SC_W]
    cext[HALO_C:HALO_C + tt, :] = bcv[:, SC_W:2 * SC_W] * bcv[:, 2 * SC_W:3 * SC_W]

    n_shift_rows = HALO_A + tt - SUBLANES
    for p in range(1, SUBLANES):
        ashift[p - 1, 0:n_shift_rows, :] = aext[p:p + n_shift_rows, :]

    ch = min(tt, CONV_ROWS)
    tiles = (ch // SUBLANES, SUBLANES, CONF_W)
    for c0 in range(0, tt, ch):
        acc = jnp.broadcast_to(dw_b_ref[...], tiles)
        for k in range(CONF_KERNEL):
            row = HALO_A - (CONF_KERNEL - 1) + c0 + k
            phase = row % SUBLANES
            if phase == 0:
                tap = aext[row:row + ch, :]
            else:
                tap = ashift[phase - 1, row - phase:row - phase + ch, :]
            acc = acc + tap.reshape(tiles) * dw_w_ref[k]
        acc = acc.reshape(ch, CONF_W)
        mu = jnp.mean(acc, axis=-1, keepdims=True)
        xc = acc - mu
        ln = xc * lax.rsqrt(jnp.mean(xc * xc, axis=-1, keepdims=True) + EPS) * ln_g_ref[...] + ln_b_ref[...]
        ybuf[c0:c0 + ch, 0:CONF_W] = jax.nn.silu(ln).astype(jnp.bfloat16)
        sacc = None
        for k in range(SC_KERNEL):
            row = HALO_C - (SC_KERNEL - 1) + c0 + k
            term = cext[row:row + ch, :].reshape(tiles) * sc_w_ref[k]
            sacc = term if sacc is None else sacc + term
        ybuf[c0:c0 + ch, CONF_W:CONF_W + SC_W] = (
            sb_buf[c0:c0 + ch, :] * sacc.reshape(ch, SC_W)).astype(jnp.bfloat16)

    new_halo_a = aext[tt:tt + HALO_A, :]
    new_halo_c = cext[tt:tt + HALO_C, :]
    aext[0:HALO_A, :] = new_halo_a
    cext[0:HALO_C, :] = new_halo_c
    halo_a_out[0] = new_halo_a
    halo_c_out[0] = new_halo_c

    h_new = h + jnp.dot(ybuf[...], w_out_ref[...], preferred_element_type=jnp.float32)
    hout_ref[...] = h_new

    logits = _router_logits(_rms(h_new, g_ffn_ref[...]), wr_hi_ref, wr_lo_ref, br_ref)
    lane = lax.broadcasted_iota(jnp.int32, logits.shape, 1)
    neg = jnp.float32(-jnp.inf)
    big = jnp.int32(1 << 20)
    glog = jnp.where(lane < N_GROUPS, logits, neg)
    gmax = jnp.max(glog, axis=-1, keepdims=True)
    g_idx = jnp.min(jnp.where(glog == gmax, lane, big), axis=-1, keepdims=True)
    el = lane - N_GROUPS
    in_group = (el >= 0) & (el < N_EXPERTS) & ((el >> 3) == g_idx)
    l1 = jnp.where(in_group, logits, neg)
    m1 = jnp.max(l1, axis=-1, keepdims=True)
    i1 = jnp.min(jnp.where(l1 == m1, el, big), axis=-1, keepdims=True)
    l2 = jnp.where(el == i1, neg, l1)
    m2 = jnp.max(l2, axis=-1, keepdims=True)
    i2 = jnp.min(jnp.where(l2 == m2, el, big), axis=-1, keepdims=True)
    lo_e = jnp.minimum(i1, i2) & (EXPERTS_PER_GROUP - 1)
    hi_e = jnp.maximum(i1, i2) & (EXPERTS_PER_GROUP - 1)
    cls_ref[...] = g_idx * 64 + lo_e * 8 + hi_e


def _mixer_call(h_all, halo_a, halo_c, wts, *, row0, n_batch, n_tiles, tt):
    n_all = h_all.shape[0]
    blk0 = row0 // tt
    const = lambda shape: pl.BlockSpec(shape, lambda b, j: (0,) * len(shape))
    h_spec = pl.BlockSpec((tt, D_MODEL), lambda b, j: (blk0 + b * n_tiles + j, 0))
    n_rows = n_batch * n_tiles * tt
    return pl.pallas_call(
        functools.partial(_mixer_kernel, tt=tt),
        grid=(n_batch, n_tiles),
        in_specs=[
            h_spec,
            const((HALO_A, CONF_W)), const((HALO_C, SC_W)),
            const((1, D_MODEL)), const((D_MODEL, PROJ_W)),
            const((CONF_KERNEL, SUBLANES, CONF_W)), const((1, CONF_W)), const((1, CONF_W)), const((1, CONF_W)),
            const((SC_KERNEL, SUBLANES, SC_W)), const((CONF_W + SC_W, D_MODEL)), const((1, D_MODEL)),
            const((D_MODEL, ROUTER_COLS)), const((D_MODEL, ROUTER_COLS)), const((1, ROUTER_COLS)),
        ],
        out_specs=[
            h_spec,
            pl.BlockSpec((tt, 1), lambda b, j: (b * n_tiles + j, 0)),
            pl.BlockSpec((1, HALO_A, CONF_W), lambda b, j: (b, 0, 0)),
            pl.BlockSpec((1, HALO_C, SC_W), lambda b, j: (b, 0, 0)),
        ],
        out_shape=[
            jax.ShapeDtypeStruct((n_all, D_MODEL), jnp.float32),
            jax.ShapeDtypeStruct((n_rows, 1), jnp.int32),
            jax.ShapeDtypeStruct((n_batch, HALO_A, CONF_W), jnp.float32),
            jax.ShapeDtypeStruct((n_batch, HALO_C, SC_W), jnp.float32),
        ],
        scratch_shapes=[
            pltpu.VMEM((HALO_A + tt, CONF_W), jnp.float32),
            pltpu.VMEM((SUBLANES - 1, HALO_A + tt, CONF_W), jnp.float32),
            pltpu.VMEM((HALO_C + tt, SC_W), jnp.float32),
            pltpu.VMEM((tt, SC_W), jnp.float32),
            pltpu.VMEM((tt, CONF_W + SC_W), jnp.bfloat16),
        ],
        input_output_aliases={0: 0},
        compiler_params=pltpu.CompilerParams(
            dimension_semantics=("arbitrary", "arbitrary"), vmem_limit_bytes=VMEM_LIMIT),
        name=f"mixer_t{tt}",
    )(h_all, halo_a, halo_c, *wts)


def _moe_kernel(ea_ref, eb_ref, nblk_ref,
                tok0_ref, gtok_ref, stok_ref, h_in, g_ffn_ref, wr_hi_ref, wr_lo_ref, br_ref,
                wg_a, wu_a, wd_a, wg_b, wu_b, wd_b,
                h_out,
                xbuf0, xbuf1, obuf0, obuf1, wga_c, wua_c, wda_c, wgb_c, wub_c, wdb_c, gsem, ssem, prev):
    i = pl.program_id(0)
    nblk = nblk_ref[0]
    xbuf = (xbuf0, xbuf1)
    obuf = (obuf0, obuf1)

    def gather_row(tok_ref, r, dst_slot):
        return pltpu.make_async_copy(
            h_in.at[pl.ds(tok_ref[0, 0, r], 1), :], xbuf[dst_slot].at[pl.ds(r, 1), :], gsem.at[dst_slot])

    def scatter_row(r, src_slot):
        return pltpu.make_async_copy(
            obuf[src_slot].at[pl.ds(r, 1), :], h_out.at[pl.ds(stok_ref[0, 0, r], 1), :], ssem.at[src_slot])

    def wait_gather(dst_slot):
        pltpu.make_async_copy(h_in.at[pl.ds(0, MOE_ROWS), :], xbuf[dst_slot], gsem.at[dst_slot]).wait()

    def wait_scatter(src_slot):
        pltpu.make_async_copy(obuf[src_slot], h_out.at[pl.ds(0, MOE_ROWS), :], ssem.at[src_slot]).wait()

    @pl.when(i == 0)
    def _():
        obuf1[...] = jnp.zeros_like(obuf1)
        prev[0] = -1
        prev[1] = -1

        def start(r, c):
            gather_row(tok0_ref, r, 0).start()
            return c

        lax.fori_loop(0, MOE_ROWS, start, 0)

    def step(slot):
        other = 1 - slot
        ea = ea_ref[i]
        eb = eb_ref[i]
        wait_gather(slot)

        @pl.when(i > 0)
        def _():
            wait_scatter(slot)

        @pl.when(ea != prev[0])
        def _():
            wga_c[...] = wg_a[0, 0].astype(jnp.bfloat16)
            wua_c[...] = wu_a[0, 0].astype(jnp.bfloat16)
            wda_c[...] = wd_a[0, 0].astype(jnp.bfloat16)
            prev[0] = ea

        @pl.when(eb != prev[1])
        def _():
            wgb_c[...] = wg_b[0, 0].astype(jnp.bfloat16)
            wub_c[...] = wu_b[0, 0].astype(jnp.bfloat16)
            wdb_c[...] = wd_b[0, 0].astype(jnp.bfloat16)
            prev[1] = eb

        for r in range(MOE_ROWS):
            gather_row(gtok_ref, r, other).start()
        for r in range(MOE_ROWS):
            scatter_row(r, other).start()

        x = xbuf[slot][...]
        hn = _rms(x, g_ffn_ref[...])
        logits = _router_logits(hn, wr_hi_ref, wr_lo_ref, br_ref)
        lane = lax.broadcasted_iota(jnp.int32, logits.shape, 1)
        neg = jnp.float32(-jnp.inf)
        glog = jnp.where(lane < N_GROUPS, logits, neg)
        gmax = jnp.max(glog, axis=-1, keepdims=True)
        gsum = jnp.sum(jnp.exp(glog - gmax), axis=-1, keepdims=True)
        grp = ea >> 3
        l_g = jnp.sum(jnp.where(lane == grp, logits, 0.0), axis=-1, keepdims=True)
        p_g = jnp.exp(l_g - gmax) / gsum
        l_a = jnp.sum(jnp.where(lane == N_GROUPS + ea, logits, 0.0), axis=-1, keepdims=True)
        l_b = jnp.sum(jnp.where(lane == N_GROUPS + eb, logits, 0.0), axis=-1, keepdims=True)
        w_a = p_g / (1.0 + jnp.exp(l_b - l_a))
        w_b = p_g / (1.0 + jnp.exp(l_a - l_b))

        xb = hn.astype(jnp.bfloat16)

        def expert(wg_c, wu_c, wd_c):
            gate = jnp.dot(xb, wg_c[...], preferred_element_type=jnp.float32)
            up = jnp.dot(xb, wu_c[...], preferred_element_type=jnp.float32)
            hid = (jax.nn.silu(gate) * up).astype(jnp.bfloat16)
            return jnp.dot(hid, wd_c[...], preferred_element_type=jnp.float32)

        y = w_a * expert(wga_c, wua_c, wda_c) + w_b * expert(wgb_c, wub_c, wdb_c)
        obuf[slot][...] = x + y

        @pl.when(i == nblk)
        def _():
            wait_gather(other)
            wait_scatter(other)

    for parity in (0, 1):
        pl.when((i <= nblk) & ((i & 1) == parity))(functools.partial(step, parity))


def _moe_call(h_all, tok0, gtok, stok, ea, eb, nblk, g_ffn, wr_hi, wr_lo, br, w_gate, w_up, w_down, *, layer):
    n_steps = gtok.shape[0]
    const = lambda shape: pl.BlockSpec(shape, lambda i, *_: (0,) * len(shape))
    w_in_spec = lambda which: pl.BlockSpec(
        (1, 1, D_MODEL, D_EXPERT), lambda i, ea_, eb_, nb_: (layer, (ea_, eb_)[which][i], 0, 0))
    w_out_spec = lambda which: pl.BlockSpec(
        (1, 1, D_EXPERT, D_MODEL), lambda i, ea_, eb_, nb_: (layer, (ea_, eb_)[which][i], 0, 0))
    tok_spec = lambda imap: pl.BlockSpec((1, 1, MOE_ROWS), imap, memory_space=pltpu.SMEM)
    grid_spec = pltpu.PrefetchScalarGridSpec(
        num_scalar_prefetch=3,
        grid=(n_steps,),
        in_specs=[
            tok_spec(lambda i, *_: (0, 0, 0)), tok_spec(lambda i, *_: (i, 0, 0)), tok_spec(lambda i, *_: (i, 0, 0)),
            pl.BlockSpec(memory_space=pl.ANY),
            const((1, D_MODEL)), const((D_MODEL, ROUTER_COLS)), const((D_MODEL, ROUTER_COLS)),
            const((1, ROUTER_COLS)),
            w_in_spec(0), w_in_spec(0), w_out_spec(0),
            w_in_spec(1), w_in_spec(1), w_out_spec(1),
        ],
        out_specs=pl.BlockSpec(memory_space=pl.ANY),
        scratch_shapes=[
            pltpu.VMEM((MOE_ROWS, D_MODEL), jnp.float32),
            pltpu.VMEM((MOE_ROWS, D_MODEL), jnp.float32),
            pltpu.VMEM((MOE_ROWS, D_MODEL), jnp.float32),
            pltpu.VMEM((MOE_ROWS, D_MODEL), jnp.float32),
            pltpu.VMEM((D_MODEL, D_EXPERT), jnp.bfloat16),
            pltpu.VMEM((D_MODEL, D_EXPERT), jnp.bfloat16),
            pltpu.VMEM((D_EXPERT, D_MODEL), jnp.bfloat16),
            pltpu.VMEM((D_MODEL, D_EXPERT), jnp.bfloat16),
            pltpu.VMEM((D_MODEL, D_EXPERT), jnp.bfloat16),
            pltpu.VMEM((D_EXPERT, D_MODEL), jnp.bfloat16),
            pltpu.SemaphoreType.DMA((2,)),
            pltpu.SemaphoreType.DMA((2,)),
            pltpu.SMEM((2,), jnp.int32),
        ],
    )
    return pl.pallas_call(
        _moe_kernel,
        grid_spec=grid_spec,
        out_shape=jax.ShapeDtypeStruct(h_all.shape, h_all.dtype),
        input_output_aliases={6: 0},
        compiler_params=pltpu.CompilerParams(
            dimension_semantics=("arbitrary",), vmem_limit_bytes=VMEM_LIMIT),
        name="moe_pairs",
    )(ea, eb, nblk, tok0, gtok, stok, h_all, g_ffn, wr_hi, wr_lo, br,
      w_gate, w_up, w_down, w_gate, w_up, w_down)


def _final_norm_kernel(h_ref, g_ref, o_ref):
    o_ref[...] = _rms(h_ref[...], g_ref[...])


def _final_norm_call(h_all, g, n_rows):
    return pl.pallas_call(
        _final_norm_kernel,
        grid=(n_rows // MIX_TILE,),
        in_specs=[pl.BlockSpec((MIX_TILE, D_MODEL), lambda i: (i, 0)),
                  pl.BlockSpec((1, D_MODEL), lambda i: (0, 0))],
        out_specs=pl.BlockSpec((MIX_TILE, D_MODEL), lambda i: (i, 0)),
        out_shape=jax.ShapeDtypeStruct((n_rows, D_MODEL), jnp.float32),
        compiler_params=pltpu.CompilerParams(dimension_semantics=("arbitrary",)),
        name="final_norm",
    )(h_all, g)


def _dispatch_tables(cls, n_blocks):
    n_all = cls.shape[0]
    tok_bits = (n_all - 1).bit_length()
    key = jnp.sort((cls << tok_bits) | jnp.arange(n_all, dtype=jnp.int32))
    order = key & ((1 << tok_bits) - 1)
    ids = jnp.arange(N_CLASS_IDS, dtype=jnp.int32)
    counts = jnp.sum((cls[:, None] == ids[None, :]).astype(jnp.int32), axis=0)
    start = jnp.cumsum(counts) - counts
    padded = (counts + MOE_ROWS - 1) // MOE_ROWS * MOE_ROWS
    pad_end = jnp.cumsum(padded)
    pad_start = pad_end - padded
    nblk = (pad_end[-1] // MOE_ROWS).astype(jnp.int32)

    step = jnp.arange(n_blocks + 1, dtype=jnp.int32)
    blk = jnp.minimum(step, nblk - 1)
    p0 = blk * MOE_ROWS
    cb = jnp.sum((pad_end[None, :] <= p0[:, None]).astype(jnp.int32), axis=1)
    cb = jnp.minimum(cb, N_CLASS_IDS - 1)
    r0 = p0 - pad_start[cb]
    nv = jnp.clip(counts[cb] - r0, 0, MOE_ROWS).astype(jnp.int32)
    grp = cb >> 6
    ea = grp * EXPERTS_PER_GROUP + ((cb >> 3) & 7)
    eb = grp * EXPERTS_PER_GROUP + (cb & 7)

    r = jnp.arange(MOE_ROWS, dtype=jnp.int32)[None, :]
    src = jnp.clip(start[cb][:, None] + r0[:, None] + r, 0, n_all - 1)
    spare = n_all + (step[:, None] & (PAD_BANKS - 1)) * MOE_ROWS + r
    rows = jnp.where(r < nv[:, None], order[src], spare).astype(jnp.int32)
    sink = n_all + SINK_BANK * MOE_ROWS + r
    source = n_all + SOURCE_BANK * MOE_ROWS + r
    nxt = jnp.concatenate([rows[1:], rows[-1:]], axis=0)
    prv = jnp.concatenate([rows[:1], rows[:-1]], axis=0)
    gtok = jnp.where((step + 1 < nblk)[:, None], nxt, source)
    stok = jnp.where(((step >= 1) & (step <= nblk))[:, None], prv, sink)
    shape3 = (n_blocks + 1, 1, MOE_ROWS)
    return (rows[:1].reshape(1, 1, MOE_ROWS), gtok.reshape(shape3), stok.reshape(shape3), ea, eb,
            nblk.reshape(1))


def _over_sublanes(w):
    return jnp.broadcast_to(w[:, None, :], (w.shape[0], SUBLANES, w.shape[1]))


def kernel(x, meta_tokens, norm_mix_g, w_in, conf_dw_w, conf_dw_b, conf_ln_g, conf_ln_b, sc_conv_w, w_out,
           norm_ffn_g, w_router_group, b_router_group, w_router_expert, b_router_expert, w_exp_gate,
           w_exp_up, w_exp_down, final_norm_g):
    bsz, seq, d = x.shape
    depth = w_in.shape[0]
    n_main = bsz * seq
    n_all = n_main + N_META
    n_blocks = n_all // MOE_ROWS + N_CLASSES
    f32 = jnp.float32

    zero_halo = (jnp.zeros((HALO_A, CONF_W), f32), jnp.zeros((HALO_C, SC_W), f32))
    h_all = jnp.concatenate(
        [x.reshape(n_main, d), meta_tokens.astype(x.dtype), jnp.zeros((SPARE_ROWS, d), x.dtype)], axis=0)

    for l in range(depth):
        w_r = jnp.concatenate([w_router_group[l], w_router_expert[l]], axis=1).astype(f32)
        w_r = jnp.pad(w_r, ((0, 0), (0, ROUTER_COLS - w_r.shape[1])))
        wr_hi = w_r.astype(jnp.bfloat16)
        wr_lo = (w_r - wr_hi.astype(f32)).astype(jnp.bfloat16)
        b_r = jnp.concatenate([b_router_group[l], b_router_expert[l]]).astype(f32)
        b_r = jnp.pad(b_r, (0, ROUTER_COLS - b_r.shape[0])).reshape(1, ROUTER_COLS)
        g_ffn = norm_ffn_g[l].reshape(1, d)
        wts = (norm_mix_g[l].reshape(1, d), w_in[l].astype(jnp.bfloat16), _over_sublanes(conf_dw_w[l]),
               conf_dw_b[l].reshape(1, CONF_W), conf_ln_g[l].reshape(1, CONF_W),
               conf_ln_b[l].reshape(1, CONF_W), _over_sublanes(sc_conv_w[l]), w_out[l].astype(jnp.bfloat16),
               g_ffn, wr_hi, wr_lo, b_r)

        h_all, cls_meta, halo_a, halo_c = _mixer_call(
            h_all, *zero_halo, wts, row0=n_main, n_batch=1, n_tiles=1, tt=N_META)
        h_all, cls_main, _, _ = _mixer_call(
            h_all, halo_a[0], halo_c[0], wts, row0=0, n_batch=bsz, n_tiles=seq // MIX_TILE, tt=MIX_TILE)

        cls = jnp.concatenate([cls_main[:, 0], cls_meta[:, 0]])
        tok0, gtok, stok, ea, eb, nblk = _dispatch_tables(cls, n_blocks)
        h_all = _moe_call(h_all, tok0, gtok, stok, ea, eb, nblk, g_ffn, wr_hi, wr_lo, b_r,
                          w_exp_gate, w_exp_up, w_exp_down, layer=l)

    out = _final_norm_call(h_all, final_norm_g.reshape(1, d), n_main)
    return out.reshape(bsz, seq, d)
```

```python
import functools

import jax
import jax.numpy as jnp
from jax import lax
from jax.experimental import pallas as pl
from jax.experimental.pallas import tpu as pltpu

D_MODEL = 1024
N_META = 16
CONF_W = 512
SC_W = 512
PROJ_W = 2 * CONF_W + 3 * SC_W
CONF_KERNEL = 31
SC_KERNEL = 3
N_GROUPS = 4
EXPERTS_PER_GROUP = 8
N_EXPERTS = N_GROUPS * EXPERTS_PER_GROUP
D_EXPERT = 512
EPS = 1e-6

LANES = 128
SUBLANES = 8
HALO_A = 32
HALO_C = 8
CONV_ROWS = 32
MIX_TILE = 512
MOE_ROWS = 256
N_CLASS_IDS = N_GROUPS * 64
N_CLASSES = N_GROUPS * (EXPERTS_PER_GROUP * (EXPERTS_PER_GROUP - 1) // 2)
VMEM_LIMIT = 56 * 1024 * 1024
ROUTER_COLS = LANES
PAD_BANKS = 4
SINK_BANK = PAD_BANKS
SOURCE_BANK = PAD_BANKS + 1
SPARE_ROWS = (PAD_BANKS + 2) * MOE_ROWS


def _rms(x, g):
    return x * lax.rsqrt(jnp.mean(x * x, axis=-1, keepdims=True) + EPS) * g


def _router_logits(hn, w_hi_ref, w_lo_ref, b_ref):
    hi = hn.astype(jnp.bfloat16)
    lo = (hn - hi.astype(jnp.float32)).astype(jnp.bfloat16)
    acc = jnp.dot(hi, w_hi_ref[...], preferred_element_type=jnp.float32)
    acc = acc + jnp.dot(lo, w_hi_ref[...], preferred_element_type=jnp.float32)
    acc = acc + jnp.dot(hi, w_lo_ref[...], preferred_element_type=jnp.float32)
    return acc + b_ref[...]


def _mixer_kernel(h_ref, halo_a_ref, halo_c_ref, g_mix_ref, w_in_ref, dw_w_ref, dw_b_ref, ln_g_ref,
                  ln_b_ref, sc_w_ref, w_out_ref, g_ffn_ref, wr_hi_ref, wr_lo_ref, br_ref,
                  hout_ref, cls_ref, halo_a_out, halo_c_out,
                  aext, ashift, cext, sb_buf, ybuf, *, tt):
    j = pl.program_id(1)

    @pl.when(j == 0)
    def _():
        aext[0:HALO_A, :] = halo_a_ref[...]
        cext[0:HALO_C, :] = halo_c_ref[...]

    h = h_ref[...]
    xn = _rms(h, g_mix_ref[...]).astype(jnp.bfloat16)
    ug = jnp.dot(xn, w_in_ref[:, 0:2 * CONF_W], preferred_element_type=jnp.float32)
    aext[HALO_A:HALO_A + tt, :] = ug[:, 0:CONF_W] * jax.nn.sigmoid(ug[:, CONF_W:2 * CONF_W])
    bcv = jnp.dot(xn, w_in_ref[:, 2 * CONF_W:PROJ_W], preferred_element_type=jnp.float32)
    sb_buf[...] = bcv[:, 0:SC_W]
    cext[HALO_C:HALO_C + tt, :] = bcv[:, SC_W:2 * SC_W] * bcv[:, 2 * SC_W:3 * SC_W]

    n_shift_rows = HALO_A + tt - SUBLANES
    for p in range(1, SUBLANES):
        ashift[p - 1, 0:n_shift_rows, :] = aext[p:p + n_shift_rows, :]

    ch = min(tt, CONV_ROWS)
    tiles = (ch // SUBLANES, SUBLANES, CONF_W)
    for c0 in range(0, tt, ch):
        acc = jnp.broadcast_to(dw_b_ref[...], tiles)
        for k in range(CONF_KERNEL):
            row = HALO_A - (CONF_KERNEL - 1) + c0 + k
            phase = row % SUBLANES
            if phase == 0:
                tap = aext[row:row + ch, :]
            else:
                tap = ashift[phase - 1, row - phase:row - phase + ch, :]
            acc = acc + tap.reshape(tiles) * dw_w_ref[k]
        acc = acc.reshape(ch, CONF_W)
        mu = jnp.mean(acc, axis=-1, keepdims=True)
        xc = acc - mu
        ln = xc * lax.rsqrt(jnp.mean(xc * xc, axis=-1, keepdims=True) + EPS) * ln_g_ref[...] + ln_b_ref[...]
        ybuf[c0:c0 + ch, 0:CONF_W] = jax.nn.silu(ln).astype(jnp.bfloat16)
        sacc = None
        for k in range(SC_KERNEL):
            row = HALO_C - (SC_KERNEL - 1) + c0 + k
            term = cext[row:row + ch, :].reshape(tiles) * sc_w_ref[k]
            sacc = term if sacc is None else sacc + term
        ybuf[c0:c0 + ch, CONF_W:CONF_W + SC_W] = (
            sb_buf[c0:c0 + ch, :] * sacc.reshape(ch, SC_W)).astype(jnp.bfloat16)

    new_halo_a = aext[tt:tt + HALO_A, :]
    new_halo_c = cext[tt:tt + HALO_C, :]
    aext[0:HALO_A, :] = new_halo_a
    cext[0:HALO_C, :] = new_halo_c
    halo_a_out[0] = new_halo_a
    halo_c_out[0] = new_halo_c

    h_new = h + jnp.dot(ybuf[...], w_out_ref[...], preferred_element_type=jnp.float32)
    hout_ref[...] = h_new

    logits = _router_logits(_rms(h_new, g_ffn_ref[...]), wr_hi_ref, wr_lo_ref, br_ref)
    lane = lax.broadcasted_iota(jnp.int32, logits.shape, 1)
    neg = jnp.float32(-jnp.inf)
    big = jnp.int32(1 << 20)
    glog = jnp.where(lane < N_GROUPS, logits, neg)
    gmax = jnp.max(glog, axis=-1, keepdims=True)
    g_idx = jnp.min(jnp.where(glog == gmax, lane, big), axis=-1, keepdims=True)
    el = lane - N_GROUPS
    in_group = (el >= 0) & (el < N_EXPERTS) & ((el >> 3) == g_idx)
    l1 = jnp.where(in_group, logits, neg)
    m1 = jnp.max(l1, axis=-1, keepdims=True)
    i1 = jnp.min(jnp.where(l1 == m1, el, big), axis=-1, keepdims=True)
    l2 = jnp.where(el == i1, neg, l1)
    m2 = jnp.max(l2, axis=-1, keepdims=True)
    i2 = jnp.min(jnp.where(l2 == m2, el, big), axis=-1, keepdims=True)
    lo_e = jnp.minimum(i1, i2) & (EXPERTS_PER_GROUP - 1)
    hi_e = jnp.maximum(i1, i2) & (EXPERTS_PER_GROUP - 1)
    cls_ref[...] = g_idx * 64 + lo_e * 8 + hi_e


def _mixer_call(h_all, halo_a, halo_c, wts, *, row0, n_batch, n_tiles, tt):
    n_all = h_all.shape[0]
    blk0 = row0 // tt
    const = lambda shape: pl.BlockSpec(shape, lambda b, j: (0,) * len(shape))
    h_spec = pl.BlockSpec((tt, D_MODEL), lambda b, j: (blk0 + b * n_tiles + j, 0))
    n_rows = n_batch * n_tiles * tt
    return pl.pallas_call(
        functools.partial(_mixer_kernel, tt=tt),
        grid=(n_batch, n_tiles),
        in_specs=[
            h_spec,
            const((HALO_A, CONF_W)), const((HALO_C, SC_W)),
            const((1, D_MODEL)), const((D_MODEL, PROJ_W)),
            const((CONF_KERNEL, SUBLANES, CONF_W)), const((1, CONF_W)), const((1, CONF_W)), const((1, CONF_W)),
            const((SC_KERNEL, SUBLANES, SC_W)), const((CONF_W + SC_W, D_MODEL)), const((1, D_MODEL)),
            const((D_MODEL, ROUTER_COLS)), const((D_MODEL, ROUTER_COLS)), const((1, ROUTER_COLS)),
        ],
        out_specs=[
            h_spec,
            pl.BlockSpec((tt, 1), lambda b, j: (b * n_tiles + j, 0)),
            pl.BlockSpec((1, HALO_A, CONF_W), lambda b, j: (b, 0, 0)),
            pl.BlockSpec((1, HALO_C, SC_W), lambda b, j: (b, 0, 0)),
        ],
        out_shape=[
            jax.ShapeDtypeStruct((n_all, D_MODEL), jnp.float32),
            jax.ShapeDtypeStruct((n_rows, 1), jnp.int32),
            jax.ShapeDtypeStruct((n_batch, HALO_A, CONF_W), jnp.float32),
            jax.ShapeDtypeStruct((n_batch, HALO_C, SC_W), jnp.float32),
        ],
        scratch_shapes=[
            pltpu.VMEM((HALO_A + tt, CONF_W), jnp.float32),
            pltpu.VMEM((SUBLANES - 1, HALO_A + tt, CONF_W), jnp.float32),
            pltpu.VMEM((HALO_C + tt, SC_W), jnp.float32),
            pltpu.VMEM((tt, SC_W), jnp.float32),
            pltpu.VMEM((tt, CONF_W + SC_W), jnp.bfloat16),
        ],
        input_output_aliases={0: 0},
        compiler_params=pltpu.CompilerParams(
            dimension_semantics=("arbitrary", "arbitrary"), vmem_limit_bytes=VMEM_LIMIT),
        name=f"mixer_t{tt}",
    )(h_all, halo_a, halo_c, *wts)


def _moe_kernel(ea_ref, eb_ref, chg_ref, stage_ref, flag_ref, nblk_ref,
                tok0_ref, gtok_ref, stok_ref, h_in, g_ffn_ref, wr_hi_ref, wr_lo_ref, br_ref,
                wg_st, wu_st, wd_st,
                h_out,
                xbuf0, xbuf1, obuf0, obuf1, wg_c, wu_c, wd_c, wga, wua, wda, wgb, wub, wdb, gsem, ssem):
    i = pl.program_id(0)
    nblk = nblk_ref[0]
    xbuf = (xbuf0, xbuf1)
    obuf = (obuf0, obuf1)
    cur_a = (wga, wua, wda)
    cur_b = (wgb, wub, wdb)

    def gather_row(tok_ref, r, dst_slot):
        return pltpu.make_async_copy(
            h_in.at[pl.ds(tok_ref[0, 0, r], 1), :], xbuf[dst_slot].at[pl.ds(r, 1), :], gsem.at[dst_slot])

    def scatter_row(r, src_slot):
        return pltpu.make_async_copy(
            obuf[src_slot].at[pl.ds(r, 1), :], h_out.at[pl.ds(stok_ref[0, 0, r], 1), :], ssem.at[src_slot])

    def wait_gather(dst_slot):
        pltpu.make_async_copy(h_in.at[pl.ds(0, MOE_ROWS), :], xbuf[dst_slot], gsem.at[dst_slot]).wait()

    def wait_scatter(src_slot):
        pltpu.make_async_copy(obuf[src_slot], h_out.at[pl.ds(0, MOE_ROWS), :], ssem.at[src_slot]).wait()

    @pl.when(i == 0)
    def _():
        obuf1[...] = jnp.zeros_like(obuf1)

        def start(r, c):
            gather_row(tok0_ref, r, 0).start()
            return c

        lax.fori_loop(0, MOE_ROWS, start, 0)

    def step(slot):
        other = 1 - slot
        ea = ea_ref[i]
        eb = eb_ref[i]
        wait_gather(slot)

        @pl.when(i > 0)
        def _():
            wait_scatter(slot)

        @pl.when(flag_ref[i] != 0)
        def _():
            dst = stage_ref[i] & (EXPERTS_PER_GROUP - 1)
            wg_c[dst] = wg_st[0, 0].astype(jnp.bfloat16)
            wu_c[dst] = wu_st[0, 0].astype(jnp.bfloat16)
            wd_c[dst] = wd_st[0, 0].astype(jnp.bfloat16)

        for e, changed, cur in ((ea, chg_ref[0, i], cur_a), (eb, chg_ref[1, i], cur_b)):
            @pl.when(changed != 0)
            def _(e=e, cur=cur):
                at = e & (EXPERTS_PER_GROUP - 1)
                cur[0][...] = wg_c[at]
                cur[1][...] = wu_c[at]
                cur[2][...] = wd_c[at]

        for r in range(MOE_ROWS):
            gather_row(gtok_ref, r, other).start()
        for r in range(MOE_ROWS):
            scatter_row(r, other).start()

        x = xbuf[slot][...]
        hn = _rms(x, g_ffn_ref[...])
        logits = _router_logits(hn, wr_hi_ref, wr_lo_ref, br_ref)
        lane = lax.broadcasted_iota(jnp.int32, logits.shape, 1)
        neg = jnp.float32(-jnp.inf)
        glog = jnp.where(lane < N_GROUPS, logits, neg)
        gmax = jnp.max(glog, axis=-1, keepdims=True)
        gsum = jnp.sum(jnp.exp(glog - gmax), axis=-1, keepdims=True)
        grp = ea >> 3
        l_g = jnp.sum(jnp.where(lane == grp, logits, 0.0), axis=-1, keepdims=True)
        p_g = jnp.exp(l_g - gmax) / gsum
        l_a = jnp.sum(jnp.where(lane == N_GROUPS + ea, logits, 0.0), axis=-1, keepdims=True)
        l_b = jnp.sum(jnp.where(lane == N_GROUPS + eb, logits, 0.0), axis=-1, keepdims=True)
        w_a = p_g / (1.0 + jnp.exp(l_b - l_a))
        w_b = p_g / (1.0 + jnp.exp(l_a - l_b))

        xb = hn.astype(jnp.bfloat16)

        def expert(cur):
            gate = jnp.dot(xb, cur[0][...], preferred_element_type=jnp.float32)
            up = jnp.dot(xb, cur[1][...], preferred_element_type=jnp.float32)
            hid = (jax.nn.silu(gate) * up).astype(jnp.bfloat16)
            return jnp.dot(hid, cur[2][...], preferred_element_type=jnp.float32)

        y = w_a * expert(cur_a) + w_b * expert(cur_b)
        obuf[slot][...] = x + y

        @pl.when(i == nblk)
        def _():
            wait_gather(other)
            wait_scatter(other)

    for parity in (0, 1):
        pl.when((i <= nblk) & ((i & 1) == parity))(functools.partial(step, parity))


def _moe_call(h_all, tables, g_ffn, wr_hi, wr_lo, br, w_gate, w_up, w_down, *, layer):
    tok0, gtok, stok, ea, eb, chg, stage, flag, nblk = tables
    n_steps = gtok.shape[0]
    const = lambda shape: pl.BlockSpec(shape, lambda i, *_: (0,) * len(shape))
    staged = lambda shape: pl.BlockSpec(
        (1, 1) + shape, lambda i, ea_, eb_, ch_, st_, fl_, nb_: (layer, st_[i], 0, 0))
    tok_spec = lambda imap: pl.BlockSpec((1, 1, MOE_ROWS), imap, memory_space=pltpu.SMEM)
    grid_spec = pltpu.PrefetchScalarGridSpec(
        num_scalar_prefetch=6,
        grid=(n_steps,),
        in_specs=[
            tok_spec(lambda i, *_: (0, 0, 0)), tok_spec(lambda i, *_: (i, 0, 0)), tok_spec(lambda i, *_: (i, 0, 0)),
            pl.BlockSpec(memory_space=pl.ANY),
            const((1, D_MODEL)), const((D_MODEL, ROUTER_COLS)), const((D_MODEL, ROUTER_COLS)),
            const((1, ROUTER_COLS)),
            staged((D_MODEL, D_EXPERT)), staged((D_MODEL, D_EXPERT)), staged((D_EXPERT, D_MODEL)),
        ],
        out_specs=pl.BlockSpec(memory_space=pl.ANY),
        scratch_shapes=[
            pltpu.VMEM((MOE_ROWS, D_MODEL), jnp.float32),
            pltpu.VMEM((MOE_ROWS, D_MODEL), jnp.float32),
            pltpu.VMEM((MOE_ROWS, D_MODEL), jnp.float32),
            pltpu.VMEM((MOE_ROWS, D_MODEL), jnp.float32),
            pltpu.VMEM((EXPERTS_PER_GROUP, D_MODEL, D_EXPERT), jnp.bfloat16),
            pltpu.VMEM((EXPERTS_PER_GROUP, D_MODEL, D_EXPERT), jnp.bfloat16),
            pltpu.VMEM((EXPERTS_PER_GROUP, D_EXPERT, D_MODEL), jnp.bfloat16),
            pltpu.VMEM((D_MODEL, D_EXPERT), jnp.bfloat16),
            pltpu.VMEM((D_MODEL, D_EXPERT), jnp.bfloat16),
            pltpu.VMEM((D_EXPERT, D_MODEL), jnp.bfloat16),
            pltpu.VMEM((D_MODEL, D_EXPERT), jnp.bfloat16),
            pltpu.VMEM((D_MODEL, D_EXPERT), jnp.bfloat16),
            pltpu.VMEM((D_EXPERT, D_MODEL), jnp.bfloat16),
            pltpu.SemaphoreType.DMA((2,)),
            pltpu.SemaphoreType.DMA((2,)),
        ],
    )
    return pl.pallas_call(
        _moe_kernel,
        grid_spec=grid_spec,
        out_shape=jax.ShapeDtypeStruct(h_all.shape, h_all.dtype),
        input_output_aliases={9: 0},
        compiler_params=pltpu.CompilerParams(
            dimension_semantics=("arbitrary",), vmem_limit_bytes=VMEM_LIMIT),
        name="moe_pairs",
    )(ea, eb, chg, stage, flag, nblk, tok0, gtok, stok, h_all, g_ffn, wr_hi, wr_lo, br, w_gate, w_up, w_down)


def _final_norm_kernel(h_ref, g_ref, o_ref):
    o_ref[...] = _rms(h_ref[...], g_ref[...])


def _final_norm_call(h_all, g, n_rows):
    return pl.pallas_call(
        _final_norm_kernel,
        grid=(n_rows // MIX_TILE,),
        in_specs=[pl.BlockSpec((MIX_TILE, D_MODEL), lambda i: (i, 0)),
                  pl.BlockSpec((1, D_MODEL), lambda i: (0, 0))],
        out_specs=pl.BlockSpec((MIX_TILE, D_MODEL), lambda i: (i, 0)),
        out_shape=jax.ShapeDtypeStruct((n_rows, D_MODEL), jnp.float32),
        compiler_params=pltpu.CompilerParams(dimension_semantics=("arbitrary",)),
        name="final_norm",
    )(h_all, g)


def _dispatch_tables(cls, n_blocks):
    n_all = cls.shape[0]
    tok_bits = (n_all - 1).bit_length()
    key = jnp.sort((cls << tok_bits) | jnp.arange(n_all, dtype=jnp.int32))
    order = key & ((1 << tok_bits) - 1)
    ids = jnp.arange(N_CLASS_IDS, dtype=jnp.int32)
    counts = jnp.sum((cls[:, None] == ids[None, :]).astype(jnp.int32), axis=0)
    start = jnp.cumsum(counts) - counts
    padded = (counts + MOE_ROWS - 1) // MOE_ROWS * MOE_ROWS
    pad_end = jnp.cumsum(padded)
    pad_start = pad_end - padded
    nblk = (pad_end[-1] // MOE_ROWS).astype(jnp.int32)

    kidx = jnp.arange(n_blocks, dtype=jnp.int32)
    real = kidx < nblk
    p0 = kidx * MOE_ROWS
    cb = jnp.sum((pad_end[None, :] <= p0[:, None]).astype(jnp.int32), axis=1)
    cb = jnp.minimum(cb, N_CLASS_IDS - 1)
    r0 = p0 - pad_start[cb]
    nv = jnp.where(real, jnp.clip(counts[cb] - r0, 0, MOE_ROWS), 0).astype(jnp.int32)
    grp = cb >> 6
    blk_a = grp * EXPERTS_PER_GROUP + ((cb >> 3) & 7)
    blk_b = grp * EXPERTS_PER_GROUP + (cb & 7)
    r = jnp.arange(MOE_ROWS, dtype=jnp.int32)[None, :]
    src = jnp.clip(start[cb][:, None] + r0[:, None] + r, 0, n_all - 1)
    blk_tok = order[src]

    never = jnp.int32(n_blocks)
    experts = jnp.arange(N_EXPERTS, dtype=jnp.int32)[:, None]
    used = real[None, :] & ((blk_a[None, :] == experts) | (blk_b[None, :] == experts))
    first_use = jnp.min(jnp.where(used, kidx[None, :], never), axis=1)
    new_a = real & (first_use[blk_a] == kidx)
    new_b = real & (first_use[blk_b] == kidx)
    filler = new_a & new_b
    blk_step = kidx + jnp.cumsum(filler.astype(jnp.int32))
    n_act = nblk + jnp.sum(filler.astype(jnp.int32))

    n_steps = n_blocks + N_EXPERTS // 2 + 1
    step = jnp.arange(n_steps, dtype=jnp.int32)
    k_done = jnp.sum((real[None, :] & (blk_step[None, :] <= step[:, None])).astype(jnp.int32), axis=1) - 1
    kc = jnp.clip(k_done, 0, n_blocks - 1)
    kn = jnp.clip(k_done + 1, 0, n_blocks - 1)
    active = step < n_act
    is_block = active & (k_done >= 0) & (blk_step[kc] == step)
    is_filler = active & ~is_block

    spare = n_all + (step[:, None] & (PAD_BANKS - 1)) * MOE_ROWS + r
    rows = jnp.where(is_block[:, None] & (r < nv[kc][:, None]), blk_tok[kc], spare).astype(jnp.int32)
    last = jnp.clip(nblk - 1, 0, n_blocks - 1)
    ea = jnp.where(is_block, blk_a[kc], jnp.where(is_filler, blk_a[kn], blk_a[last]))
    eb = jnp.where(is_block, blk_b[kc], jnp.where(is_filler, blk_a[kn], blk_b[last]))
    staged_here = jnp.where(
        is_block, jnp.where(new_b[kc], blk_b[kc], jnp.where(new_a[kc], blk_a[kc], -1)),
        jnp.where(is_filler, blk_a[kn], -1))
    flag = (staged_here >= 0).astype(jnp.int32)
    far = jnp.int32(n_steps)
    next_evt = lax.cummin(jnp.where(flag > 0, step, far), axis=0, reverse=True)
    last_evt = jnp.max(jnp.where(flag > 0, step, 0))
    stage = staged_here[jnp.where(next_evt < far, next_evt, last_evt)]

    sink = n_all + SINK_BANK * MOE_ROWS + r
    source = n_all + SOURCE_BANK * MOE_ROWS + r
    nxt = jnp.concatenate([rows[1:], rows[-1:]], axis=0)
    prv = jnp.concatenate([rows[:1], rows[:-1]], axis=0)
    gtok = jnp.where((step + 1 < n_act)[:, None], nxt, source)
    stok = jnp.where(((step >= 1) & (step <= n_act))[:, None], prv, sink)
    shape3 = (n_steps, 1, MOE_ROWS)
    chg = jnp.stack([jnp.concatenate([jnp.ones((1,), jnp.int32), (e[1:] != e[:-1]).astype(jnp.int32)])
                     for e in (ea, eb)])
    return (rows[:1].reshape(1, 1, MOE_ROWS), gtok.reshape(shape3), stok.reshape(shape3), ea, eb, chg,
            stage, flag, n_act.reshape(1).astype(jnp.int32))


def _over_sublanes(w):
    return jnp.broadcast_to(w[:, None, :], (w.shape[0], SUBLANES, w.shape[1]))


def kernel(x, meta_tokens, norm_mix_g, w_in, conf_dw_w, conf_dw_b, conf_ln_g, conf_ln_b, sc_conv_w, w_out,
           norm_ffn_g, w_router_group, b_router_group, w_router_expert, b_router_expert, w_exp_gate,
           w_exp_up, w_exp_down, final_norm_g):
    bsz, seq, d = x.shape
    depth = w_in.shape[0]
    n_main = bsz * seq
    n_all = n_main + N_META
    n_blocks = n_all // MOE_ROWS + N_CLASSES
    f32 = jnp.float32

    zero_halo = (jnp.zeros((HALO_A, CONF_W), f32), jnp.zeros((HALO_C, SC_W), f32))
    h_all = jnp.concatenate(
        [x.reshape(n_main, d), meta_tokens.astype(x.dtype), jnp.zeros((SPARE_ROWS, d), x.dtype)], axis=0)

    for l in range(depth):
        w_r = jnp.concatenate([w_router_group[l], w_router_expert[l]], axis=1).astype(f32)
        w_r = jnp.pad(w_r, ((0, 0), (0, ROUTER_COLS - w_r.shape[1])))
        wr_hi = w_r.astype(jnp.bfloat16)
        wr_lo = (w_r - wr_hi.astype(f32)).astype(jnp.bfloat16)
        b_r = jnp.concatenate([b_router_group[l], b_router_expert[l]]).astype(f32)
        b_r = jnp.pad(b_r, (0, ROUTER_COLS - b_r.shape[0])).reshape(1, ROUTER_COLS)
        g_ffn = norm_ffn_g[l].reshape(1, d)
        wts = (norm_mix_g[l].reshape(1, d), w_in[l].astype(jnp.bfloat16), _over_sublanes(conf_dw_w[l]),
               conf_dw_b[l].reshape(1, CONF_W), conf_ln_g[l].reshape(1, CONF_W),
               conf_ln_b[l].reshape(1, CONF_W), _over_sublanes(sc_conv_w[l]), w_out[l].astype(jnp.bfloat16),
               g_ffn, wr_hi, wr_lo, b_r)

        h_all, cls_meta, halo_a, halo_c = _mixer_call(
            h_all, *zero_halo, wts, row0=n_main, n_batch=1, n_tiles=1, tt=N_META)
        h_all, cls_main, _, _ = _mixer_call(
            h_all, halo_a[0], halo_c[0], wts, row0=0, n_batch=bsz, n_tiles=seq // MIX_TILE, tt=MIX_TILE)

        cls = jnp.concatenate([cls_main[:, 0], cls_meta[:, 0]])
        h_all = _moe_call(h_all, _dispatch_tables(cls, n_blocks), g_ffn, wr_hi, wr_lo, b_r,
                          w_exp_gate, w_exp_up, w_exp_down, layer=l)

    out = _final_norm_call(h_all, final_norm_g.reshape(1, d), n_main)
    return out.reshape(bsz, seq, d)
```

```python
import functools

import jax
import jax.numpy as jnp
from jax import lax
from jax.experimental import pallas as pl
from jax.experimental.pallas import tpu as pltpu

D_MODEL = 1024
N_META = 16
CONF_W = 512
SC_W = 512
PROJ_W = 2 * CONF_W + 3 * SC_W
CONF_KERNEL = 31
SC_KERNEL = 3
N_GROUPS = 4
EXPERTS_PER_GROUP = 8
N_EXPERTS = N_GROUPS * EXPERTS_PER_GROUP
D_EXPERT = 512
EPS = 1e-6

LANES = 128
SUBLANES = 8
HALO_A = 32
HALO_C = 8
CONV_ROWS = 32
MIX_TILE = 512
MOE_ROWS = 256
N_CLASS_IDS = N_GROUPS * 64
N_CLASSES = N_GROUPS * (EXPERTS_PER_GROUP * (EXPERTS_PER_GROUP - 1) // 2)
VMEM_LIMIT = 56 * 1024 * 1024
ROUTER_COLS = LANES
PAD_BANKS = 4
SINK_BANK = PAD_BANKS
SOURCE_BANK = PAD_BANKS + 1
SPARE_ROWS = (PAD_BANKS + 2) * MOE_ROWS


def _load_tokens(ref, n):
    return jnp.concatenate([ref[pl.ds(s, n, stride=SUBLANES), :] for s in range(SUBLANES)], axis=-1)


def _store_tokens(ref, x, n):
    for s in range(SUBLANES):
        ref[pl.ds(s, n, stride=SUBLANES), :] = x[:, s * LANES:(s + 1) * LANES]


def _rms(x, g):
    return x * lax.rsqrt(jnp.mean(x * x, axis=-1, keepdims=True) + EPS) * g


def _router_logits(hn, w_hi_ref, w_lo_ref, b_ref):
    hi = hn.astype(jnp.bfloat16)
    lo = (hn - hi.astype(jnp.float32)).astype(jnp.bfloat16)
    acc = jnp.dot(hi, w_hi_ref[...], preferred_element_type=jnp.float32)
    acc = acc + jnp.dot(lo, w_hi_ref[...], preferred_element_type=jnp.float32)
    acc = acc + jnp.dot(hi, w_lo_ref[...], preferred_element_type=jnp.float32)
    return acc + b_ref[...]


def _mixer_kernel(h_ref, halo_a_ref, halo_c_ref, g_mix_ref, w_in_ref, dw_w_ref, dw_b_ref, ln_g_ref,
                  ln_b_ref, sc_w_ref, w_out_ref, g_ffn_ref, wr_hi_ref, wr_lo_ref, br_ref,
                  hout_ref, cls_ref, halo_a_out, halo_c_out,
                  aext, ashift, cext, sb_buf, ybuf, *, tt):
    j = pl.program_id(1)

    @pl.when(j == 0)
    def _():
        aext[0:HALO_A, :] = halo_a_ref[...]
        cext[0:HALO_C, :] = halo_c_ref[...]

    h = _load_tokens(h_ref, tt)
    xn = _rms(h, g_mix_ref[...]).astype(jnp.bfloat16)
    ug = jnp.dot(xn, w_in_ref[:, 0:2 * CONF_W], preferred_element_type=jnp.float32)
    aext[HALO_A:HALO_A + tt, :] = ug[:, 0:CONF_W] * jax.nn.sigmoid(ug[:, CONF_W:2 * CONF_W])
    bcv = jnp.dot(xn, w_in_ref[:, 2 * CONF_W:PROJ_W], preferred_element_type=jnp.float32)
    sb_buf[...] = bcv[:, 0:SC_W]
    cext[HALO_C:HALO_C + tt, :] = bcv[:, SC_W:2 * SC_W] * bcv[:, 2 * SC_W:3 * SC_W]

    n_shift_rows = HALO_A + tt - SUBLANES
    for p in range(1, SUBLANES):
        ashift[p - 1, 0:n_shift_rows, :] = aext[p:p + n_shift_rows, :]

    ch = min(tt, CONV_ROWS)
    tiles = (ch // SUBLANES, SUBLANES, CONF_W)
    for c0 in range(0, tt, ch):
        acc = jnp.broadcast_to(dw_b_ref[...], tiles)
        for k in range(CONF_KERNEL):
            row = HALO_A - (CONF_KERNEL - 1) + c0 + k
            phase = row % SUBLANES
            if phase == 0:
                tap = aext[row:row + ch, :]
            else:
                tap = ashift[phase - 1, row - phase:row - phase + ch, :]
            acc = acc + tap.reshape(tiles) * dw_w_ref[k]
        acc = acc.reshape(ch, CONF_W)
        mu = jnp.mean(acc, axis=-1, keepdims=True)
        xc = acc - mu
        ln = xc * lax.rsqrt(jnp.mean(xc * xc, axis=-1, keepdims=True) + EPS) * ln_g_ref[...] + ln_b_ref[...]
        ybuf[c0:c0 + ch, 0:CONF_W] = jax.nn.silu(ln).astype(jnp.bfloat16)
        sacc = None
        for k in range(SC_KERNEL):
            row = HALO_C - (SC_KERNEL - 1) + c0 + k
            term = cext[row:row + ch, :].reshape(tiles) * sc_w_ref[k]
            sacc = term if sacc is None else sacc + term
        ybuf[c0:c0 + ch, CONF_W:CONF_W + SC_W] = (
            sb_buf[c0:c0 + ch, :] * sacc.reshape(ch, SC_W)).astype(jnp.bfloat16)

    new_halo_a = aext[tt:tt + HALO_A, :]
    new_halo_c = cext[tt:tt + HALO_C, :]
    aext[0:HALO_A, :] = new_halo_a
    cext[0:HALO_C, :] = new_halo_c
    halo_a_out[0] = new_halo_a
    halo_c_out[0] = new_halo_c

    h_new = h + jnp.dot(ybuf[...], w_out_ref[...], preferred_element_type=jnp.float32)
    _store_tokens(hout_ref, h_new, tt)

    logits = _router_logits(_rms(h_new, g_ffn_ref[...]), wr_hi_ref, wr_lo_ref, br_ref)
    lane = lax.broadcasted_iota(jnp.int32, logits.shape, 1)
    neg = jnp.float32(-jnp.inf)
    big = jnp.int32(1 << 20)
    glog = jnp.where(lane < N_GROUPS, logits, neg)
    gmax = jnp.max(glog, axis=-1, keepdims=True)
    g_idx = jnp.min(jnp.where(glog == gmax, lane, big), axis=-1, keepdims=True)
    el = lane - N_GROUPS
    in_group = (el >= 0) & (el < N_EXPERTS) & ((el >> 3) == g_idx)
    l1 = jnp.where(in_group, logits, neg)
    m1 = jnp.max(l1, axis=-1, keepdims=True)
    i1 = jnp.min(jnp.where(l1 == m1, el, big), axis=-1, keepdims=True)
    l2 = jnp.where(el == i1, neg, l1)
    m2 = jnp.max(l2, axis=-1, keepdims=True)
    i2 = jnp.min(jnp.where(l2 == m2, el, big), axis=-1, keepdims=True)
    lo_e = jnp.minimum(i1, i2) & (EXPERTS_PER_GROUP - 1)
    hi_e = jnp.maximum(i1, i2) & (EXPERTS_PER_GROUP - 1)
    cls_ref[...] = g_idx * 64 + lo_e * 8 + hi_e


def _mixer_call(h_all, halo_a, halo_c, wts, *, row0, n_batch, n_tiles, tt):
    blk0 = row0 // tt
    const = lambda shape: pl.BlockSpec(shape, lambda b, j: (0,) * len(shape))
    h_spec = pl.BlockSpec((tt * SUBLANES, LANES), lambda b, j: (blk0 + b * n_tiles + j, 0))
    n_rows = n_batch * n_tiles * tt
    return pl.pallas_call(
        functools.partial(_mixer_kernel, tt=tt),
        grid=(n_batch, n_tiles),
        in_specs=[
            h_spec,
            const((HALO_A, CONF_W)), const((HALO_C, SC_W)),
            const((1, D_MODEL)), const((D_MODEL, PROJ_W)),
            const((CONF_KERNEL, SUBLANES, CONF_W)), const((1, CONF_W)), const((1, CONF_W)), const((1, CONF_W)),
            const((SC_KERNEL, SUBLANES, SC_W)), const((CONF_W + SC_W, D_MODEL)), const((1, D_MODEL)),
            const((D_MODEL, ROUTER_COLS)), const((D_MODEL, ROUTER_COLS)), const((1, ROUTER_COLS)),
        ],
        out_specs=[
            h_spec,
            pl.BlockSpec((tt, 1), lambda b, j: (b * n_tiles + j, 0)),
            pl.BlockSpec((1, HALO_A, CONF_W), lambda b, j: (b, 0, 0)),
            pl.BlockSpec((1, HALO_C, SC_W), lambda b, j: (b, 0, 0)),
        ],
        out_shape=[
            jax.ShapeDtypeStruct(h_all.shape, jnp.float32),
            jax.ShapeDtypeStruct((n_rows, 1), jnp.int32),
            jax.ShapeDtypeStruct((n_batch, HALO_A, CONF_W), jnp.float32),
            jax.ShapeDtypeStruct((n_batch, HALO_C, SC_W), jnp.float32),
        ],
        scratch_shapes=[
            pltpu.VMEM((HALO_A + tt, CONF_W), jnp.float32),
            pltpu.VMEM((SUBLANES - 1, HALO_A + tt, CONF_W), jnp.float32),
            pltpu.VMEM((HALO_C + tt, SC_W), jnp.float32),
            pltpu.VMEM((tt, SC_W), jnp.float32),
            pltpu.VMEM((tt, CONF_W + SC_W), jnp.bfloat16),
        ],
        input_output_aliases={0: 0},
        compiler_params=pltpu.CompilerParams(
            dimension_semantics=("arbitrary", "arbitrary"), vmem_limit_bytes=VMEM_LIMIT),
        name=f"mixer_t{tt}",
    )(h_all, halo_a, halo_c, *wts)


def _moe_kernel(ea_ref, eb_ref, chg_ref, stage_ref, flag_ref, nblk_ref,
                tok0_ref, gtok_ref, stok_ref, h_in, g_ffn_ref, wr_hi_ref, wr_lo_ref, br_ref,
                wg_st, wu_st, wd_st,
                h_out,
                xbuf0, xbuf1, obuf0, obuf1, wg_c, wu_c, wd_c, wga, wua, wda, wgb, wub, wdb, gsem, ssem):
    i = pl.program_id(0)
    nblk = nblk_ref[0]
    xbuf = (xbuf0, xbuf1)
    obuf = (obuf0, obuf1)
    cur_a = (wga, wua, wda)
    cur_b = (wgb, wub, wdb)

    def token_tile(tok_ref, r):
        return pl.ds(pl.multiple_of(tok_ref[0, 0, r], SUBLANES), SUBLANES)

    def gather_row(tok_ref, r, dst_slot):
        return pltpu.make_async_copy(
            h_in.at[token_tile(tok_ref, r), :], xbuf[dst_slot].at[pl.ds(r * SUBLANES, SUBLANES), :],
            gsem.at[dst_slot])

    def scatter_row(r, src_slot):
        return pltpu.make_async_copy(
            obuf[src_slot].at[pl.ds(r * SUBLANES, SUBLANES), :], h_out.at[token_tile(stok_ref, r), :],
            ssem.at[src_slot])

    def wait_gather(dst_slot):
        pltpu.make_async_copy(
            h_in.at[pl.ds(0, MOE_ROWS * SUBLANES), :], xbuf[dst_slot], gsem.at[dst_slot]).wait()

    def wait_scatter(src_slot):
        pltpu.make_async_copy(
            obuf[src_slot], h_out.at[pl.ds(0, MOE_ROWS * SUBLANES), :], ssem.at[src_slot]).wait()

    @pl.when(i == 0)
    def _():
        obuf1[...] = jnp.zeros_like(obuf1)

        def start(r, c):
            gather_row(tok0_ref, r, 0).start()
            return c

        lax.fori_loop(0, MOE_ROWS, start, 0)

    def step(slot):
        other = 1 - slot
        ea = ea_ref[i]
        eb = eb_ref[i]
        wait_gather(slot)

        @pl.when(i > 0)
        def _():
            wait_scatter(slot)

        @pl.when(flag_ref[i] != 0)
        def _():
            dst = stage_ref[i] & (EXPERTS_PER_GROUP - 1)
            wg_c[dst] = wg_st[0, 0].astype(jnp.bfloat16)
            wu_c[dst] = wu_st[0, 0].astype(jnp.bfloat16)
            wd_c[dst] = wd_st[0, 0].astype(jnp.bfloat16)

        for e, changed, cur in ((ea, chg_ref[0, i], cur_a), (eb, chg_ref[1, i], cur_b)):
            @pl.when(changed != 0)
            def _(e=e, cur=cur):
                at = e & (EXPERTS_PER_GROUP - 1)
                cur[0][...] = wg_c[at]
                cur[1][...] = wu_c[at]
                cur[2][...] = wd_c[at]

        for r in range(MOE_ROWS):
            gather_row(gtok_ref, r, other).start()
        for r in range(MOE_ROWS):
            scatter_row(r, other).start()

        x = _load_tokens(xbuf[slot], MOE_ROWS)
        hn = _rms(x, g_ffn_ref[...])
        logits = _router_logits(hn, wr_hi_ref, wr_lo_ref, br_ref)
        lane = lax.broadcasted_iota(jnp.int32, logits.shape, 1)
        neg = jnp.float32(-jnp.inf)
        glog = jnp.where(lane < N_GROUPS, logits, neg)
        gmax = jnp.max(glog, axis=-1, keepdims=True)
        gsum = jnp.sum(jnp.exp(glog - gmax), axis=-1, keepdims=True)
        grp = ea >> 3
        l_g = jnp.sum(jnp.where(lane == grp, logits, 0.0), axis=-1, keepdims=True)
        p_g = jnp.exp(l_g - gmax) / gsum
        l_a = jnp.sum(jnp.where(lane == N_GROUPS + ea, logits, 0.0), axis=-1, keepdims=True)
        l_b = jnp.sum(jnp.where(lane == N_GROUPS + eb, logits, 0.0), axis=-1, keepdims=True)
        w_a = p_g / (1.0 + jnp.exp(l_b - l_a))
        w_b = p_g / (1.0 + jnp.exp(l_a - l_b))

        xb = hn.astype(jnp.bfloat16)

        def expert(cur):
            gate = jnp.dot(xb, cur[0][...], preferred_element_type=jnp.float32)
            up = jnp.dot(xb, cur[1][...], preferred_element_type=jnp.float32)
            hid = (jax.nn.silu(gate) * up).astype(jnp.bfloat16)
            return jnp.dot(hid, cur[2][...], preferred_element_type=jnp.float32)

        y = w_a * expert(cur_a) + w_b * expert(cur_b)
        _store_tokens(obuf[slot], x + y, MOE_ROWS)

        @pl.when(i == nblk)
        def _():
            wait_gather(other)
            wait_scatter(other)

    for parity in (0, 1):
        pl.when((i <= nblk) & ((i & 1) == parity))(functools.partial(step, parity))


def _moe_call(h_all, tables, g_ffn, wr_hi, wr_lo, br, w_gate, w_up, w_down, *, layer):
    tok0, gtok, stok, ea, eb, chg, stage, flag, nblk = tables
    n_steps = gtok.shape[0]
    const = lambda shape: pl.BlockSpec(shape, lambda i, *_: (0,) * len(shape))
    staged = lambda shape: pl.BlockSpec(
        (1, 1) + shape, lambda i, ea_, eb_, ch_, st_, fl_, nb_: (layer, st_[i], 0, 0))
    tok_spec = lambda imap: pl.BlockSpec((1, 1, MOE_ROWS), imap, memory_space=pltpu.SMEM)
    grid_spec = pltpu.PrefetchScalarGridSpec(
        num_scalar_prefetch=6,
        grid=(n_steps,),
        in_specs=[
            tok_spec(lambda i, *_: (0, 0, 0)), tok_spec(lambda i, *_: (i, 0, 0)), tok_spec(lambda i, *_: (i, 0, 0)),
            pl.BlockSpec(memory_space=pl.ANY),
            const((1, D_MODEL)), const((D_MODEL, ROUTER_COLS)), const((D_MODEL, ROUTER_COLS)),
            const((1, ROUTER_COLS)),
            staged((D_MODEL, D_EXPERT)), staged((D_MODEL, D_EXPERT)), staged((D_EXPERT, D_MODEL)),
        ],
        out_specs=pl.BlockSpec(memory_space=pl.ANY),
        scratch_shapes=[
            pltpu.VMEM((MOE_ROWS * SUBLANES, LANES), jnp.float32),
            pltpu.VMEM((MOE_ROWS * SUBLANES, LANES), jnp.float32),
            pltpu.VMEM((MOE_ROWS * SUBLANES, LANES), jnp.float32),
            pltpu.VMEM((MOE_ROWS * SUBLANES, LANES), jnp.float32),
            pltpu.VMEM((EXPERTS_PER_GROUP, D_MODEL, D_EXPERT), jnp.bfloat16),
            pltpu.VMEM((EXPERTS_PER_GROUP, D_MODEL, D_EXPERT), jnp.bfloat16),
            pltpu.VMEM((EXPERTS_PER_GROUP, D_EXPERT, D_MODEL), jnp.bfloat16),
            pltpu.VMEM((D_MODEL, D_EXPERT), jnp.bfloat16),
            pltpu.VMEM((D_MODEL, D_EXPERT), jnp.bfloat16),
            pltpu.VMEM((D_EXPERT, D_MODEL), jnp.bfloat16),
            pltpu.VMEM((D_MODEL, D_EXPERT), jnp.bfloat16),
            pltpu.VMEM((D_MODEL, D_EXPERT), jnp.bfloat16),
            pltpu.VMEM((D_EXPERT, D_MODEL), jnp.bfloat16),
            pltpu.SemaphoreType.DMA((2,)),
            pltpu.SemaphoreType.DMA((2,)),
        ],
    )
    return pl.pallas_call(
        _moe_kernel,
        grid_spec=grid_spec,
        out_shape=jax.ShapeDtypeStruct(h_all.shape, h_all.dtype),
        input_output_aliases={9: 0},
        compiler_params=pltpu.CompilerParams(
            dimension_semantics=("arbitrary",), vmem_limit_bytes=VMEM_LIMIT),
        name="moe_pairs",
    )(ea, eb, chg, stage, flag, nblk, tok0, gtok, stok, h_all, g_ffn, wr_hi, wr_lo, br, w_gate, w_up, w_down)


def _final_norm_kernel(h_ref, g_ref, o_ref):
    o_ref[...] = _rms(_load_tokens(h_ref, MIX_TILE), g_ref[...])


def _final_norm_call(h_all, g, n_rows):
    return pl.pallas_call(
        _final_norm_kernel,
        grid=(n_rows // MIX_TILE,),
        in_specs=[pl.BlockSpec((MIX_TILE * SUBLANES, LANES), lambda i: (i, 0)),
                  pl.BlockSpec((1, D_MODEL), lambda i: (0, 0))],
        out_specs=pl.BlockSpec((MIX_TILE, D_MODEL), lambda i: (i, 0)),
        out_shape=jax.ShapeDtypeStruct((n_rows, D_MODEL), jnp.float32),
        compiler_params=pltpu.CompilerParams(dimension_semantics=("arbitrary",)),
        name="final_norm",
    )(h_all, g)


def _dispatch_tables(cls, n_blocks):
    n_all = cls.shape[0]
    tok_bits = (n_all - 1).bit_length()
    key = jnp.sort((cls << tok_bits) | jnp.arange(n_all, dtype=jnp.int32))
    order = key & ((1 << tok_bits) - 1)
    ids = jnp.arange(N_CLASS_IDS, dtype=jnp.int32)
    counts = jnp.sum((cls[:, None] == ids[None, :]).astype(jnp.int32), axis=0)
    start = jnp.cumsum(counts) - counts
    padded = (counts + MOE_ROWS - 1) // MOE_ROWS * MOE_ROWS
    pad_end = jnp.cumsum(padded)
    pad_start = pad_end - padded
    nblk = (pad_end[-1] // MOE_ROWS).astype(jnp.int32)

    kidx = jnp.arange(n_blocks, dtype=jnp.int32)
    real = kidx < nblk
    p0 = kidx * MOE_ROWS
    cb = jnp.sum((pad_end[None, :] <= p0[:, None]).astype(jnp.int32), axis=1)
    cb = jnp.minimum(cb, N_CLASS_IDS - 1)
    r0 = p0 - pad_start[cb]
    nv = jnp.where(real, jnp.clip(counts[cb] - r0, 0, MOE_ROWS), 0).astype(jnp.int32)
    grp = cb >> 6
    blk_a = grp * EXPERTS_PER_GROUP + ((cb >> 3) & 7)
    blk_b = grp * EXPERTS_PER_GROUP + (cb & 7)
    r = jnp.arange(MOE_ROWS, dtype=jnp.int32)[None, :]
    src = jnp.clip(start[cb][:, None] + r0[:, None] + r, 0, n_all - 1)
    blk_tok = order[src]

    never = jnp.int32(n_blocks)
    experts = jnp.arange(N_EXPERTS, dtype=jnp.int32)[:, None]
    used = real[None, :] & ((blk_a[None, :] == experts) | (blk_b[None, :] == experts))
    first_use = jnp.min(jnp.where(used, kidx[None, :], never), axis=1)
    new_a = real & (first_use[blk_a] == kidx)
    new_b = real & (first_use[blk_b] == kidx)
    filler = new_a & new_b
    blk_step = kidx + jnp.cumsum(filler.astype(jnp.int32))
    n_act = nblk + jnp.sum(filler.astype(jnp.int32))

    n_steps = n_blocks + N_EXPERTS // 2 + 1
    step = jnp.arange(n_steps, dtype=jnp.int32)
    k_done = jnp.sum((real[None, :] & (blk_step[None, :] <= step[:, None])).astype(jnp.int32), axis=1) - 1
    kc = jnp.clip(k_done, 0, n_blocks - 1)
    kn = jnp.clip(k_done + 1, 0, n_blocks - 1)
    active = step < n_act
    is_block = active & (k_done >= 0) & (blk_step[kc] == step)
    is_filler = active & ~is_block

    spare = n_all + (step[:, None] & (PAD_BANKS - 1)) * MOE_ROWS + r
    rows = jnp.where(is_block[:, None] & (r < nv[kc][:, None]), blk_tok[kc], spare).astype(jnp.int32)
    last = jnp.clip(nblk - 1, 0, n_blocks - 1)
    ea = jnp.where(is_block, blk_a[kc], jnp.where(is_filler, blk_a[kn], blk_a[last]))
    eb = jnp.where(is_block, blk_b[kc], jnp.where(is_filler, blk_a[kn], blk_b[last]))
    staged_here = jnp.where(
        is_block, jnp.where(new_b[kc], blk_b[kc], jnp.where(new_a[kc], blk_a[kc], -1)),
        jnp.where(is_filler, blk_a[kn], -1))
    flag = (staged_here >= 0).astype(jnp.int32)
    far = jnp.int32(n_steps)
    next_evt = lax.cummin(jnp.where(flag > 0, step, far), axis=0, reverse=True)
    last_evt = jnp.max(jnp.where(flag > 0, step, 0))
    stage = staged_here[jnp.where(next_evt < far, next_evt, last_evt)]

    sink = n_all + SINK_BANK * MOE_ROWS + r
    source = n_all + SOURCE_BANK * MOE_ROWS + r
    nxt = jnp.concatenate([rows[1:], rows[-1:]], axis=0)
    prv = jnp.concatenate([rows[:1], rows[:-1]], axis=0)
    gtok = jnp.where((step + 1 < n_act)[:, None], nxt, source)
    stok = jnp.where(((step >= 1) & (step <= n_act))[:, None], prv, sink)
    shape3 = (n_steps, 1, MOE_ROWS)
    rows, gtok, stok = rows * SUBLANES, gtok * SUBLANES, stok * SUBLANES
    chg = jnp.stack([jnp.concatenate([jnp.ones((1,), jnp.int32), (e[1:] != e[:-1]).astype(jnp.int32)])
                     for e in (ea, eb)])
    return (rows[:1].reshape(1, 1, MOE_ROWS), gtok.reshape(shape3), stok.reshape(shape3), ea, eb, chg,
            stage, flag, n_act.reshape(1).astype(jnp.int32))


def _over_sublanes(w):
    return jnp.broadcast_to(w[:, None, :], (w.shape[0], SUBLANES, w.shape[1]))


def kernel(x, meta_tokens, norm_mix_g, w_in, conf_dw_w, conf_dw_b, conf_ln_g, conf_ln_b, sc_conv_w, w_out,
           norm_ffn_g, w_router_group, b_router_group, w_router_expert, b_router_expert, w_exp_gate,
           w_exp_up, w_exp_down, final_norm_g):
    bsz, seq, d = x.shape
    depth = w_in.shape[0]
    n_main = bsz * seq
    n_all = n_main + N_META
    n_blocks = n_all // MOE_ROWS + N_CLASSES
    f32 = jnp.float32

    zero_halo = (jnp.zeros((HALO_A, CONF_W), f32), jnp.zeros((HALO_C, SC_W), f32))
    assert d == SUBLANES * LANES
    h_all = jnp.concatenate(
        [x.reshape(n_main, d), meta_tokens.astype(x.dtype), jnp.zeros((SPARE_ROWS, d), x.dtype)], axis=0)
    h_all = h_all.reshape(-1, LANES)

    for l in range(depth):
        w_r = jnp.concatenate([w_router_group[l], w_router_expert[l]], axis=1).astype(f32)
        w_r = jnp.pad(w_r, ((0, 0), (0, ROUTER_COLS - w_r.shape[1])))
        wr_hi = w_r.astype(jnp.bfloat16)
        wr_lo = (w_r - wr_hi.astype(f32)).astype(jnp.bfloat16)
        b_r = jnp.concatenate([b_router_group[l], b_router_expert[l]]).astype(f32)
        b_r = jnp.pad(b_r, (0, ROUTER_COLS - b_r.shape[0])).reshape(1, ROUTER_COLS)
        g_ffn = norm_ffn_g[l].reshape(1, d)
        wts = (norm_mix_g[l].reshape(1, d), w_in[l].astype(jnp.bfloat16), _over_sublanes(conf_dw_w[l]),
               conf_dw_b[l].reshape(1, CONF_W), conf_ln_g[l].reshape(1, CONF_W),
               conf_ln_b[l].reshape(1, CONF_W), _over_sublanes(sc_conv_w[l]), w_out[l].astype(jnp.bfloat16),
               g_ffn, wr_hi, wr_lo, b_r)

        h_all, cls_meta, halo_a, halo_c = _mixer_call(
            h_all, *zero_halo, wts, row0=n_main, n_batch=1, n_tiles=1, tt=N_META)
        h_all, cls_main, _, _ = _mixer_call(
            h_all, halo_a[0], halo_c[0], wts, row0=0, n_batch=bsz, n_tiles=seq // MIX_TILE, tt=MIX_TILE)

        cls = jnp.concatenate([cls_main[:, 0], cls_meta[:, 0]])
        h_all = _moe_call(h_all, _dispatch_tables(cls, n_blocks), g_ffn, wr_hi, wr_lo, b_r,
                          w_exp_gate, w_exp_up, w_exp_down, layer=l)

    out = _final_norm_call(h_all, final_norm_g.reshape(1, d), n_main)
    return out.reshape(bsz, seq, d)
```

```python
import functools

import jax
import jax.numpy as jnp
from jax import lax
from jax.experimental import pallas as pl
from jax.experimental.pallas import tpu as pltpu

D_MODEL = 1024
N_META = 16
CONF_W = 512
SC_W = 512
PROJ_W = 2 * CONF_W + 3 * SC_W
CONF_KERNEL = 31
SC_KERNEL = 3
N_GROUPS = 4
EXPERTS_PER_GROUP = 8
N_EXPERTS = N_GROUPS * EXPERTS_PER_GROUP
D_EXPERT = 512
EPS = 1e-6

LANES = 128
SUBLANES = 8
HALO_A = 32
HALO_C = 8
CONV_ROWS = 32
MIX_TILE = 512
MOE_ROWS = 256
N_CLASS_IDS = N_GROUPS * 64
N_CLASSES = N_GROUPS * (EXPERTS_PER_GROUP * (EXPERTS_PER_GROUP - 1) // 2)
VMEM_LIMIT = 56 * 1024 * 1024
ROUTER_COLS = LANES
PAD_BANKS = 4
SINK_BANK = PAD_BANKS
SOURCE_BANK = PAD_BANKS + 1
TAIL_TILES = 4
SPARE_ROWS = TAIL_TILES * MIX_TILE - N_META
assert SPARE_ROWS >= (PAD_BANKS + 2) * MOE_ROWS


def _load_tokens(ref, n):
    return jnp.concatenate([ref[pl.ds(s, n, stride=SUBLANES), :] for s in range(SUBLANES)], axis=-1)


def _store_tokens(ref, x, n):
    for s in range(SUBLANES):
        ref[pl.ds(s, n, stride=SUBLANES), :] = x[:, s * LANES:(s + 1) * LANES]


def _rms(x, g):
    return x * lax.rsqrt(jnp.mean(x * x, axis=-1, keepdims=True) + EPS) * g


def _router_logits(hn, w_hi_ref, w_lo_ref, b_ref):
    hi = hn.astype(jnp.bfloat16)
    lo = (hn - hi.astype(jnp.float32)).astype(jnp.bfloat16)
    acc = jnp.dot(hi, w_hi_ref[...], preferred_element_type=jnp.float32)
    acc = acc + jnp.dot(lo, w_hi_ref[...], preferred_element_type=jnp.float32)
    acc = acc + jnp.dot(hi, w_lo_ref[...], preferred_element_type=jnp.float32)
    return acc + b_ref[...]


N_MIXER_INPUTS = 15


def _mixer_kernel(h_ref, *refs, tt, src_tm, dst_tm, n_batch, has_tail):
    if not has_tail:
        return _mixer_tile(h_ref, *refs, tt=tt, src_tm=src_tm, dst_tm=dst_tm)
    tail_ref, refs = refs[0], refs[1:]
    b = pl.program_id(0)
    pl.when(b < n_batch)(lambda: _mixer_tile(h_ref, *refs, tt=tt, src_tm=src_tm, dst_tm=dst_tm))

    @pl.when(b == n_batch)
    def _():
        hout_ref = refs[N_MIXER_INPUTS - 1]
        _store_tokens(hout_ref, tail_ref[...], tt)


def _mixer_tile(h_ref, halo_a_ref, halo_c_ref, g_mix_ref, w_in_ref, dw_w_ref, dw_b_ref, ln_g_ref,
                ln_b_ref, sc_w_ref, w_out_ref, g_ffn_ref, wr_hi_ref, wr_lo_ref, br_ref,
                hout_ref, cls_ref, halo_a_out, halo_c_out,
                aext, ashift, cext, sb_buf, ybuf, *, tt, src_tm, dst_tm):
    j = pl.program_id(1)

    @pl.when(j == 0)
    def _():
        aext[0:HALO_A, :] = halo_a_ref[...]
        cext[0:HALO_C, :] = halo_c_ref[...]

    h = _load_tokens(h_ref, tt) if src_tm else h_ref[...]
    xn = _rms(h, g_mix_ref[...]).astype(jnp.bfloat16)
    ug = jnp.dot(xn, w_in_ref[:, 0:2 * CONF_W], preferred_element_type=jnp.float32)
    aext[HALO_A:HALO_A + tt, :] = ug[:, 0:CONF_W] * jax.nn.sigmoid(ug[:, CONF_W:2 * CONF_W])
    bcv = jnp.dot(xn, w_in_ref[:, 2 * CONF_W:PROJ_W], preferred_element_type=jnp.float32)
    sb_buf[...] = bcv[:, 0:SC_W]
    cext[HALO_C:HALO_C + tt, :] = bcv[:, SC_W:2 * SC_W] * bcv[:, 2 * SC_W:3 * SC_W]

    n_shift_rows = HALO_A + tt - SUBLANES
    for p in range(1, SUBLANES):
        ashift[p - 1, 0:n_shift_rows, :] = aext[p:p + n_shift_rows, :]

    ch = min(tt, CONV_ROWS)
    tiles = (ch // SUBLANES, SUBLANES, CONF_W)
    for c0 in range(0, tt, ch):
        acc = jnp.broadcast_to(dw_b_ref[...], tiles)
        for k in range(CONF_KERNEL):
            row = HALO_A - (CONF_KERNEL - 1) + c0 + k
            phase = row % SUBLANES
            if phase == 0:
                tap = aext[row:row + ch, :]
            else:
                tap = ashift[phase - 1, row - phase:row - phase + ch, :]
            acc = acc + tap.reshape(tiles) * dw_w_ref[k]
        acc = acc.reshape(ch, CONF_W)
        mu = jnp.mean(acc, axis=-1, keepdims=True)
        xc = acc - mu
        ln = xc * lax.rsqrt(jnp.mean(xc * xc, axis=-1, keepdims=True) + EPS) * ln_g_ref[...] + ln_b_ref[...]
        ybuf[c0:c0 + ch, 0:CONF_W] = jax.nn.silu(ln).astype(jnp.bfloat16)
        sacc = None
        for k in range(SC_KERNEL):
            row = HALO_C - (SC_KERNEL - 1) + c0 + k
            term = cext[row:row + ch, :].reshape(tiles) * sc_w_ref[k]
            sacc = term if sacc is None else sacc + term
        ybuf[c0:c0 + ch, CONF_W:CONF_W + SC_W] = (
            sb_buf[c0:c0 + ch, :] * sacc.reshape(ch, SC_W)).astype(jnp.bfloat16)

    new_halo_a = aext[tt:tt + HALO_A, :]
    new_halo_c = cext[tt:tt + HALO_C, :]
    aext[0:HALO_A, :] = new_halo_a
    cext[0:HALO_C, :] = new_halo_c
    halo_a_out[0] = new_halo_a
    halo_c_out[0] = new_halo_c

    h_new = h + jnp.dot(ybuf[...], w_out_ref[...], preferred_element_type=jnp.float32)
    if dst_tm:
        _store_tokens(hout_ref, h_new, tt)
    else:
        hout_ref[...] = h_new

    logits = _router_logits(_rms(h_new, g_ffn_ref[...]), wr_hi_ref, wr_lo_ref, br_ref)
    lane = lax.broadcasted_iota(jnp.int32, logits.shape, 1)
    neg = jnp.float32(-jnp.inf)
    big = jnp.int32(1 << 20)
    glog = jnp.where(lane < N_GROUPS, logits, neg)
    gmax = jnp.max(glog, axis=-1, keepdims=True)
    g_idx = jnp.min(jnp.where(glog == gmax, lane, big), axis=-1, keepdims=True)
    el = lane - N_GROUPS
    in_group = (el >= 0) & (el < N_EXPERTS) & ((el >> 3) == g_idx)
    l1 = jnp.where(in_group, logits, neg)
    m1 = jnp.max(l1, axis=-1, keepdims=True)
    i1 = jnp.min(jnp.where(l1 == m1, el, big), axis=-1, keepdims=True)
    l2 = jnp.where(el == i1, neg, l1)
    m2 = jnp.max(l2, axis=-1, keepdims=True)
    i2 = jnp.min(jnp.where(l2 == m2, el, big), axis=-1, keepdims=True)
    lo_e = jnp.minimum(i1, i2) & (EXPERTS_PER_GROUP - 1)
    hi_e = jnp.maximum(i1, i2) & (EXPERTS_PER_GROUP - 1)
    cls_ref[...] = g_idx * 64 + lo_e * 8 + hi_e


def _mixer_call(h_src, halo_a, halo_c, wts, *, row0, n_batch, n_tiles, tt, fresh=None, tail=None):
    blk0 = row0 // tt
    n_rows = n_batch * n_tiles * tt
    last = n_batch * n_tiles - 1
    tail_tiles = 0 if tail is None else tail.shape[0] // tt
    const = lambda shape: pl.BlockSpec(shape, lambda b, j: (0,) * len(shape))
    tile = lambda b, j: jnp.minimum(b * n_tiles + j, last)
    tm_spec = pl.BlockSpec((tt * SUBLANES, LANES), lambda b, j: (blk0 + b * n_tiles + j, 0))
    row_spec = pl.BlockSpec((tt, D_MODEL), lambda b, j: (blk0 + tile(b, j), 0))
    if fresh is None:
        in_spec, out_spec, out_struct = tm_spec, tm_spec, jax.ShapeDtypeStruct(h_src.shape, jnp.float32)
    elif fresh == "rows":
        in_spec, out_spec, out_struct = row_spec, row_spec, jax.ShapeDtypeStruct(h_src.shape, jnp.float32)
    else:
        in_spec = row_spec
        tail_at = lambda b, j: jnp.where(b == n_batch, jnp.minimum(j, tail_tiles - 1), j)
        out_spec = pl.BlockSpec((tt * SUBLANES, LANES), lambda b, j: (b * n_tiles + tail_at(b, j), 0))
        out_struct = jax.ShapeDtypeStruct(((n_rows + tail.shape[0]) * SUBLANES, LANES), jnp.float32)
    tail_specs, tail_args = [], []
    if tail is not None:
        tail_specs = [pl.BlockSpec(
            (tt, D_MODEL), lambda b, j: (jnp.where(b == n_batch, jnp.minimum(j, tail_tiles - 1), 0), 0))]
        tail_args = [tail]
    return pl.pallas_call(
        functools.partial(_mixer_kernel, tt=tt, src_tm=fresh is None, dst_tm=fresh != "rows",
                          n_batch=n_batch, has_tail=tail is not None),
        grid=(n_batch + (tail is not None), n_tiles),
        in_specs=[
            in_spec, *tail_specs,
            const((HALO_A, CONF_W)), const((HALO_C, SC_W)),
            const((1, D_MODEL)), const((D_MODEL, PROJ_W)),
            const((CONF_KERNEL, SUBLANES, CONF_W)), const((1, CONF_W)), const((1, CONF_W)), const((1, CONF_W)),
            const((SC_KERNEL, SUBLANES, SC_W)), const((CONF_W + SC_W, D_MODEL)), const((1, D_MODEL)),
            const((D_MODEL, ROUTER_COLS)), const((D_MODEL, ROUTER_COLS)), const((1, ROUTER_COLS)),
        ],
        out_specs=[
            out_spec,
            pl.BlockSpec((tt, 1), lambda b, j: (tile(b, j), 0)),
            pl.BlockSpec((1, HALO_A, CONF_W), lambda b, j: (jnp.minimum(b, n_batch - 1), 0, 0)),
            pl.BlockSpec((1, HALO_C, SC_W), lambda b, j: (jnp.minimum(b, n_batch - 1), 0, 0)),
        ],
        out_shape=[
            out_struct,
            jax.ShapeDtypeStruct((n_rows, 1), jnp.int32),
            jax.ShapeDtypeStruct((n_batch, HALO_A, CONF_W), jnp.float32),
            jax.ShapeDtypeStruct((n_batch, HALO_C, SC_W), jnp.float32),
        ],
        scratch_shapes=[
            pltpu.VMEM((HALO_A + tt, CONF_W), jnp.float32),
            pltpu.VMEM((SUBLANES - 1, HALO_A + tt, CONF_W), jnp.float32),
            pltpu.VMEM((HALO_C + tt, SC_W), jnp.float32),
            pltpu.VMEM((tt, SC_W), jnp.float32),
            pltpu.VMEM((tt, CONF_W + SC_W), jnp.bfloat16),
        ],
        input_output_aliases={0: 0} if fresh is None else {},
        compiler_params=pltpu.CompilerParams(
            dimension_semantics=("arbitrary", "arbitrary"), vmem_limit_bytes=VMEM_LIMIT),
        name=f"mixer_t{tt}",
    )(h_src, *tail_args, halo_a, halo_c, *wts)


def _moe_kernel(ea_ref, eb_ref, chg_ref, stage_ref, flag_ref, nblk_ref,
                tok0_ref, gtok_ref, stok_ref, h_in, g_ffn_ref, wr_hi_ref, wr_lo_ref, br_ref,
                wg_st, wu_st, wd_st,
                h_out,
                xbuf0, xbuf1, obuf0, obuf1, wg_c, wu_c, wd_c, wga, wua, wda, wgb, wub, wdb, gsem, ssem):
    i = pl.program_id(0)
    nblk = nblk_ref[0]
    xbuf = (xbuf0, xbuf1)
    obuf = (obuf0, obuf1)
    cur_a = (wga, wua, wda)
    cur_b = (wgb, wub, wdb)

    def token_tile(tok_ref, r):
        return pl.ds(pl.multiple_of(tok_ref[0, 0, r], SUBLANES), SUBLANES)

    def gather_row(tok_ref, r, dst_slot):
        return pltpu.make_async_copy(
            h_in.at[token_tile(tok_ref, r), :], xbuf[dst_slot].at[pl.ds(r * SUBLANES, SUBLANES), :],
            gsem.at[dst_slot])

    def scatter_row(r, src_slot):
        return pltpu.make_async_copy(
            obuf[src_slot].at[pl.ds(r * SUBLANES, SUBLANES), :], h_out.at[token_tile(stok_ref, r), :],
            ssem.at[src_slot])

    def wait_gather(dst_slot):
        pltpu.make_async_copy(
            h_in.at[pl.ds(0, MOE_ROWS * SUBLANES), :], xbuf[dst_slot], gsem.at[dst_slot]).wait()

    def wait_scatter(src_slot):
        pltpu.make_async_copy(
            obuf[src_slot], h_out.at[pl.ds(0, MOE_ROWS * SUBLANES), :], ssem.at[src_slot]).wait()

    @pl.when(i == 0)
    def _():
        obuf1[...] = jnp.zeros_like(obuf1)

        def start(r, c):
            gather_row(tok0_ref, r, 0).start()
            return c

        lax.fori_loop(0, MOE_ROWS, start, 0)

    def step(slot):
        other = 1 - slot
        ea = ea_ref[i]
        eb = eb_ref[i]
        wait_gather(slot)

        @pl.when(i > 0)
        def _():
            wait_scatter(slot)

        @pl.when(flag_ref[i] != 0)
        def _():
            dst = stage_ref[i] & (EXPERTS_PER_GROUP - 1)
            wg_c[dst] = wg_st[0, 0].astype(jnp.bfloat16)
            wu_c[dst] = wu_st[0, 0].astype(jnp.bfloat16)
            wd_c[dst] = wd_st[0, 0].astype(jnp.bfloat16)

        for e, changed, cur in ((ea, chg_ref[0, i], cur_a), (eb, chg_ref[1, i], cur_b)):
            @pl.when(changed != 0)
            def _(e=e, cur=cur):
                at = e & (EXPERTS_PER_GROUP - 1)
                cur[0][...] = wg_c[at]
                cur[1][...] = wu_c[at]
                cur[2][...] = wd_c[at]

        for r in range(MOE_ROWS):
            gather_row(gtok_ref, r, other).start()
        for r in range(MOE_ROWS):
            scatter_row(r, other).start()

        x = _load_tokens(xbuf[slot], MOE_ROWS)
        hn = _rms(x, g_ffn_ref[...])
        logits = _router_logits(hn, wr_hi_ref, wr_lo_ref, br_ref)
        lane = lax.broadcasted_iota(jnp.int32, logits.shape, 1)
        neg = jnp.float32(-jnp.inf)
        glog = jnp.where(lane < N_GROUPS, logits, neg)
        gmax = jnp.max(glog, axis=-1, keepdims=True)
        gsum = jnp.sum(jnp.exp(glog - gmax), axis=-1, keepdims=True)
        grp = ea >> 3
        l_g = jnp.sum(jnp.where(lane == grp, logits, 0.0), axis=-1, keepdims=True)
        p_g = jnp.exp(l_g - gmax) / gsum
        l_a = jnp.sum(jnp.where(lane == N_GROUPS + ea, logits, 0.0), axis=-1, keepdims=True)
        l_b = jnp.sum(jnp.where(lane == N_GROUPS + eb, logits, 0.0), axis=-1, keepdims=True)
        w_a = p_g / (1.0 + jnp.exp(l_b - l_a))
        w_b = p_g / (1.0 + jnp.exp(l_a - l_b))

        xb = hn.astype(jnp.bfloat16)

        def expert(cur):
            gate = jnp.dot(xb, cur[0][...], preferred_element_type=jnp.float32)
            up = jnp.dot(xb, cur[1][...], preferred_element_type=jnp.float32)
            hid = (jax.nn.silu(gate) * up).astype(jnp.bfloat16)
            return jnp.dot(hid, cur[2][...], preferred_element_type=jnp.float32)

        y = w_a * expert(cur_a) + w_b * expert(cur_b)
        _store_tokens(obuf[slot], x + y, MOE_ROWS)

        @pl.when(i == nblk)
        def _():
            wait_gather(other)
            wait_scatter(other)

    for parity in (0, 1):
        pl.when((i <= nblk) & ((i & 1) == parity))(functools.partial(step, parity))


def _moe_call(h_all, tables, g_ffn, wr_hi, wr_lo, br, w_gate, w_up, w_down, *, layer):
    tok0, gtok, stok, ea, eb, chg, stage, flag, nblk = tables
    n_steps = gtok.shape[0]
    const = lambda shape: pl.BlockSpec(shape, lambda i, *_: (0,) * len(shape))
    staged = lambda shape: pl.BlockSpec(
        (1, 1) + shape, lambda i, ea_, eb_, ch_, st_, fl_, nb_: (layer, st_[i], 0, 0))
    tok_spec = lambda imap: pl.BlockSpec((1, 1, MOE_ROWS), imap, memory_space=pltpu.SMEM)
    grid_spec = pltpu.PrefetchScalarGridSpec(
        num_scalar_prefetch=6,
        grid=(n_steps,),
        in_specs=[
            tok_spec(lambda i, *_: (0, 0, 0)), tok_spec(lambda i, *_: (i, 0, 0)), tok_spec(lambda i, *_: (i, 0, 0)),
            pl.BlockSpec(memory_space=pl.ANY),
            const((1, D_MODEL)), const((D_MODEL, ROUTER_COLS)), const((D_MODEL, ROUTER_COLS)),
            const((1, ROUTER_COLS)),
            staged((D_MODEL, D_EXPERT)), staged((D_MODEL, D_EXPERT)), staged((D_EXPERT, D_MODEL)),
        ],
        out_specs=pl.BlockSpec(memory_space=pl.ANY),
        scratch_shapes=[
            pltpu.VMEM((MOE_ROWS * SUBLANES, LANES), jnp.float32),
            pltpu.VMEM((MOE_ROWS * SUBLANES, LANES), jnp.float32),
            pltpu.VMEM((MOE_ROWS * SUBLANES, LANES), jnp.float32),
            pltpu.VMEM((MOE_ROWS * SUBLANES, LANES), jnp.float32),
            pltpu.VMEM((EXPERTS_PER_GROUP, D_MODEL, D_EXPERT), jnp.bfloat16),
            pltpu.VMEM((EXPERTS_PER_GROUP, D_MODEL, D_EXPERT), jnp.bfloat16),
            pltpu.VMEM((EXPERTS_PER_GROUP, D_EXPERT, D_MODEL), jnp.bfloat16),
            pltpu.VMEM((D_MODEL, D_EXPERT), jnp.bfloat16),
            pltpu.VMEM((D_MODEL, D_EXPERT), jnp.bfloat16),
            pltpu.VMEM((D_EXPERT, D_MODEL), jnp.bfloat16),
            pltpu.VMEM((D_MODEL, D_EXPERT), jnp.bfloat16),
            pltpu.VMEM((D_MODEL, D_EXPERT), jnp.bfloat16),
            pltpu.VMEM((D_EXPERT, D_MODEL), jnp.bfloat16),
            pltpu.SemaphoreType.DMA((2,)),
            pltpu.SemaphoreType.DMA((2,)),
        ],
    )
    return pl.pallas_call(
        _moe_kernel,
        grid_spec=grid_spec,
        out_shape=jax.ShapeDtypeStruct(h_all.shape, h_all.dtype),
        input_output_aliases={9: 0},
        compiler_params=pltpu.CompilerParams(
            dimension_semantics=("arbitrary",), vmem_limit_bytes=VMEM_LIMIT),
        name="moe_pairs",
    )(ea, eb, chg, stage, flag, nblk, tok0, gtok, stok, h_all, g_ffn, wr_hi, wr_lo, br, w_gate, w_up, w_down)


def _final_norm_kernel(h_ref, g_ref, o_ref):
    o_ref[...] = _rms(_load_tokens(h_ref, MIX_TILE), g_ref[...])


def _final_norm_call(h_all, g, n_rows):
    return pl.pallas_call(
        _final_norm_kernel,
        grid=(n_rows // MIX_TILE,),
        in_specs=[pl.BlockSpec((MIX_TILE * SUBLANES, LANES), lambda i: (i, 0)),
                  pl.BlockSpec((1, D_MODEL), lambda i: (0, 0))],
        out_specs=pl.BlockSpec((MIX_TILE, D_MODEL), lambda i: (i, 0)),
        out_shape=jax.ShapeDtypeStruct((n_rows, D_MODEL), jnp.float32),
        compiler_params=pltpu.CompilerParams(dimension_semantics=("arbitrary",)),
        name="final_norm",
    )(h_all, g)


def _dispatch_tables(cls, n_blocks):
    n_all = cls.shape[0]
    tok_bits = (n_all - 1).bit_length()
    key = jnp.sort((cls << tok_bits) | jnp.arange(n_all, dtype=jnp.int32))
    order = key & ((1 << tok_bits) - 1)
    ids = jnp.arange(N_CLASS_IDS, dtype=jnp.int32)
    counts = jnp.sum((cls[:, None] == ids[None, :]).astype(jnp.int32), axis=0)
    start = jnp.cumsum(counts) - counts
    padded = (counts + MOE_ROWS - 1) // MOE_ROWS * MOE_ROWS
    pad_end = jnp.cumsum(padded)
    pad_start = pad_end - padded
    nblk = (pad_end[-1] // MOE_ROWS).astype(jnp.int32)

    kidx = jnp.arange(n_blocks, dtype=jnp.int32)
    real = kidx < nblk
    p0 = kidx * MOE_ROWS
    cb = jnp.sum((pad_end[None, :] <= p0[:, None]).astype(jnp.int32), axis=1)
    cb = jnp.minimum(cb, N_CLASS_IDS - 1)
    r0 = p0 - pad_start[cb]
    nv = jnp.where(real, jnp.clip(counts[cb] - r0, 0, MOE_ROWS), 0).astype(jnp.int32)
    grp = cb >> 6
    blk_a = grp * EXPERTS_PER_GROUP + ((cb >> 3) & 7)
    blk_b = grp * EXPERTS_PER_GROUP + (cb & 7)
    r = jnp.arange(MOE_ROWS, dtype=jnp.int32)[None, :]
    blk_src0 = start[cb] + r0

    never = jnp.int32(n_blocks)
    experts = jnp.arange(N_EXPERTS, dtype=jnp.int32)[:, None]
    used = real[None, :] & ((blk_a[None, :] == experts) | (blk_b[None, :] == experts))
    first_use = jnp.min(jnp.where(used, kidx[None, :], never), axis=1)
    new_a = real & (first_use[blk_a] == kidx)
    new_b = real & (first_use[blk_b] == kidx)
    filler = new_a & new_b
    blk_step = kidx + jnp.cumsum(filler.astype(jnp.int32))
    n_act = nblk + jnp.sum(filler.astype(jnp.int32))

    n_steps = n_blocks + N_EXPERTS // 2 + 1
    step = jnp.arange(n_steps, dtype=jnp.int32)
    k_done = jnp.sum((real[None, :] & (blk_step[None, :] <= step[:, None])).astype(jnp.int32), axis=1) - 1
    kc = jnp.clip(k_done, 0, n_blocks - 1)
    kn = jnp.clip(k_done + 1, 0, n_blocks - 1)
    active = step < n_act
    is_block = active & (k_done >= 0) & (blk_step[kc] == step)
    is_filler = active & ~is_block

    spare = n_all + (step[:, None] & (PAD_BANKS - 1)) * MOE_ROWS + r
    src = jnp.clip(blk_src0[kc][:, None] + r, 0, n_all - 1)
    rows = jnp.where(is_block[:, None] & (r < nv[kc][:, None]), order[src], spare).astype(jnp.int32)
    last = jnp.clip(nblk - 1, 0, n_blocks - 1)
    ea = jnp.where(is_block, blk_a[kc], jnp.where(is_filler, blk_a[kn], blk_a[last]))
    eb = jnp.where(is_block, blk_b[kc], jnp.where(is_filler, blk_a[kn], blk_b[last]))
    staged_here = jnp.where(
        is_block, jnp.where(new_b[kc], blk_b[kc], jnp.where(new_a[kc], blk_a[kc], -1)),
        jnp.where(is_filler, blk_a[kn], -1))
    flag = (staged_here >= 0).astype(jnp.int32)
    far = jnp.int32(n_steps)
    next_evt = lax.cummin(jnp.where(flag > 0, step, far), axis=0, reverse=True)
    last_evt = jnp.max(jnp.where(flag > 0, step, 0))
    stage = staged_here[jnp.where(next_evt < far, next_evt, last_evt)]

    sink = n_all + SINK_BANK * MOE_ROWS + r
    source = n_all + SOURCE_BANK * MOE_ROWS + r
    nxt = jnp.concatenate([rows[1:], rows[-1:]], axis=0)
    prv = jnp.concatenate([rows[:1], rows[:-1]], axis=0)
    gtok = jnp.where((step + 1 < n_act)[:, None], nxt, source)
    stok = jnp.where(((step >= 1) & (step <= n_act))[:, None], prv, sink)
    shape3 = (n_steps, 1, MOE_ROWS)
    rows, gtok, stok = rows * SUBLANES, gtok * SUBLANES, stok * SUBLANES
    chg = jnp.stack([jnp.concatenate([jnp.ones((1,), jnp.int32), (e[1:] != e[:-1]).astype(jnp.int32)])
                     for e in (ea, eb)])
    return (rows[:1].reshape(1, 1, MOE_ROWS), gtok.reshape(shape3), stok.reshape(shape3), ea, eb, chg,
            stage, flag, n_act.reshape(1).astype(jnp.int32))


def _over_sublanes(w):
    return jnp.broadcast_to(w[:, None, :], (w.shape[0], SUBLANES, w.shape[1]))


def kernel(x, meta_tokens, norm_mix_g, w_in, conf_dw_w, conf_dw_b, conf_ln_g, conf_ln_b, sc_conv_w, w_out,
           norm_ffn_g, w_router_group, b_router_group, w_router_expert, b_router_expert, w_exp_gate,
           w_exp_up, w_exp_down, final_norm_g):
    bsz, seq, d = x.shape
    depth = w_in.shape[0]
    n_main = bsz * seq
    n_all = n_main + N_META
    n_blocks = n_all // MOE_ROWS + N_CLASSES
    f32 = jnp.float32

    zero_halo = (jnp.zeros((HALO_A, CONF_W), f32), jnp.zeros((HALO_C, SC_W), f32))
    assert d == SUBLANES * LANES
    h_all = None

    for l in range(depth):
        w_r = jnp.concatenate([w_router_group[l], w_router_expert[l]], axis=1).astype(f32)
        w_r = jnp.pad(w_r, ((0, 0), (0, ROUTER_COLS - w_r.shape[1])))
        wr_hi = w_r.astype(jnp.bfloat16)
        wr_lo = (w_r - wr_hi.astype(f32)).astype(jnp.bfloat16)
        b_r = jnp.concatenate([b_router_group[l], b_router_expert[l]]).astype(f32)
        b_r = jnp.pad(b_r, (0, ROUTER_COLS - b_r.shape[0])).reshape(1, ROUTER_COLS)
        g_ffn = norm_ffn_g[l].reshape(1, d)
        wts = (norm_mix_g[l].reshape(1, d), w_in[l].astype(jnp.bfloat16), _over_sublanes(conf_dw_w[l]),
               conf_dw_b[l].reshape(1, CONF_W), conf_ln_g[l].reshape(1, CONF_W),
               conf_ln_b[l].reshape(1, CONF_W), _over_sublanes(sc_conv_w[l]), w_out[l].astype(jnp.bfloat16),
               g_ffn, wr_hi, wr_lo, b_r)

        main = dict(row0=0, n_batch=bsz, n_tiles=seq // MIX_TILE, tt=MIX_TILE)
        if h_all is None:
            h_meta, cls_meta, halo_a, halo_c = _mixer_call(
                meta_tokens.astype(x.dtype), *zero_halo, wts, row0=0, n_batch=1, n_tiles=1, tt=N_META,
                fresh="rows")
            tail = jnp.concatenate([h_meta, jnp.zeros((SPARE_ROWS, d), x.dtype)], axis=0)
            h_all, cls_main, _, _ = _mixer_call(
                x.reshape(n_main, d), halo_a[0], halo_c[0], wts, fresh="tokens", tail=tail, **main)
        else:
            h_all, cls_meta, halo_a, halo_c = _mixer_call(
                h_all, *zero_halo, wts, row0=n_main, n_batch=1, n_tiles=1, tt=N_META)
            h_all, cls_main, _, _ = _mixer_call(h_all, halo_a[0], halo_c[0], wts, **main)

        cls = jnp.concatenate([cls_main[:, 0], cls_meta[:, 0]])
        h_all = _moe_call(h_all, _dispatch_tables(cls, n_blocks), g_ffn, wr_hi, wr_lo, b_r,
                          w_exp_gate, w_exp_up, w_exp_down, layer=l)

    out = _final_norm_call(h_all, final_norm_g.reshape(1, d), n_main)
    return out.reshape(bsz, seq, d)
```

```python
import functools

import jax
import jax.numpy as jnp
from jax import lax
from jax.experimental import pallas as pl
from jax.experimental.pallas import tpu as pltpu

D_MODEL = 1024
N_META = 16
CONF_W = 512
SC_W = 512
PROJ_W = 2 * CONF_W + 3 * SC_W
CONF_KERNEL = 31
SC_KERNEL = 3
N_GROUPS = 4
EXPERTS_PER_GROUP = 8
N_EXPERTS = N_GROUPS * EXPERTS_PER_GROUP
D_EXPERT = 512
EPS = 1e-6

LANES = 128
SUBLANES = 8
HALO_A = 32
HALO_C = 8
CONV_ROWS = 32
MIX_TILE = 512
MOE_ROWS = 256
N_CLASS_IDS = N_GROUPS * 64
N_CLASSES = N_GROUPS * (EXPERTS_PER_GROUP * (EXPERTS_PER_GROUP - 1) // 2)
VMEM_LIMIT = 56 * 1024 * 1024
ROUTER_COLS = LANES
PAD_BANKS = 4
SINK_BANK = PAD_BANKS
SOURCE_BANK = PAD_BANKS + 1
TAIL_TILES = 4
SPARE_ROWS = TAIL_TILES * MIX_TILE - N_META
assert SPARE_ROWS >= (PAD_BANKS + 2) * MOE_ROWS


def _load_tokens(ref, n):
    return jnp.concatenate([ref[pl.ds(s, n, stride=SUBLANES), :] for s in range(SUBLANES)], axis=-1)


def _store_tokens(ref, x, n):
    for s in range(SUBLANES):
        ref[pl.ds(s, n, stride=SUBLANES), :] = x[:, s * LANES:(s + 1) * LANES]


def _rms(x, g):
    return x * lax.rsqrt(jnp.mean(x * x, axis=-1, keepdims=True) + EPS) * g


def _router_logits(hn, w_hi_ref, w_lo_ref, b_ref):
    hi = hn.astype(jnp.bfloat16)
    lo = (hn - hi.astype(jnp.float32)).astype(jnp.bfloat16)
    acc = jnp.dot(hi, w_hi_ref[...], preferred_element_type=jnp.float32)
    acc = acc + jnp.dot(lo, w_hi_ref[...], preferred_element_type=jnp.float32)
    acc = acc + jnp.dot(hi, w_lo_ref[...], preferred_element_type=jnp.float32)
    return acc + b_ref[...]


N_MIXER_INPUTS = 15


def _mixer_kernel(h_ref, *refs, tt, src_tm, dst_tm, n_batch, has_tail):
    if not has_tail:
        return _mixer_tile(h_ref, *refs, tt=tt, src_tm=src_tm, dst_tm=dst_tm)
    tail_ref, refs = refs[0], refs[1:]
    b = pl.program_id(0)
    pl.when(b < n_batch)(lambda: _mixer_tile(h_ref, *refs, tt=tt, src_tm=src_tm, dst_tm=dst_tm))

    @pl.when(b == n_batch)
    def _():
        hout_ref = refs[N_MIXER_INPUTS - 1]
        _store_tokens(hout_ref, tail_ref[...], tt)


def _mixer_tile(h_ref, halo_a_ref, halo_c_ref, g_mix_ref, w_in_ref, dw_w_ref, dw_b_ref, ln_g_ref,
                ln_b_ref, sc_w_ref, w_out_ref, g_ffn_ref, wr_hi_ref, wr_lo_ref, br_ref,
                hout_ref, cls_ref, halo_a_out, halo_c_out,
                aext, ashift, cext, sb_buf, ybuf, *, tt, src_tm, dst_tm):
    j = pl.program_id(1)

    @pl.when(j == 0)
    def _():
        aext[0:HALO_A, :] = halo_a_ref[...]
        cext[0:HALO_C, :] = halo_c_ref[...]

    h = _load_tokens(h_ref, tt) if src_tm else h_ref[...]
    xn = _rms(h, g_mix_ref[...]).astype(jnp.bfloat16)
    ug = jnp.dot(xn, w_in_ref[:, 0:2 * CONF_W], preferred_element_type=jnp.float32)
    aext[HALO_A:HALO_A + tt, :] = ug[:, 0:CONF_W] * jax.nn.sigmoid(ug[:, CONF_W:2 * CONF_W])
    bcv = jnp.dot(xn, w_in_ref[:, 2 * CONF_W:PROJ_W], preferred_element_type=jnp.float32)
    sb_buf[...] = bcv[:, 0:SC_W]
    cext[HALO_C:HALO_C + tt, :] = bcv[:, SC_W:2 * SC_W] * bcv[:, 2 * SC_W:3 * SC_W]

    n_shift_rows = HALO_A + tt - SUBLANES
    for p in range(1, SUBLANES):
        ashift[p - 1, 0:n_shift_rows, :] = aext[p:p + n_shift_rows, :]

    ch = min(tt, CONV_ROWS)
    tiles = (ch // SUBLANES, SUBLANES, CONF_W)
    for c0 in range(0, tt, ch):
        acc = jnp.broadcast_to(dw_b_ref[...], tiles)
        for k in range(CONF_KERNEL):
            row = HALO_A - (CONF_KERNEL - 1) + c0 + k
            phase = row % SUBLANES
            if phase == 0:
                tap = aext[row:row + ch, :]
            else:
                tap = ashift[phase - 1, row - phase:row - phase + ch, :]
            acc = acc + tap.reshape(tiles) * dw_w_ref[k]
        acc = acc.reshape(ch, CONF_W)
        mu = jnp.mean(acc, axis=-1, keepdims=True)
        xc = acc - mu
        ln = xc * lax.rsqrt(jnp.mean(xc * xc, axis=-1, keepdims=True) + EPS) * ln_g_ref[...] + ln_b_ref[...]
        ybuf[c0:c0 + ch, 0:CONF_W] = jax.nn.silu(ln).astype(jnp.bfloat16)
        sacc = None
        for k in range(SC_KERNEL):
            row = HALO_C - (SC_KERNEL - 1) + c0 + k
            term = cext[row:row + ch, :].reshape(tiles) * sc_w_ref[k]
            sacc = term if sacc is None else sacc + term
        ybuf[c0:c0 + ch, CONF_W:CONF_W + SC_W] = (
            sb_buf[c0:c0 + ch, :] * sacc.reshape(ch, SC_W)).astype(jnp.bfloat16)

    new_halo_a = aext[tt:tt + HALO_A, :]
    new_halo_c = cext[tt:tt + HALO_C, :]
    aext[0:HALO_A, :] = new_halo_a
    cext[0:HALO_C, :] = new_halo_c
    halo_a_out[0] = new_halo_a
    halo_c_out[0] = new_halo_c

    h_new = h + jnp.dot(ybuf[...], w_out_ref[...], preferred_element_type=jnp.float32)
    if dst_tm:
        _store_tokens(hout_ref, h_new, tt)
    else:
        hout_ref[...] = h_new

    logits = _router_logits(_rms(h_new, g_ffn_ref[...]), wr_hi_ref, wr_lo_ref, br_ref)
    lane = lax.broadcasted_iota(jnp.int32, logits.shape, 1)
    neg = jnp.float32(-jnp.inf)
    big = jnp.int32(1 << 20)
    glog = jnp.where(lane < N_GROUPS, logits, neg)
    gmax = jnp.max(glog, axis=-1, keepdims=True)
    g_idx = jnp.min(jnp.where(glog == gmax, lane, big), axis=-1, keepdims=True)
    el = lane - N_GROUPS
    in_group = (el >= 0) & (el < N_EXPERTS) & ((el >> 3) == g_idx)
    l1 = jnp.where(in_group, logits, neg)
    m1 = jnp.max(l1, axis=-1, keepdims=True)
    i1 = jnp.min(jnp.where(l1 == m1, el, big), axis=-1, keepdims=True)
    l2 = jnp.where(el == i1, neg, l1)
    m2 = jnp.max(l2, axis=-1, keepdims=True)
    i2 = jnp.min(jnp.where(l2 == m2, el, big), axis=-1, keepdims=True)
    lo_e = jnp.minimum(i1, i2) & (EXPERTS_PER_GROUP - 1)
    hi_e = jnp.maximum(i1, i2) & (EXPERTS_PER_GROUP - 1)
    cls_ref[...] = g_idx * 64 + lo_e * 8 + hi_e


def _mixer_call(h_src, halo_a, halo_c, wts, *, row0, n_batch, n_tiles, tt, fresh=None, tail=None):
    blk0 = row0 // tt
    n_rows = n_batch * n_tiles * tt
    last = n_batch * n_tiles - 1
    tail_tiles = 0 if tail is None else tail.shape[0] // tt
    const = lambda shape: pl.BlockSpec(shape, lambda b, j: (0,) * len(shape))
    tile = lambda b, j: jnp.minimum(b * n_tiles + j, last)
    tm_spec = pl.BlockSpec((tt * SUBLANES, LANES), lambda b, j: (blk0 + b * n_tiles + j, 0))
    row_spec = pl.BlockSpec((tt, D_MODEL), lambda b, j: (blk0 + tile(b, j), 0))
    if fresh is None:
        in_spec, out_spec, out_struct = tm_spec, tm_spec, jax.ShapeDtypeStruct(h_src.shape, jnp.float32)
    elif fresh == "rows":
        in_spec, out_spec, out_struct = row_spec, row_spec, jax.ShapeDtypeStruct(h_src.shape, jnp.float32)
    else:
        in_spec = row_spec
        tail_at = lambda b, j: jnp.where(b == n_batch, jnp.minimum(j, tail_tiles - 1), j)
        out_spec = pl.BlockSpec((tt * SUBLANES, LANES), lambda b, j: (b * n_tiles + tail_at(b, j), 0))
        out_struct = jax.ShapeDtypeStruct(((n_rows + tail.shape[0]) * SUBLANES, LANES), jnp.float32)
    tail_specs, tail_args = [], []
    if tail is not None:
        tail_specs = [pl.BlockSpec(
            (tt, D_MODEL), lambda b, j: (jnp.where(b == n_batch, jnp.minimum(j, tail_tiles - 1), 0), 0))]
        tail_args = [tail]
    return pl.pallas_call(
        functools.partial(_mixer_kernel, tt=tt, src_tm=fresh is None, dst_tm=fresh != "rows",
                          n_batch=n_batch, has_tail=tail is not None),
        grid=(n_batch + (tail is not None), n_tiles),
        in_specs=[
            in_spec, *tail_specs,
            const((HALO_A, CONF_W)), const((HALO_C, SC_W)),
            const((1, D_MODEL)), const((D_MODEL, PROJ_W)),
            const((CONF_KERNEL, SUBLANES, CONF_W)), const((1, CONF_W)), const((1, CONF_W)), const((1, CONF_W)),
            const((SC_KERNEL, SUBLANES, SC_W)), const((CONF_W + SC_W, D_MODEL)), const((1, D_MODEL)),
            const((D_MODEL, ROUTER_COLS)), const((D_MODEL, ROUTER_COLS)), const((1, ROUTER_COLS)),
        ],
        out_specs=[
            out_spec,
            pl.BlockSpec((tt, 1), lambda b, j: (tile(b, j), 0)),
            pl.BlockSpec((1, HALO_A, CONF_W), lambda b, j: (jnp.minimum(b, n_batch - 1), 0, 0)),
            pl.BlockSpec((1, HALO_C, SC_W), lambda b, j: (jnp.minimum(b, n_batch - 1), 0, 0)),
        ],
        out_shape=[
            out_struct,
            jax.ShapeDtypeStruct((n_rows, 1), jnp.int32),
            jax.ShapeDtypeStruct((n_batch, HALO_A, CONF_W), jnp.float32),
            jax.ShapeDtypeStruct((n_batch, HALO_C, SC_W), jnp.float32),
        ],
        scratch_shapes=[
            pltpu.VMEM((HALO_A + tt, CONF_W), jnp.float32),
            pltpu.VMEM((SUBLANES - 1, HALO_A + tt, CONF_W), jnp.float32),
            pltpu.VMEM((HALO_C + tt, SC_W), jnp.float32),
            pltpu.VMEM((tt, SC_W), jnp.float32),
            pltpu.VMEM((tt, CONF_W + SC_W), jnp.bfloat16),
        ],
        input_output_aliases={0: 0} if fresh is None else {},
        compiler_params=pltpu.CompilerParams(
            dimension_semantics=("arbitrary", "arbitrary"), vmem_limit_bytes=VMEM_LIMIT),
        name=f"mixer_t{tt}",
    )(h_src, *tail_args, halo_a, halo_c, *wts)


def _moe_kernel(ea_ref, eb_ref, chg_ref, stage_ref, flag_ref, nblk_ref,
                tok0_ref, gtok_ref, stok_ref, h_in, g_ffn_ref, wr_ref, br_ref,
                wg_st, wu_st, wd_st,
                h_out,
                xbuf0, xbuf1, obuf0, obuf1, wg_c, wu_c, wd_c, wga, wua, wda, wgb, wub, wdb, gsem, ssem):
    i = pl.program_id(0)
    nblk = nblk_ref[0]
    xbuf = (xbuf0, xbuf1)
    obuf = (obuf0, obuf1)
    cur_a = (wga, wua, wda)
    cur_b = (wgb, wub, wdb)

    def token_tile(tok_ref, r):
        return pl.ds(pl.multiple_of(tok_ref[0, 0, r], SUBLANES), SUBLANES)

    def gather_row(tok_ref, r, dst_slot):
        return pltpu.make_async_copy(
            h_in.at[token_tile(tok_ref, r), :], xbuf[dst_slot].at[pl.ds(r * SUBLANES, SUBLANES), :],
            gsem.at[dst_slot])

    def scatter_row(r, src_slot):
        return pltpu.make_async_copy(
            obuf[src_slot].at[pl.ds(r * SUBLANES, SUBLANES), :], h_out.at[token_tile(stok_ref, r), :],
            ssem.at[src_slot])

    def wait_gather(dst_slot):
        pltpu.make_async_copy(
            h_in.at[pl.ds(0, MOE_ROWS * SUBLANES), :], xbuf[dst_slot], gsem.at[dst_slot]).wait()

    def wait_scatter(src_slot):
        pltpu.make_async_copy(
            obuf[src_slot], h_out.at[pl.ds(0, MOE_ROWS * SUBLANES), :], ssem.at[src_slot]).wait()

    @pl.when(i == 0)
    def _():
        obuf1[...] = jnp.zeros_like(obuf1)

        def start(r, c):
            gather_row(tok0_ref, r, 0).start()
            return c

        lax.fori_loop(0, MOE_ROWS, start, 0)

    def step(slot):
        other = 1 - slot
        ea = ea_ref[i]
        eb = eb_ref[i]
        wait_gather(slot)

        @pl.when(i > 0)
        def _():
            wait_scatter(slot)

        @pl.when(flag_ref[i] != 0)
        def _():
            dst = stage_ref[i] & (EXPERTS_PER_GROUP - 1)
            wg_c[dst] = wg_st[0, 0].astype(jnp.bfloat16)
            wu_c[dst] = wu_st[0, 0].astype(jnp.bfloat16)
            wd_c[dst] = wd_st[0, 0].astype(jnp.bfloat16)

        for e, changed, cur in ((ea, chg_ref[0, i], cur_a), (eb, chg_ref[1, i], cur_b)):
            @pl.when(changed != 0)
            def _(e=e, cur=cur):
                at = e & (EXPERTS_PER_GROUP - 1)
                cur[0][...] = wg_c[at]
                cur[1][...] = wu_c[at]
                cur[2][...] = wd_c[at]

        for r in range(MOE_ROWS):
            gather_row(gtok_ref, r, other).start()
        for r in range(MOE_ROWS):
            scatter_row(r, other).start()

        x = _load_tokens(xbuf[slot], MOE_ROWS)
        xb = _rms(x, g_ffn_ref[...]).astype(jnp.bfloat16)
        logits = jnp.dot(xb, wr_ref[...], preferred_element_type=jnp.float32) + br_ref[...]
        lane = lax.broadcasted_iota(jnp.int32, logits.shape, 1)
        neg = jnp.float32(-jnp.inf)
        glog = jnp.where(lane < N_GROUPS, logits, neg)
        gmax = jnp.max(glog, axis=-1, keepdims=True)
        gsum = jnp.sum(jnp.exp(glog - gmax), axis=-1, keepdims=True)
        grp = ea >> 3
        l_g = jnp.sum(jnp.where(lane == grp, logits, 0.0), axis=-1, keepdims=True)
        p_g = jnp.exp(l_g - gmax) / gsum
        l_a = jnp.sum(jnp.where(lane == N_GROUPS + ea, logits, 0.0), axis=-1, keepdims=True)
        l_b = jnp.sum(jnp.where(lane == N_GROUPS + eb, logits, 0.0), axis=-1, keepdims=True)
        w_a = p_g / (1.0 + jnp.exp(l_b - l_a))
        w_b = p_g / (1.0 + jnp.exp(l_a - l_b))

        def expert(cur):
            gate = jnp.dot(xb, cur[0][...], preferred_element_type=jnp.float32)
            up = jnp.dot(xb, cur[1][...], preferred_element_type=jnp.float32)
            hid = (jax.nn.silu(gate) * up).astype(jnp.bfloat16)
            return jnp.dot(hid, cur[2][...], preferred_element_type=jnp.float32)

        y = w_a * expert(cur_a) + w_b * expert(cur_b)
        _store_tokens(obuf[slot], x + y, MOE_ROWS)

        @pl.when(i == nblk)
        def _():
            wait_gather(other)
            wait_scatter(other)

    for parity in (0, 1):
        pl.when((i <= nblk) & ((i & 1) == parity))(functools.partial(step, parity))


def _moe_call(h_all, tables, g_ffn, wr, br, w_gate, w_up, w_down, *, layer):
    tok0, gtok, stok, ea, eb, chg, stage, flag, nblk = tables
    n_steps = gtok.shape[0]
    const = lambda shape: pl.BlockSpec(shape, lambda i, *_: (0,) * len(shape))
    staged = lambda shape: pl.BlockSpec(
        (1, 1) + shape, lambda i, ea_, eb_, ch_, st_, fl_, nb_: (layer, st_[i], 0, 0))
    tok_spec = lambda imap: pl.BlockSpec((1, 1, MOE_ROWS), imap, memory_space=pltpu.SMEM)
    grid_spec = pltpu.PrefetchScalarGridSpec(
        num_scalar_prefetch=6,
        grid=(n_steps,),
        in_specs=[
            tok_spec(lambda i, *_: (0, 0, 0)), tok_spec(lambda i, *_: (i, 0, 0)), tok_spec(lambda i, *_: (i, 0, 0)),
            pl.BlockSpec(memory_space=pl.ANY),
            const((1, D_MODEL)), const((D_MODEL, ROUTER_COLS)), const((1, ROUTER_COLS)),
            staged((D_MODEL, D_EXPERT)), staged((D_MODEL, D_EXPERT)), staged((D_EXPERT, D_MODEL)),
        ],
        out_specs=pl.BlockSpec(memory_space=pl.ANY),
        scratch_shapes=[
            pltpu.VMEM((MOE_ROWS * SUBLANES, LANES), jnp.float32),
            pltpu.VMEM((MOE_ROWS * SUBLANES, LANES), jnp.float32),
            pltpu.VMEM((MOE_ROWS * SUBLANES, LANES), jnp.float32),
            pltpu.VMEM((MOE_ROWS * SUBLANES, LANES), jnp.float32),
            pltpu.VMEM((EXPERTS_PER_GROUP, D_MODEL, D_EXPERT), jnp.bfloat16),
            pltpu.VMEM((EXPERTS_PER_GROUP, D_MODEL, D_EXPERT), jnp.bfloat16),
            pltpu.VMEM((EXPERTS_PER_GROUP, D_EXPERT, D_MODEL), jnp.bfloat16),
            pltpu.VMEM((D_MODEL, D_EXPERT), jnp.bfloat16),
            pltpu.VMEM((D_MODEL, D_EXPERT), jnp.bfloat16),
            pltpu.VMEM((D_EXPERT, D_MODEL), jnp.bfloat16),
            pltpu.VMEM((D_MODEL, D_EXPERT), jnp.bfloat16),
            pltpu.VMEM((D_MODEL, D_EXPERT), jnp.bfloat16),
            pltpu.VMEM((D_EXPERT, D_MODEL), jnp.bfloat16),
            pltpu.SemaphoreType.DMA((2,)),
            pltpu.SemaphoreType.DMA((2,)),
        ],
    )
    return pl.pallas_call(
        _moe_kernel,
        grid_spec=grid_spec,
        out_shape=jax.ShapeDtypeStruct(h_all.shape, h_all.dtype),
        input_output_aliases={9: 0},
        compiler_params=pltpu.CompilerParams(
            dimension_semantics=("arbitrary",), vmem_limit_bytes=VMEM_LIMIT),
        name="moe_pairs",
    )(ea, eb, chg, stage, flag, nblk, tok0, gtok, stok, h_all, g_ffn, wr, br, w_gate, w_up, w_down)


def _final_norm_kernel(h_ref, g_ref, o_ref):
    o_ref[...] = _rms(_load_tokens(h_ref, MIX_TILE), g_ref[...])


def _final_norm_call(h_all, g, n_rows):
    return pl.pallas_call(
        _final_norm_kernel,
        grid=(n_rows // MIX_TILE,),
        in_specs=[pl.BlockSpec((MIX_TILE * SUBLANES, LANES), lambda i: (i, 0)),
                  pl.BlockSpec((1, D_MODEL), lambda i: (0, 0))],
        out_specs=pl.BlockSpec((MIX_TILE, D_MODEL), lambda i: (i, 0)),
        out_shape=jax.ShapeDtypeStruct((n_rows, D_MODEL), jnp.float32),
        compiler_params=pltpu.CompilerParams(dimension_semantics=("arbitrary",)),
        name="final_norm",
    )(h_all, g)


def _dispatch_tables(cls, n_blocks):
    n_all = cls.shape[0]
    tok_bits = (n_all - 1).bit_length()
    key = jnp.sort((cls << tok_bits) | jnp.arange(n_all, dtype=jnp.int32))
    order = key & ((1 << tok_bits) - 1)
    ids = jnp.arange(N_CLASS_IDS, dtype=jnp.int32)
    counts = jnp.sum((cls[:, None] == ids[None, :]).astype(jnp.int32), axis=0)
    start = jnp.cumsum(counts) - counts
    padded = (counts + MOE_ROWS - 1) // MOE_ROWS * MOE_ROWS
    pad_end = jnp.cumsum(padded)
    pad_start = pad_end - padded
    nblk = (pad_end[-1] // MOE_ROWS).astype(jnp.int32)

    kidx = jnp.arange(n_blocks, dtype=jnp.int32)
    real = kidx < nblk
    p0 = kidx * MOE_ROWS
    cb = jnp.sum((pad_end[None, :] <= p0[:, None]).astype(jnp.int32), axis=1)
    cb = jnp.minimum(cb, N_CLASS_IDS - 1)
    r0 = p0 - pad_start[cb]
    nv = jnp.where(real, jnp.clip(counts[cb] - r0, 0, MOE_ROWS), 0).astype(jnp.int32)
    grp = cb >> 6
    blk_a = grp * EXPERTS_PER_GROUP + ((cb >> 3) & 7)
    blk_b = grp * EXPERTS_PER_GROUP + (cb & 7)
    r = jnp.arange(MOE_ROWS, dtype=jnp.int32)[None, :]
    blk_src0 = start[cb] + r0

    never = jnp.int32(n_blocks)
    experts = jnp.arange(N_EXPERTS, dtype=jnp.int32)[:, None]
    used = real[None, :] & ((blk_a[None, :] == experts) | (blk_b[None, :] == experts))
    first_use = jnp.min(jnp.where(used, kidx[None, :], never), axis=1)
    new_a = real & (first_use[blk_a] == kidx)
    new_b = real & (first_use[blk_b] == kidx)
    filler = new_a & new_b
    blk_step = kidx + jnp.cumsum(filler.astype(jnp.int32))
    n_act = nblk + jnp.sum(filler.astype(jnp.int32))

    n_steps = n_blocks + N_EXPERTS // 2 + 1
    step = jnp.arange(n_steps, dtype=jnp.int32)
    k_done = jnp.sum((real[None, :] & (blk_step[None, :] <= step[:, None])).astype(jnp.int32), axis=1) - 1
    kc = jnp.clip(k_done, 0, n_blocks - 1)
    kn = jnp.clip(k_done + 1, 0, n_blocks - 1)
    active = step < n_act
    is_block = active & (k_done >= 0) & (blk_step[kc] == step)
    is_filler = active & ~is_block

    spare = n_all + (step[:, None] & (PAD_BANKS - 1)) * MOE_ROWS + r
    src = jnp.clip(blk_src0[kc][:, None] + r, 0, n_all - 1)
    rows = jnp.where(is_block[:, None] & (r < nv[kc][:, None]), order[src], spare).astype(jnp.int32)
    last = jnp.clip(nblk - 1, 0, n_blocks - 1)
    ea = jnp.where(is_block, blk_a[kc], jnp.where(is_filler, blk_a[kn], blk_a[last]))
    eb = jnp.where(is_block, blk_b[kc], jnp.where(is_filler, blk_a[kn], blk_b[last]))
    staged_here = jnp.where(
        is_block, jnp.where(new_b[kc], blk_b[kc], jnp.where(new_a[kc], blk_a[kc], -1)),
        jnp.where(is_filler, blk_a[kn], -1))
    flag = (staged_here >= 0).astype(jnp.int32)
    far = jnp.int32(n_steps)
    next_evt = lax.cummin(jnp.where(flag > 0, step, far), axis=0, reverse=True)
    last_evt = jnp.max(jnp.where(flag > 0, step, 0))
    stage = staged_here[jnp.where(next_evt < far, next_evt, last_evt)]

    sink = n_all + SINK_BANK * MOE_ROWS + r
    source = n_all + SOURCE_BANK * MOE_ROWS + r
    nxt = jnp.concatenate([rows[1:], rows[-1:]], axis=0)
    prv = jnp.concatenate([rows[:1], rows[:-1]], axis=0)
    gtok = jnp.where((step + 1 < n_act)[:, None], nxt, source)
    stok = jnp.where(((step >= 1) & (step <= n_act))[:, None], prv, sink)
    shape3 = (n_steps, 1, MOE_ROWS)
    rows, gtok, stok = rows * SUBLANES, gtok * SUBLANES, stok * SUBLANES
    chg = jnp.stack([jnp.concatenate([jnp.ones((1,), jnp.int32), (e[1:] != e[:-1]).astype(jnp.int32)])
                     for e in (ea, eb)])
    return (rows[:1].reshape(1, 1, MOE_ROWS), gtok.reshape(shape3), stok.reshape(shape3), ea, eb, chg,
            stage, flag, n_act.reshape(1).astype(jnp.int32))


def _over_sublanes(w):
    return jnp.broadcast_to(w[:, None, :], (w.shape[0], SUBLANES, w.shape[1]))


def kernel(x, meta_tokens, norm_mix_g, w_in, conf_dw_w, conf_dw_b, conf_ln_g, conf_ln_b, sc_conv_w, w_out,
           norm_ffn_g, w_router_group, b_router_group, w_router_expert, b_router_expert, w_exp_gate,
           w_exp_up, w_exp_down, final_norm_g):
    bsz, seq, d = x.shape
    depth = w_in.shape[0]
    n_main = bsz * seq
    n_all = n_main + N_META
    n_blocks = n_all // MOE_ROWS + N_CLASSES
    f32 = jnp.float32

    zero_halo = (jnp.zeros((HALO_A, CONF_W), f32), jnp.zeros((HALO_C, SC_W), f32))
    assert d == SUBLANES * LANES
    h_all = None

    for l in range(depth):
        w_r = jnp.concatenate([w_router_group[l], w_router_expert[l]], axis=1).astype(f32)
        w_r = jnp.pad(w_r, ((0, 0), (0, ROUTER_COLS - w_r.shape[1])))
        wr_hi = w_r.astype(jnp.bfloat16)
        wr_lo = (w_r - wr_hi.astype(f32)).astype(jnp.bfloat16)
        b_r = jnp.concatenate([b_router_group[l], b_router_expert[l]]).astype(f32)
        b_r = jnp.pad(b_r, (0, ROUTER_COLS - b_r.shape[0])).reshape(1, ROUTER_COLS)
        g_ffn = norm_ffn_g[l].reshape(1, d)
        wts = (norm_mix_g[l].reshape(1, d), w_in[l].astype(jnp.bfloat16), _over_sublanes(conf_dw_w[l]),
               conf_dw_b[l].reshape(1, CONF_W), conf_ln_g[l].reshape(1, CONF_W),
               conf_ln_b[l].reshape(1, CONF_W), _over_sublanes(sc_conv_w[l]), w_out[l].astype(jnp.bfloat16),
               g_ffn, wr_hi, wr_lo, b_r)

        main = dict(row0=0, n_batch=bsz, n_tiles=seq // MIX_TILE, tt=MIX_TILE)
        if h_all is None:
            h_meta, cls_meta, halo_a, halo_c = _mixer_call(
                meta_tokens.astype(x.dtype), *zero_halo, wts, row0=0, n_batch=1, n_tiles=1, tt=N_META,
                fresh="rows")
            tail = jnp.concatenate([h_meta, jnp.zeros((SPARE_ROWS, d), x.dtype)], axis=0)
            h_all, cls_main, _, _ = _mixer_call(
                x.reshape(n_main, d), halo_a[0], halo_c[0], wts, fresh="tokens", tail=tail, **main)
        else:
            h_all, cls_meta, halo_a, halo_c = _mixer_call(
                h_all, *zero_halo, wts, row0=n_main, n_batch=1, n_tiles=1, tt=N_META)
            h_all, cls_main, _, _ = _mixer_call(h_all, halo_a[0], halo_c[0], wts, **main)

        cls = jnp.concatenate([cls_main[:, 0], cls_meta[:, 0]])
        h_all = _moe_call(h_all, _dispatch_tables(cls, n_blocks), g_ffn, wr_hi, b_r,
                          w_exp_gate, w_exp_up, w_exp_down, layer=l)

    out = _final_norm_call(h_all, final_norm_g.reshape(1, d), n_main)
    return out.reshape(bsz, seq, d)
```

```python
import functools

import jax
import jax.numpy as jnp
from jax import lax
from jax.experimental import pallas as pl
from jax.experimental.pallas import tpu as pltpu

D_MODEL = 1024
N_META = 16
CONF_W = 512
SC_W = 512
PROJ_W = 2 * CONF_W + 3 * SC_W
CONF_KERNEL = 31
SC_KERNEL = 3
N_GROUPS = 4
EXPERTS_PER_GROUP = 8
N_EXPERTS = N_GROUPS * EXPERTS_PER_GROUP
D_EXPERT = 512
EPS = 1e-6

LANES = 128
SUBLANES = 8
HALO_A = 32
HALO_C = 8
CONV_ROWS = 32
MIX_TILE = 512
MOE_ROWS = 256
N_CLASS_IDS = N_GROUPS * 64
N_CLASSES = N_GROUPS * (EXPERTS_PER_GROUP * (EXPERTS_PER_GROUP - 1) // 2)
VMEM_LIMIT = 56 * 1024 * 1024
ROUTER_COLS = LANES
MOE_BUFFERS = 3
PAD_BANKS = 8
SINK_BANK = PAD_BANKS
SOURCE_BANK = PAD_BANKS + 1
TAIL_TILES = 6
SPARE_ROWS = TAIL_TILES * MIX_TILE - N_META
assert SPARE_ROWS >= (PAD_BANKS + 2) * MOE_ROWS


def _load_tokens(ref, n):
    return jnp.concatenate([ref[pl.ds(s, n, stride=SUBLANES), :] for s in range(SUBLANES)], axis=-1)


def _store_tokens(ref, x, n):
    for s in range(SUBLANES):
        ref[pl.ds(s, n, stride=SUBLANES), :] = x[:, s * LANES:(s + 1) * LANES]


def _rms(x, g):
    return x * lax.rsqrt(jnp.mean(x * x, axis=-1, keepdims=True) + EPS) * g


def _router_logits(hn, w_hi_ref, w_lo_ref, b_ref):
    hi = hn.astype(jnp.bfloat16)
    lo = (hn - hi.astype(jnp.float32)).astype(jnp.bfloat16)
    acc = jnp.dot(hi, w_hi_ref[...], preferred_element_type=jnp.float32)
    acc = acc + jnp.dot(lo, w_hi_ref[...], preferred_element_type=jnp.float32)
    acc = acc + jnp.dot(hi, w_lo_ref[...], preferred_element_type=jnp.float32)
    return acc + b_ref[...]


N_MIXER_INPUTS = 15


def _mixer_kernel(h_ref, *refs, tt, src_tm, dst_tm, n_batch, has_tail):
    if not has_tail:
        return _mixer_tile(h_ref, *refs, tt=tt, src_tm=src_tm, dst_tm=dst_tm)
    tail_ref, refs = refs[0], refs[1:]
    b = pl.program_id(0)
    pl.when(b < n_batch)(lambda: _mixer_tile(h_ref, *refs, tt=tt, src_tm=src_tm, dst_tm=dst_tm))

    @pl.when(b == n_batch)
    def _():
        hout_ref = refs[N_MIXER_INPUTS - 1]
        _store_tokens(hout_ref, tail_ref[...], tt)


def _mixer_tile(h_ref, halo_a_ref, halo_c_ref, g_mix_ref, w_in_ref, dw_w_ref, dw_b_ref, ln_g_ref,
                ln_b_ref, sc_w_ref, w_out_ref, g_ffn_ref, wr_hi_ref, wr_lo_ref, br_ref,
                hout_ref, cls_ref, halo_a_out, halo_c_out,
                aext, ashift, cext, sb_buf, ybuf, *, tt, src_tm, dst_tm):
    j = pl.program_id(1)

    @pl.when(j == 0)
    def _():
        aext[0:HALO_A, :] = halo_a_ref[...]
        cext[0:HALO_C, :] = halo_c_ref[...]

    h = _load_tokens(h_ref, tt) if src_tm else h_ref[...]
    xn = _rms(h, g_mix_ref[...]).astype(jnp.bfloat16)
    ug = jnp.dot(xn, w_in_ref[:, 0:2 * CONF_W], preferred_element_type=jnp.float32)
    aext[HALO_A:HALO_A + tt, :] = ug[:, 0:CONF_W] * jax.nn.sigmoid(ug[:, CONF_W:2 * CONF_W])
    bcv = jnp.dot(xn, w_in_ref[:, 2 * CONF_W:PROJ_W], preferred_element_type=jnp.float32)
    sb_buf[...] = bcv[:, 0:SC_W]
    cext[HALO_C:HALO_C + tt, :] = bcv[:, SC_W:2 * SC_W] * bcv[:, 2 * SC_W:3 * SC_W]

    n_shift_rows = HALO_A + tt - SUBLANES
    for p in range(1, SUBLANES):
        ashift[p - 1, 0:n_shift_rows, :] = aext[p:p + n_shift_rows, :]

    ch = min(tt, CONV_ROWS)
    tiles = (ch // SUBLANES, SUBLANES, CONF_W)
    for c0 in range(0, tt, ch):
        acc = jnp.broadcast_to(dw_b_ref[...], tiles)
        for k in range(CONF_KERNEL):
            row = HALO_A - (CONF_KERNEL - 1) + c0 + k
            phase = row % SUBLANES
            if phase == 0:
                tap = aext[row:row + ch, :]
            else:
                tap = ashift[phase - 1, row - phase:row - phase + ch, :]
            acc = acc + tap.reshape(tiles) * dw_w_ref[k]
        acc = acc.reshape(ch, CONF_W)
        mu = jnp.mean(acc, axis=-1, keepdims=True)
        xc = acc - mu
        ln = xc * lax.rsqrt(jnp.mean(xc * xc, axis=-1, keepdims=True) + EPS) * ln_g_ref[...] + ln_b_ref[...]
        ybuf[c0:c0 + ch, 0:CONF_W] = jax.nn.silu(ln).astype(jnp.bfloat16)
        sacc = None
        for k in range(SC_KERNEL):
            row = HALO_C - (SC_KERNEL - 1) + c0 + k
            term = cext[row:row + ch, :].reshape(tiles) * sc_w_ref[k]
            sacc = term if sacc is None else sacc + term
        ybuf[c0:c0 + ch, CONF_W:CONF_W + SC_W] = (
            sb_buf[c0:c0 + ch, :] * sacc.reshape(ch, SC_W)).astype(jnp.bfloat16)

    new_halo_a = aext[tt:tt + HALO_A, :]
    new_halo_c = cext[tt:tt + HALO_C, :]
    aext[0:HALO_A, :] = new_halo_a
    cext[0:HALO_C, :] = new_halo_c
    halo_a_out[0] = new_halo_a
    halo_c_out[0] = new_halo_c

    h_new = h + jnp.dot(ybuf[...], w_out_ref[...], preferred_element_type=jnp.float32)
    if dst_tm:
        _store_tokens(hout_ref, h_new, tt)
    else:
        hout_ref[...] = h_new

    logits = _router_logits(_rms(h_new, g_ffn_ref[...]), wr_hi_ref, wr_lo_ref, br_ref)
    lane = lax.broadcasted_iota(jnp.int32, logits.shape, 1)
    neg = jnp.float32(-jnp.inf)
    big = jnp.int32(1 << 20)
    glog = jnp.where(lane < N_GROUPS, logits, neg)
    gmax = jnp.max(glog, axis=-1, keepdims=True)
    g_idx = jnp.min(jnp.where(glog == gmax, lane, big), axis=-1, keepdims=True)
    el = lane - N_GROUPS
    in_group = (el >= 0) & (el < N_EXPERTS) & ((el >> 3) == g_idx)
    l1 = jnp.where(in_group, logits, neg)
    m1 = jnp.max(l1, axis=-1, keepdims=True)
    i1 = jnp.min(jnp.where(l1 == m1, el, big), axis=-1, keepdims=True)
    l2 = jnp.where(el == i1, neg, l1)
    m2 = jnp.max(l2, axis=-1, keepdims=True)
    i2 = jnp.min(jnp.where(l2 == m2, el, big), axis=-1, keepdims=True)
    lo_e = jnp.minimum(i1, i2) & (EXPERTS_PER_GROUP - 1)
    hi_e = jnp.maximum(i1, i2) & (EXPERTS_PER_GROUP - 1)
    cls_ref[...] = g_idx * 64 + lo_e * 8 + hi_e


def _mixer_call(h_src, halo_a, halo_c, wts, *, row0, n_batch, n_tiles, tt, fresh=None, tail=None):
    blk0 = row0 // tt
    n_rows = n_batch * n_tiles * tt
    last = n_batch * n_tiles - 1
    tail_tiles = 0 if tail is None else tail.shape[0] // tt
    const = lambda shape: pl.BlockSpec(shape, lambda b, j: (0,) * len(shape))
    tile = lambda b, j: jnp.minimum(b * n_tiles + j, last)
    tm_spec = pl.BlockSpec((tt * SUBLANES, LANES), lambda b, j: (blk0 + b * n_tiles + j, 0))
    row_spec = pl.BlockSpec((tt, D_MODEL), lambda b, j: (blk0 + tile(b, j), 0))
    if fresh is None:
        in_spec, out_spec, out_struct = tm_spec, tm_spec, jax.ShapeDtypeStruct(h_src.shape, jnp.float32)
    elif fresh == "rows":
        in_spec, out_spec, out_struct = row_spec, row_spec, jax.ShapeDtypeStruct(h_src.shape, jnp.float32)
    else:
        in_spec = row_spec
        tail_at = lambda b, j: jnp.where(b == n_batch, jnp.minimum(j, tail_tiles - 1), j)
        out_spec = pl.BlockSpec((tt * SUBLANES, LANES), lambda b, j: (b * n_tiles + tail_at(b, j), 0))
        out_struct = jax.ShapeDtypeStruct(((n_rows + tail.shape[0]) * SUBLANES, LANES), jnp.float32)
    tail_specs, tail_args = [], []
    if tail is not None:
        tail_specs = [pl.BlockSpec(
            (tt, D_MODEL), lambda b, j: (jnp.where(b == n_batch, jnp.minimum(j, tail_tiles - 1), 0), 0))]
        tail_args = [tail]
    return pl.pallas_call(
        functools.partial(_mixer_kernel, tt=tt, src_tm=fresh is None, dst_tm=fresh != "rows",
                          n_batch=n_batch, has_tail=tail is not None),
        grid=(n_batch + (tail is not None), n_tiles),
        in_specs=[
            in_spec, *tail_specs,
            const((HALO_A, CONF_W)), const((HALO_C, SC_W)),
            const((1, D_MODEL)), const((D_MODEL, PROJ_W)),
            const((CONF_KERNEL, SUBLANES, CONF_W)), const((1, CONF_W)), const((1, CONF_W)), const((1, CONF_W)),
            const((SC_KERNEL, SUBLANES, SC_W)), const((CONF_W + SC_W, D_MODEL)), const((1, D_MODEL)),
            const((D_MODEL, ROUTER_COLS)), const((D_MODEL, ROUTER_COLS)), const((1, ROUTER_COLS)),
        ],
        out_specs=[
            out_spec,
            pl.BlockSpec((tt, 1), lambda b, j: (tile(b, j), 0)),
            pl.BlockSpec((1, HALO_A, CONF_W), lambda b, j: (jnp.minimum(b, n_batch - 1), 0, 0)),
            pl.BlockSpec((1, HALO_C, SC_W), lambda b, j: (jnp.minimum(b, n_batch - 1), 0, 0)),
        ],
        out_shape=[
            out_struct,
            jax.ShapeDtypeStruct((n_rows, 1), jnp.int32),
            jax.ShapeDtypeStruct((n_batch, HALO_A, CONF_W), jnp.float32),
            jax.ShapeDtypeStruct((n_batch, HALO_C, SC_W), jnp.float32),
        ],
        scratch_shapes=[
            pltpu.VMEM((HALO_A + tt, CONF_W), jnp.float32),
            pltpu.VMEM((SUBLANES - 1, HALO_A + tt, CONF_W), jnp.float32),
            pltpu.VMEM((HALO_C + tt, SC_W), jnp.float32),
            pltpu.VMEM((tt, SC_W), jnp.float32),
            pltpu.VMEM((tt, CONF_W + SC_W), jnp.bfloat16),
        ],
        input_output_aliases={0: 0} if fresh is None else {},
        compiler_params=pltpu.CompilerParams(
            dimension_semantics=("arbitrary", "arbitrary"), vmem_limit_bytes=VMEM_LIMIT),
        name=f"mixer_t{tt}",
    )(h_src, *tail_args, halo_a, halo_c, *wts)


def _moe_kernel(ea_ref, eb_ref, chg_ref, stage_ref, flag_ref, nblk_ref,
                tok0_ref, gtok_ref, stok_ref, h_in, g_ffn_ref, wr_ref, br_ref,
                wg_st, wu_st, wd_st,
                h_out,
                xbuf0, xbuf1, xbuf2, obuf0, obuf1, obuf2, wg_c, wu_c, wd_c, wga, wua, wda, wgb, wub, wdb,
                gsem, ssem):
    i = pl.program_id(0)
    nblk = nblk_ref[0]
    xbuf = (xbuf0, xbuf1, xbuf2)
    obuf = (obuf0, obuf1, obuf2)
    cur_a = (wga, wua, wda)
    cur_b = (wgb, wub, wdb)

    def token_tile(tok_ref, r, blk=0):
        return pl.ds(pl.multiple_of(tok_ref[blk, 0, r], SUBLANES), SUBLANES)

    def gather_row(tok_ref, r, dst_slot, blk=0):
        return pltpu.make_async_copy(
            h_in.at[token_tile(tok_ref, r, blk), :], xbuf[dst_slot].at[pl.ds(r * SUBLANES, SUBLANES), :],
            gsem.at[dst_slot])

    def scatter_row(r, src_slot):
        return pltpu.make_async_copy(
            obuf[src_slot].at[pl.ds(r * SUBLANES, SUBLANES), :], h_out.at[token_tile(stok_ref, r), :],
            ssem.at[src_slot])

    def wait_gather(dst_slot):
        pltpu.make_async_copy(
            h_in.at[pl.ds(0, MOE_ROWS * SUBLANES), :], xbuf[dst_slot], gsem.at[dst_slot]).wait()

    def wait_scatter(src_slot):
        pltpu.make_async_copy(
            obuf[src_slot], h_out.at[pl.ds(0, MOE_ROWS * SUBLANES), :], ssem.at[src_slot]).wait()

    @pl.when(i == 0)
    def _():
        obuf2[...] = jnp.zeros_like(obuf2)

        def start(r, c):
            gather_row(tok0_ref, r, 0, blk=0).start()
            gather_row(tok0_ref, r, 1, blk=1).start()
            return c

        lax.fori_loop(0, MOE_ROWS, start, 0)

    def step(slot):
        ahead = (slot + 2) % MOE_BUFFERS
        behind = (slot - 1) % MOE_BUFFERS
        ea = ea_ref[i]
        eb = eb_ref[i]
        wait_gather(slot)

        @pl.when(i >= 2)
        def _():
            wait_scatter(slot)

        @pl.when(flag_ref[i] != 0)
        def _():
            dst = stage_ref[i] & (EXPERTS_PER_GROUP - 1)
            wg_c[dst] = wg_st[0, 0].astype(jnp.bfloat16)
            wu_c[dst] = wu_st[0, 0].astype(jnp.bfloat16)
            wd_c[dst] = wd_st[0, 0].astype(jnp.bfloat16)

        for e, changed, cur in ((ea, chg_ref[0, i], cur_a), (eb, chg_ref[1, i], cur_b)):
            @pl.when(changed != 0)
            def _(e=e, cur=cur):
                at = e & (EXPERTS_PER_GROUP - 1)
                cur[0][...] = wg_c[at]
                cur[1][...] = wu_c[at]
                cur[2][...] = wd_c[at]

        for r in range(MOE_ROWS):
            gather_row(gtok_ref, r, ahead).start()
        for r in range(MOE_ROWS):
            scatter_row(r, behind).start()

        x = _load_tokens(xbuf[slot], MOE_ROWS)
        xb = _rms(x, g_ffn_ref[...]).astype(jnp.bfloat16)
        logits = jnp.dot(xb, wr_ref[...], preferred_element_type=jnp.float32) + br_ref[...]
        lane = lax.broadcasted_iota(jnp.int32, logits.shape, 1)
        neg = jnp.float32(-jnp.inf)
        glog = jnp.where(lane < N_GROUPS, logits, neg)
        gmax = jnp.max(glog, axis=-1, keepdims=True)
        gsum = jnp.sum(jnp.exp(glog - gmax), axis=-1, keepdims=True)
        grp = ea >> 3
        l_g = jnp.sum(jnp.where(lane == grp, logits, 0.0), axis=-1, keepdims=True)
        p_g = jnp.exp(l_g - gmax) / gsum
        l_a = jnp.sum(jnp.where(lane == N_GROUPS + ea, logits, 0.0), axis=-1, keepdims=True)
        l_b = jnp.sum(jnp.where(lane == N_GROUPS + eb, logits, 0.0), axis=-1, keepdims=True)
        w_a = p_g / (1.0 + jnp.exp(l_b - l_a))
        w_b = p_g / (1.0 + jnp.exp(l_a - l_b))

        def expert(cur):
            gate = jnp.dot(xb, cur[0][...], preferred_element_type=jnp.float32)
            up = jnp.dot(xb, cur[1][...], preferred_element_type=jnp.float32)
            hid = (jax.nn.silu(gate) * up).astype(jnp.bfloat16)
            return jnp.dot(hid, cur[2][...], preferred_element_type=jnp.float32)

        y = w_a * expert(cur_a) + w_b * expert(cur_b)
        _store_tokens(obuf[slot], x + y, MOE_ROWS)

        @pl.when(i == nblk)
        def _():
            wait_gather((slot + 1) % MOE_BUFFERS)
            wait_gather(ahead)
            wait_scatter((slot - 2) % MOE_BUFFERS)
            wait_scatter(behind)

    rotation = lax.rem(i, MOE_BUFFERS)
    for slot in range(MOE_BUFFERS):
        pl.when((i <= nblk) & (rotation == slot))(functools.partial(step, slot))


def _moe_call(h_all, tables, g_ffn, wr, br, w_gate, w_up, w_down, *, layer):
    tok0, gtok, stok, ea, eb, chg, stage, flag, nblk = tables
    n_steps = gtok.shape[0]
    const = lambda shape: pl.BlockSpec(shape, lambda i, *_: (0,) * len(shape))
    staged = lambda shape: pl.BlockSpec(
        (1, 1) + shape, lambda i, ea_, eb_, ch_, st_, fl_, nb_: (layer, st_[i], 0, 0))
    tok_spec = lambda imap: pl.BlockSpec((1, 1, MOE_ROWS), imap, memory_space=pltpu.SMEM)
    grid_spec = pltpu.PrefetchScalarGridSpec(
        num_scalar_prefetch=6,
        grid=(n_steps,),
        in_specs=[
            pl.BlockSpec((2, 1, MOE_ROWS), lambda i, *_: (0, 0, 0), memory_space=pltpu.SMEM),
            tok_spec(lambda i, *_: (i, 0, 0)), tok_spec(lambda i, *_: (i, 0, 0)),
            pl.BlockSpec(memory_space=pl.ANY),
            const((1, D_MODEL)), const((D_MODEL, ROUTER_COLS)), const((1, ROUTER_COLS)),
            staged((D_MODEL, D_EXPERT)), staged((D_MODEL, D_EXPERT)), staged((D_EXPERT, D_MODEL)),
        ],
        out_specs=pl.BlockSpec(memory_space=pl.ANY),
        scratch_shapes=[
            *[pltpu.VMEM((MOE_ROWS * SUBLANES, LANES), jnp.float32)] * (2 * MOE_BUFFERS),
            pltpu.VMEM((EXPERTS_PER_GROUP, D_MODEL, D_EXPERT), jnp.bfloat16),
            pltpu.VMEM((EXPERTS_PER_GROUP, D_MODEL, D_EXPERT), jnp.bfloat16),
            pltpu.VMEM((EXPERTS_PER_GROUP, D_EXPERT, D_MODEL), jnp.bfloat16),
            pltpu.VMEM((D_MODEL, D_EXPERT), jnp.bfloat16),
            pltpu.VMEM((D_MODEL, D_EXPERT), jnp.bfloat16),
            pltpu.VMEM((D_EXPERT, D_MODEL), jnp.bfloat16),
            pltpu.VMEM((D_MODEL, D_EXPERT), jnp.bfloat16),
            pltpu.VMEM((D_MODEL, D_EXPERT), jnp.bfloat16),
            pltpu.VMEM((D_EXPERT, D_MODEL), jnp.bfloat16),
            pltpu.SemaphoreType.DMA((MOE_BUFFERS,)),
            pltpu.SemaphoreType.DMA((MOE_BUFFERS,)),
        ],
    )
    return pl.pallas_call(
        _moe_kernel,
        grid_spec=grid_spec,
        out_shape=jax.ShapeDtypeStruct(h_all.shape, h_all.dtype),
        input_output_aliases={9: 0},
        compiler_params=pltpu.CompilerParams(
            dimension_semantics=("arbitrary",), vmem_limit_bytes=VMEM_LIMIT),
        name="moe_pairs",
    )(ea, eb, chg, stage, flag, nblk, tok0, gtok, stok, h_all, g_ffn, wr, br, w_gate, w_up, w_down)


def _final_norm_kernel(h_ref, g_ref, o_ref):
    o_ref[...] = _rms(_load_tokens(h_ref, MIX_TILE), g_ref[...])


def _final_norm_call(h_all, g, n_rows):
    return pl.pallas_call(
        _final_norm_kernel,
        grid=(n_rows // MIX_TILE,),
        in_specs=[pl.BlockSpec((MIX_TILE * SUBLANES, LANES), lambda i: (i, 0)),
                  pl.BlockSpec((1, D_MODEL), lambda i: (0, 0))],
        out_specs=pl.BlockSpec((MIX_TILE, D_MODEL), lambda i: (i, 0)),
        out_shape=jax.ShapeDtypeStruct((n_rows, D_MODEL), jnp.float32),
        compiler_params=pltpu.CompilerParams(dimension_semantics=("arbitrary",)),
        name="final_norm",
    )(h_all, g)


def _dispatch_tables(cls, n_blocks):
    n_all = cls.shape[0]
    tok_bits = (n_all - 1).bit_length()
    key = jnp.sort((cls << tok_bits) | jnp.arange(n_all, dtype=jnp.int32))
    order = key & ((1 << tok_bits) - 1)
    ids = jnp.arange(N_CLASS_IDS, dtype=jnp.int32)
    counts = jnp.sum((cls[:, None] == ids[None, :]).astype(jnp.int32), axis=0)
    start = jnp.cumsum(counts) - counts
    padded = (counts + MOE_ROWS - 1) // MOE_ROWS * MOE_ROWS
    pad_end = jnp.cumsum(padded)
    pad_start = pad_end - padded
    nblk = (pad_end[-1] // MOE_ROWS).astype(jnp.int32)

    kidx = jnp.arange(n_blocks, dtype=jnp.int32)
    real = kidx < nblk
    p0 = kidx * MOE_ROWS
    cb = jnp.sum((pad_end[None, :] <= p0[:, None]).astype(jnp.int32), axis=1)
    cb = jnp.minimum(cb, N_CLASS_IDS - 1)
    r0 = p0 - pad_start[cb]
    nv = jnp.where(real, jnp.clip(counts[cb] - r0, 0, MOE_ROWS), 0).astype(jnp.int32)
    grp = cb >> 6
    blk_a = grp * EXPERTS_PER_GROUP + ((cb >> 3) & 7)
    blk_b = grp * EXPERTS_PER_GROUP + (cb & 7)
    r = jnp.arange(MOE_ROWS, dtype=jnp.int32)[None, :]
    blk_src0 = start[cb] + r0

    never = jnp.int32(n_blocks)
    experts = jnp.arange(N_EXPERTS, dtype=jnp.int32)[:, None]
    used = real[None, :] & ((blk_a[None, :] == experts) | (blk_b[None, :] == experts))
    first_use = jnp.min(jnp.where(used, kidx[None, :], never), axis=1)
    new_a = real & (first_use[blk_a] == kidx)
    new_b = real & (first_use[blk_b] == kidx)
    filler = new_a & new_b
    blk_step = kidx + jnp.cumsum(filler.astype(jnp.int32))
    n_act = nblk + jnp.sum(filler.astype(jnp.int32))

    n_steps = n_blocks + N_EXPERTS // 2 + 1
    step = jnp.arange(n_steps, dtype=jnp.int32)
    k_done = jnp.sum((real[None, :] & (blk_step[None, :] <= step[:, None])).astype(jnp.int32), axis=1) - 1
    kc = jnp.clip(k_done, 0, n_blocks - 1)
    kn = jnp.clip(k_done + 1, 0, n_blocks - 1)
    active = step < n_act
    is_block = active & (k_done >= 0) & (blk_step[kc] == step)
    is_filler = active & ~is_block

    spare = n_all + (step[:, None] & (PAD_BANKS - 1)) * MOE_ROWS + r
    src = jnp.clip(blk_src0[kc][:, None] + r, 0, n_all - 1)
    rows = jnp.where(is_block[:, None] & (r < nv[kc][:, None]), order[src], spare).astype(jnp.int32)
    last = jnp.clip(nblk - 1, 0, n_blocks - 1)
    ea = jnp.where(is_block, blk_a[kc], jnp.where(is_filler, blk_a[kn], blk_a[last]))
    eb = jnp.where(is_block, blk_b[kc], jnp.where(is_filler, blk_a[kn], blk_b[last]))
    staged_here = jnp.where(
        is_block, jnp.where(new_b[kc], blk_b[kc], jnp.where(new_a[kc], blk_a[kc], -1)),
        jnp.where(is_filler, blk_a[kn], -1))
    flag = (staged_here >= 0).astype(jnp.int32)
    far = jnp.int32(n_steps)
    next_evt = lax.cummin(jnp.where(flag > 0, step, far), axis=0, reverse=True)
    last_evt = jnp.max(jnp.where(flag > 0, step, 0))
    stage = staged_here[jnp.where(next_evt < far, next_evt, last_evt)]

    sink = n_all + SINK_BANK * MOE_ROWS + r
    source = n_all + SOURCE_BANK * MOE_ROWS + r
    nxt = jnp.concatenate([rows[2:], rows[-2:]], axis=0)
    prv = jnp.concatenate([rows[:1], rows[:-1]], axis=0)
    gtok = jnp.where((step + 2 < n_act)[:, None], nxt, source)
    stok = jnp.where(((step >= 1) & (step <= n_act))[:, None], prv, sink)
    shape3 = (n_steps, 1, MOE_ROWS)
    rows, gtok, stok = rows * SUBLANES, gtok * SUBLANES, stok * SUBLANES
    chg = jnp.stack([jnp.concatenate([jnp.ones((1,), jnp.int32), (e[1:] != e[:-1]).astype(jnp.int32)])
                     for e in (ea, eb)])
    return (rows[:2].reshape(2, 1, MOE_ROWS), gtok.reshape(shape3), stok.reshape(shape3), ea, eb, chg,
            stage, flag, n_act.reshape(1).astype(jnp.int32))


def _over_sublanes(w):
    return jnp.broadcast_to(w[:, None, :], (w.shape[0], SUBLANES, w.shape[1]))


def kernel(x, meta_tokens, norm_mix_g, w_in, conf_dw_w, conf_dw_b, conf_ln_g, conf_ln_b, sc_conv_w, w_out,
           norm_ffn_g, w_router_group, b_router_group, w_router_expert, b_router_expert, w_exp_gate,
           w_exp_up, w_exp_down, final_norm_g):
    bsz, seq, d = x.shape
    depth = w_in.shape[0]
    n_main = bsz * seq
    n_all = n_main + N_META
    n_blocks = n_all // MOE_ROWS + N_CLASSES
    f32 = jnp.float32

    zero_halo = (jnp.zeros((HALO_A, CONF_W), f32), jnp.zeros((HALO_C, SC_W), f32))
    assert d == SUBLANES * LANES
    h_all = None

    for l in range(depth):
        w_r = jnp.concatenate([w_router_group[l], w_router_expert[l]], axis=1).astype(f32)
        w_r = jnp.pad(w_r, ((0, 0), (0, ROUTER_COLS - w_r.shape[1])))
        wr_hi = w_r.astype(jnp.bfloat16)
        wr_lo = (w_r - wr_hi.astype(f32)).astype(jnp.bfloat16)
        b_r = jnp.concatenate([b_router_group[l], b_router_expert[l]]).astype(f32)
        b_r = jnp.pad(b_r, (0, ROUTER_COLS - b_r.shape[0])).reshape(1, ROUTER_COLS)
        g_ffn = norm_ffn_g[l].reshape(1, d)
        wts = (norm_mix_g[l].reshape(1, d), w_in[l].astype(jnp.bfloat16), _over_sublanes(conf_dw_w[l]),
               conf_dw_b[l].reshape(1, CONF_W), conf_ln_g[l].reshape(1, CONF_W),
               conf_ln_b[l].reshape(1, CONF_W), _over_sublanes(sc_conv_w[l]), w_out[l].astype(jnp.bfloat16),
               g_ffn, wr_hi, wr_lo, b_r)

        main = dict(row0=0, n_batch=bsz, n_tiles=seq // MIX_TILE, tt=MIX_TILE)
        if h_all is None:
            h_meta, cls_meta, halo_a, halo_c = _mixer_call(
                meta_tokens.astype(x.dtype), *zero_halo, wts, row0=0, n_batch=1, n_tiles=1, tt=N_META,
                fresh="rows")
            tail = jnp.concatenate([h_meta, jnp.zeros((SPARE_ROWS, d), x.dtype)], axis=0)
            h_all, cls_main, _, _ = _mixer_call(
                x.reshape(n_main, d), halo_a[0], halo_c[0], wts, fresh="tokens", tail=tail, **main)
        else:
            h_all, cls_meta, halo_a, halo_c = _mixer_call(
                h_all, *zero_halo, wts, row0=n_main, n_batch=1, n_tiles=1, tt=N_META)
            h_all, cls_main, _, _ = _mixer_call(h_all, halo_a[0], halo_c[0], wts, **main)

        cls = jnp.concatenate([cls_main[:, 0], cls_meta[:, 0]])
        h_all = _moe_call(h_all, _dispatch_tables(cls, n_blocks), g_ffn, wr_hi, b_r,
                          w_exp_gate, w_exp_up, w_exp_down, layer=l)

    out = _final_norm_call(h_all, final_norm_g.reshape(1, d), n_main)
    return out.reshape(bsz, seq, d)
```

```python
import functools

import jax
import jax.numpy as jnp
from jax import lax
from jax.experimental import pallas as pl
from jax.experimental.pallas import tpu as pltpu

D_MODEL = 1024
N_META = 16
CONF_W = 512
SC_W = 512
PROJ_W = 2 * CONF_W + 3 * SC_W
CONF_KERNEL = 31
SC_KERNEL = 3
N_GROUPS = 4
EXPERTS_PER_GROUP = 8
N_EXPERTS = N_GROUPS * EXPERTS_PER_GROUP
D_EXPERT = 512
EPS = 1e-6

LANES = 128
SUBLANES = 8
HALO_A = 32
HALO_C = 8
CONV_ROWS = 32
MIX_TILE = 512
NORM_TILE = 2048
MOE_ROWS = 256
N_CLASS_IDS = N_GROUPS * 64
N_CLASSES = N_GROUPS * (EXPERTS_PER_GROUP * (EXPERTS_PER_GROUP - 1) // 2)
VMEM_LIMIT = 56 * 1024 * 1024
ROUTER_COLS = LANES
MOE_BUFFERS = 3
PAD_BANKS = 8
SINK_BANK = PAD_BANKS
SOURCE_BANK = PAD_BANKS + 1
TAIL_TILES = 6
SPARE_ROWS = TAIL_TILES * MIX_TILE - N_META
assert SPARE_ROWS >= (PAD_BANKS + 2) * MOE_ROWS


def _load_tokens(ref, n, first=0):
    return jnp.concatenate(
        [ref[pl.ds(first * SUBLANES + s, n, stride=SUBLANES), :] for s in range(SUBLANES)], axis=-1)


def _store_tokens(ref, x, n):
    for s in range(SUBLANES):
        ref[pl.ds(s, n, stride=SUBLANES), :] = x[:, s * LANES:(s + 1) * LANES]


def _rms(x, g):
    return x * lax.rsqrt(jnp.mean(x * x, axis=-1, keepdims=True) + EPS) * g


def _router_logits(hn_bf16, w_ref, b_ref):
    return jnp.dot(hn_bf16, w_ref[...], preferred_element_type=jnp.float32) + b_ref[...]


N_MIXER_INPUTS = 14


def _mixer_kernel(h_ref, *refs, tt, src_tm, dst_tm, n_batch, has_tail):
    if not has_tail:
        return _mixer_tile(h_ref, *refs, tt=tt, src_tm=src_tm, dst_tm=dst_tm)
    tail_ref, refs = refs[0], refs[1:]
    b = pl.program_id(0)
    pl.when(b < n_batch)(lambda: _mixer_tile(h_ref, *refs, tt=tt, src_tm=src_tm, dst_tm=dst_tm))

    @pl.when(b == n_batch)
    def _():
        hout_ref = refs[N_MIXER_INPUTS - 1]
        _store_tokens(hout_ref, tail_ref[...], tt)


def _mixer_tile(h_ref, halo_a_ref, halo_c_ref, g_mix_ref, w_in_ref, dw_w_ref, dw_b_ref, ln_g_ref,
                ln_b_ref, sc_w_ref, w_out_ref, g_ffn_ref, wr_ref, br_ref,
                hout_ref, cls_ref, halo_a_out, halo_c_out,
                aext, ashift, cext, sb_buf, ybuf, *, tt, src_tm, dst_tm):
    j = pl.program_id(1)

    @pl.when(j == 0)
    def _():
        aext[0:HALO_A, :] = halo_a_ref[...]
        cext[0:HALO_C, :] = halo_c_ref[...]

    h = _load_tokens(h_ref, tt) if src_tm else h_ref[...]
    xn = _rms(h, g_mix_ref[...]).astype(jnp.bfloat16)
    ug = jnp.dot(xn, w_in_ref[:, 0:2 * CONF_W], preferred_element_type=jnp.float32)
    aext[HALO_A:HALO_A + tt, :] = ug[:, 0:CONF_W] * jax.nn.sigmoid(ug[:, CONF_W:2 * CONF_W])
    bcv = jnp.dot(xn, w_in_ref[:, 2 * CONF_W:PROJ_W], preferred_element_type=jnp.float32)
    sb_buf[...] = bcv[:, 0:SC_W]
    cext[HALO_C:HALO_C + tt, :] = bcv[:, SC_W:2 * SC_W] * bcv[:, 2 * SC_W:3 * SC_W]

    n_shift_rows = HALO_A + tt - SUBLANES
    for p in range(1, SUBLANES):
        ashift[p - 1, 0:n_shift_rows, :] = aext[p:p + n_shift_rows, :]

    ch = min(tt, CONV_ROWS)
    tiles = (ch // SUBLANES, SUBLANES, CONF_W)
    for c0 in range(0, tt, ch):
        acc = jnp.broadcast_to(dw_b_ref[...], tiles)
        for k in range(CONF_KERNEL):
            row = HALO_A - (CONF_KERNEL - 1) + c0 + k
            phase = row % SUBLANES
            if phase == 0:
                tap = aext[row:row + ch, :]
            else:
                tap = ashift[phase - 1, row - phase:row - phase + ch, :]
            acc = acc + tap.reshape(tiles) * dw_w_ref[k]
        acc = acc.reshape(ch, CONF_W)
        mu = jnp.mean(acc, axis=-1, keepdims=True)
        xc = acc - mu
        ln = xc * lax.rsqrt(jnp.mean(xc * xc, axis=-1, keepdims=True) + EPS) * ln_g_ref[...] + ln_b_ref[...]
        ybuf[c0:c0 + ch, 0:CONF_W] = jax.nn.silu(ln).astype(jnp.bfloat16)
        sacc = None
        for k in range(SC_KERNEL):
            row = HALO_C - (SC_KERNEL - 1) + c0 + k
            term = cext[row:row + ch, :].reshape(tiles) * sc_w_ref[k]
            sacc = term if sacc is None else sacc + term
        ybuf[c0:c0 + ch, CONF_W:CONF_W + SC_W] = (
            sb_buf[c0:c0 + ch, :] * sacc.reshape(ch, SC_W)).astype(jnp.bfloat16)

    new_halo_a = aext[tt:tt + HALO_A, :]
    new_halo_c = cext[tt:tt + HALO_C, :]
    aext[0:HALO_A, :] = new_halo_a
    cext[0:HALO_C, :] = new_halo_c
    halo_a_out[0] = new_halo_a
    halo_c_out[0] = new_halo_c

    h_new = h + jnp.dot(ybuf[...], w_out_ref[...], preferred_element_type=jnp.float32)
    if dst_tm:
        _store_tokens(hout_ref, h_new, tt)
    else:
        hout_ref[...] = h_new

    logits = _router_logits(_rms(h_new, g_ffn_ref[...]).astype(jnp.bfloat16), wr_ref, br_ref)
    lane = lax.broadcasted_iota(jnp.int32, logits.shape, 1)
    neg = jnp.float32(-jnp.inf)
    big = jnp.int32(1 << 20)
    glog = jnp.where(lane < N_GROUPS, logits, neg)
    gmax = jnp.max(glog, axis=-1, keepdims=True)
    g_idx = jnp.min(jnp.where(glog == gmax, lane, big), axis=-1, keepdims=True)
    el = lane - N_GROUPS
    in_group = (el >= 0) & (el < N_EXPERTS) & ((el >> 3) == g_idx)
    l1 = jnp.where(in_group, logits, neg)
    m1 = jnp.max(l1, axis=-1, keepdims=True)
    i1 = jnp.min(jnp.where(l1 == m1, el, big), axis=-1, keepdims=True)
    l2 = jnp.where(el == i1, neg, l1)
    m2 = jnp.max(l2, axis=-1, keepdims=True)
    i2 = jnp.min(jnp.where(l2 == m2, el, big), axis=-1, keepdims=True)
    lo_e = jnp.minimum(i1, i2) & (EXPERTS_PER_GROUP - 1)
    hi_e = jnp.maximum(i1, i2) & (EXPERTS_PER_GROUP - 1)
    cls_ref[...] = g_idx * 64 + lo_e * 8 + hi_e


def _mixer_call(h_src, halo_a, halo_c, wts, *, row0, n_batch, n_tiles, tt, fresh=None, tail=None):
    blk0 = row0 // tt
    n_rows = n_batch * n_tiles * tt
    last = n_batch * n_tiles - 1
    tail_tiles = 0 if tail is None else tail.shape[0] // tt
    const = lambda shape: pl.BlockSpec(shape, lambda b, j: (0,) * len(shape))
    tile = lambda b, j: jnp.minimum(b * n_tiles + j, last)
    tm_spec = pl.BlockSpec((tt * SUBLANES, LANES), lambda b, j: (blk0 + b * n_tiles + j, 0))
    row_spec = pl.BlockSpec((tt, D_MODEL), lambda b, j: (blk0 + tile(b, j), 0))
    if fresh is None:
        in_spec, out_spec, out_struct = tm_spec, tm_spec, jax.ShapeDtypeStruct(h_src.shape, jnp.float32)
    elif fresh == "rows":
        in_spec, out_spec, out_struct = row_spec, row_spec, jax.ShapeDtypeStruct(h_src.shape, jnp.float32)
    else:
        in_spec = row_spec
        tail_at = lambda b, j: jnp.where(b == n_batch, jnp.minimum(j, tail_tiles - 1), j)
        out_spec = pl.BlockSpec((tt * SUBLANES, LANES), lambda b, j: (b * n_tiles + tail_at(b, j), 0))
        out_struct = jax.ShapeDtypeStruct(((n_rows + tail.shape[0]) * SUBLANES, LANES), jnp.float32)
    tail_specs, tail_args = [], []
    if tail is not None:
        tail_specs = [pl.BlockSpec(
            (tt, D_MODEL), lambda b, j: (jnp.where(b == n_batch, jnp.minimum(j, tail_tiles - 1), 0), 0))]
        tail_args = [tail]
    return pl.pallas_call(
        functools.partial(_mixer_kernel, tt=tt, src_tm=fresh is None, dst_tm=fresh != "rows",
                          n_batch=n_batch, has_tail=tail is not None),
        grid=(n_batch + (tail is not None), n_tiles),
        in_specs=[
            in_spec, *tail_specs,
            const((HALO_A, CONF_W)), const((HALO_C, SC_W)),
            const((1, D_MODEL)), const((D_MODEL, PROJ_W)),
            const((CONF_KERNEL, SUBLANES, CONF_W)), const((1, CONF_W)), const((1, CONF_W)), const((1, CONF_W)),
            const((SC_KERNEL, SUBLANES, SC_W)), const((CONF_W + SC_W, D_MODEL)), const((1, D_MODEL)),
            const((D_MODEL, ROUTER_COLS)), const((1, ROUTER_COLS)),
        ],
        out_specs=[
            out_spec,
            pl.BlockSpec((tt, 1), lambda b, j: (tile(b, j), 0)),
            pl.BlockSpec((1, HALO_A, CONF_W), lambda b, j: (jnp.minimum(b, n_batch - 1), 0, 0)),
            pl.BlockSpec((1, HALO_C, SC_W), lambda b, j: (jnp.minimum(b, n_batch - 1), 0, 0)),
        ],
        out_shape=[
            out_struct,
            jax.ShapeDtypeStruct((n_rows, 1), jnp.int32),
            jax.ShapeDtypeStruct((n_batch, HALO_A, CONF_W), jnp.float32),
            jax.ShapeDtypeStruct((n_batch, HALO_C, SC_W), jnp.float32),
        ],
        scratch_shapes=[
            pltpu.VMEM((HALO_A + tt, CONF_W), jnp.float32),
            pltpu.VMEM((SUBLANES - 1, HALO_A + tt, CONF_W), jnp.float32),
            pltpu.VMEM((HALO_C + tt, SC_W), jnp.float32),
            pltpu.VMEM((tt, SC_W), jnp.float32),
            pltpu.VMEM((tt, CONF_W + SC_W), jnp.bfloat16),
        ],
        input_output_aliases={0: 0} if fresh is None else {},
        compiler_params=pltpu.CompilerParams(
            dimension_semantics=("arbitrary", "arbitrary"), vmem_limit_bytes=VMEM_LIMIT),
        name=f"mixer_t{tt}",
    )(h_src, *tail_args, halo_a, halo_c, *wts)


def _moe_kernel(ea_ref, eb_ref, chg_ref, stage_ref, flag_ref, nblk_ref,
                tok0_ref, gtok_ref, stok_ref, h_in, g_ffn_ref, wr_ref, br_ref,
                wg_st, wu_st, wd_st,
                h_out,
                xbuf0, xbuf1, xbuf2, obuf0, obuf1, obuf2, wg_c, wu_c, wd_c, wga, wua, wda, wgb, wub, wdb,
                gsem, ssem):
    i = pl.program_id(0)
    nblk = nblk_ref[0]
    xbuf = (xbuf0, xbuf1, xbuf2)
    obuf = (obuf0, obuf1, obuf2)
    cur_a = (wga, wua, wda)
    cur_b = (wgb, wub, wdb)

    def token_tile(tok_ref, r, blk=0):
        return pl.ds(pl.multiple_of(tok_ref[blk, 0, r], SUBLANES), SUBLANES)

    def gather_row(tok_ref, r, dst_slot, blk=0):
        return pltpu.make_async_copy(
            h_in.at[token_tile(tok_ref, r, blk), :], xbuf[dst_slot].at[pl.ds(r * SUBLANES, SUBLANES), :],
            gsem.at[dst_slot])

    def scatter_row(r, src_slot):
        return pltpu.make_async_copy(
            obuf[src_slot].at[pl.ds(r * SUBLANES, SUBLANES), :], h_out.at[token_tile(stok_ref, r), :],
            ssem.at[src_slot])

    def wait_gather(dst_slot):
        pltpu.make_async_copy(
            h_in.at[pl.ds(0, MOE_ROWS * SUBLANES), :], xbuf[dst_slot], gsem.at[dst_slot]).wait()

    def wait_scatter(src_slot):
        pltpu.make_async_copy(
            obuf[src_slot], h_out.at[pl.ds(0, MOE_ROWS * SUBLANES), :], ssem.at[src_slot]).wait()

    @pl.when(i == 0)
    def _():
        obuf2[...] = jnp.zeros_like(obuf2)

        def start(r, c):
            gather_row(tok0_ref, r, 0, blk=0).start()
            gather_row(tok0_ref, r, 1, blk=1).start()
            return c

        lax.fori_loop(0, MOE_ROWS, start, 0)

    def step(slot):
        ahead = (slot + 2) % MOE_BUFFERS
        behind = (slot - 1) % MOE_BUFFERS
        ea = ea_ref[i]
        eb = eb_ref[i]
        wait_gather(slot)

        @pl.when(i >= 2)
        def _():
            wait_scatter(slot)

        @pl.when(flag_ref[i] != 0)
        def _():
            dst = stage_ref[i] & (EXPERTS_PER_GROUP - 1)
            wg_c[dst] = wg_st[0, 0].astype(jnp.bfloat16)
            wu_c[dst] = wu_st[0, 0].astype(jnp.bfloat16)
            wd_c[dst] = wd_st[0, 0].astype(jnp.bfloat16)

        for e, changed, cur in ((ea, chg_ref[0, i], cur_a), (eb, chg_ref[1, i], cur_b)):
            @pl.when(changed != 0)
            def _(e=e, cur=cur):
                at = e & (EXPERTS_PER_GROUP - 1)
                cur[0][...] = wg_c[at]
                cur[1][...] = wu_c[at]
                cur[2][...] = wd_c[at]

        for r in range(MOE_ROWS):
            gather_row(gtok_ref, r, ahead).start()
        for r in range(MOE_ROWS):
            scatter_row(r, behind).start()

        x = _load_tokens(xbuf[slot], MOE_ROWS)
        xb = _rms(x, g_ffn_ref[...]).astype(jnp.bfloat16)
        logits = _router_logits(xb, wr_ref, br_ref)
        lane = lax.broadcasted_iota(jnp.int32, logits.shape, 1)
        neg = jnp.float32(-jnp.inf)
        glog = jnp.where(lane < N_GROUPS, logits, neg)
        gmax = jnp.max(glog, axis=-1, keepdims=True)
        gsum = jnp.sum(jnp.exp(glog - gmax), axis=-1, keepdims=True)
        grp = ea >> 3
        l_g = jnp.sum(jnp.where(lane == grp, logits, 0.0), axis=-1, keepdims=True)
        p_g = jnp.exp(l_g - gmax) / gsum
        l_a = jnp.sum(jnp.where(lane == N_GROUPS + ea, logits, 0.0), axis=-1, keepdims=True)
        l_b = jnp.sum(jnp.where(lane == N_GROUPS + eb, logits, 0.0), axis=-1, keepdims=True)
        w_a = p_g / (1.0 + jnp.exp(l_b - l_a))
        w_b = p_g / (1.0 + jnp.exp(l_a - l_b))

        def expert(cur):
            gate = jnp.dot(xb, cur[0][...], preferred_element_type=jnp.float32)
            up = jnp.dot(xb, cur[1][...], preferred_element_type=jnp.float32)
            hid = (jax.nn.silu(gate) * up).astype(jnp.bfloat16)
            return jnp.dot(hid, cur[2][...], preferred_element_type=jnp.float32)

        y = w_a * expert(cur_a) + w_b * expert(cur_b)
        _store_tokens(obuf[slot], x + y, MOE_ROWS)

        @pl.when(i == nblk)
        def _():
            wait_gather((slot + 1) % MOE_BUFFERS)
            wait_gather(ahead)
            wait_scatter((slot - 2) % MOE_BUFFERS)
            wait_scatter(behind)

    rotation = lax.rem(i, MOE_BUFFERS)
    for slot in range(MOE_BUFFERS):
        pl.when((i <= nblk) & (rotation == slot))(functools.partial(step, slot))


def _moe_call(h_all, tables, g_ffn, wr, br, w_gate, w_up, w_down, *, layer):
    tok0, gtok, stok, ea, eb, chg, stage, flag, nblk = tables
    n_steps = gtok.shape[0]
    const = lambda shape: pl.BlockSpec(shape, lambda i, *_: (0,) * len(shape))
    staged = lambda shape: pl.BlockSpec(
        (1, 1) + shape, lambda i, ea_, eb_, ch_, st_, fl_, nb_: (layer, st_[i], 0, 0))
    tok_spec = lambda imap: pl.BlockSpec((1, 1, MOE_ROWS), imap, memory_space=pltpu.SMEM)
    grid_spec = pltpu.PrefetchScalarGridSpec(
        num_scalar_prefetch=6,
        grid=(n_steps,),
        in_specs=[
            pl.BlockSpec((2, 1, MOE_ROWS), lambda i, *_: (0, 0, 0), memory_space=pltpu.SMEM),
            tok_spec(lambda i, *_: (i, 0, 0)), tok_spec(lambda i, *_: (i, 0, 0)),
            pl.BlockSpec(memory_space=pl.ANY),
            const((1, D_MODEL)), const((D_MODEL, ROUTER_COLS)), const((1, ROUTER_COLS)),
            staged((D_MODEL, D_EXPERT)), staged((D_MODEL, D_EXPERT)), staged((D_EXPERT, D_MODEL)),
        ],
        out_specs=pl.BlockSpec(memory_space=pl.ANY),
        scratch_shapes=[
            *[pltpu.VMEM((MOE_ROWS * SUBLANES, LANES), jnp.float32)] * (2 * MOE_BUFFERS),
            pltpu.VMEM((EXPERTS_PER_GROUP, D_MODEL, D_EXPERT), jnp.bfloat16),
            pltpu.VMEM((EXPERTS_PER_GROUP, D_MODEL, D_EXPERT), jnp.bfloat16),
            pltpu.VMEM((EXPERTS_PER_GROUP, D_EXPERT, D_MODEL), jnp.bfloat16),
            pltpu.VMEM((D_MODEL, D_EXPERT), jnp.bfloat16),
            pltpu.VMEM((D_MODEL, D_EXPERT), jnp.bfloat16),
            pltpu.VMEM((D_EXPERT, D_MODEL), jnp.bfloat16),
            pltpu.VMEM((D_MODEL, D_EXPERT), jnp.bfloat16),
            pltpu.VMEM((D_MODEL, D_EXPERT), jnp.bfloat16),
            pltpu.VMEM((D_EXPERT, D_MODEL), jnp.bfloat16),
            pltpu.SemaphoreType.DMA((MOE_BUFFERS,)),
            pltpu.SemaphoreType.DMA((MOE_BUFFERS,)),
        ],
    )
    return pl.pallas_call(
        _moe_kernel,
        grid_spec=grid_spec,
        out_shape=jax.ShapeDtypeStruct(h_all.shape, h_all.dtype),
        input_output_aliases={9: 0},
        compiler_params=pltpu.CompilerParams(
            dimension_semantics=("arbitrary",), vmem_limit_bytes=VMEM_LIMIT),
        name="moe_pairs",
    )(ea, eb, chg, stage, flag, nblk, tok0, gtok, stok, h_all, g_ffn, wr, br, w_gate, w_up, w_down)


def _final_norm_kernel(h_ref, g_ref, o_ref):
    for first in range(0, NORM_TILE, MIX_TILE):
        o_ref[first:first + MIX_TILE, :] = _rms(_load_tokens(h_ref, MIX_TILE, first), g_ref[...])


def _final_norm_call(h_all, g, n_rows):
    return pl.pallas_call(
        _final_norm_kernel,
        grid=(n_rows // NORM_TILE,),
        in_specs=[pl.BlockSpec((NORM_TILE * SUBLANES, LANES), lambda i: (i, 0)),
                  pl.BlockSpec((1, D_MODEL), lambda i: (0, 0))],
        out_specs=pl.BlockSpec((NORM_TILE, D_MODEL), lambda i: (i, 0)),
        out_shape=jax.ShapeDtypeStruct((n_rows, D_MODEL), jnp.float32),
        compiler_params=pltpu.CompilerParams(dimension_semantics=("arbitrary",), vmem_limit_bytes=VMEM_LIMIT),
        name="final_norm",
    )(h_all, g)


def _dispatch_tables(cls, n_blocks):
    n_all = cls.shape[0]
    tok_bits = (n_all - 1).bit_length()
    key = jnp.sort((cls << tok_bits) | jnp.arange(n_all, dtype=jnp.int32))
    order = key & ((1 << tok_bits) - 1)
    ids = jnp.arange(N_CLASS_IDS, dtype=jnp.int32)
    counts = jnp.sum((cls[:, None] == ids[None, :]).astype(jnp.int32), axis=0)
    start = jnp.cumsum(counts) - counts
    padded = (counts + MOE_ROWS - 1) // MOE_ROWS * MOE_ROWS
    pad_end = jnp.cumsum(padded)
    pad_start = pad_end - padded
    nblk = (pad_end[-1] // MOE_ROWS).astype(jnp.int32)

    kidx = jnp.arange(n_blocks, dtype=jnp.int32)
    real = kidx < nblk
    p0 = kidx * MOE_ROWS
    cb = jnp.sum((pad_end[None, :] <= p0[:, None]).astype(jnp.int32), axis=1)
    cb = jnp.minimum(cb, N_CLASS_IDS - 1)
    r0 = p0 - pad_start[cb]
    nv = jnp.where(real, jnp.clip(counts[cb] - r0, 0, MOE_ROWS), 0).astype(jnp.int32)
    grp = cb >> 6
    blk_a = grp * EXPERTS_PER_GROUP + ((cb >> 3) & 7)
    blk_b = grp * EXPERTS_PER_GROUP + (cb & 7)
    r = jnp.arange(MOE_ROWS, dtype=jnp.int32)[None, :]
    blk_src0 = start[cb] + r0

    never = jnp.int32(n_blocks)
    experts = jnp.arange(N_EXPERTS, dtype=jnp.int32)[:, None]
    used = real[None, :] & ((blk_a[None, :] == experts) | (blk_b[None, :] == experts))
    first_use = jnp.min(jnp.where(used, kidx[None, :], never), axis=1)
    new_a = real & (first_use[blk_a] == kidx)
    new_b = real & (first_use[blk_b] == kidx)
    filler = new_a & new_b
    blk_step = kidx + jnp.cumsum(filler.astype(jnp.int32))
    n_act = nblk + jnp.sum(filler.astype(jnp.int32))

    n_steps = n_blocks + N_EXPERTS // 2 + 1
    step = jnp.arange(n_steps, dtype=jnp.int32)
    k_done = jnp.sum((real[None, :] & (blk_step[None, :] <= step[:, None])).astype(jnp.int32), axis=1) - 1
    kc = jnp.clip(k_done, 0, n_blocks - 1)
    kn = jnp.clip(k_done + 1, 0, n_blocks - 1)
    active = step < n_act
    is_block = active & (k_done >= 0) & (blk_step[kc] == step)
    is_filler = active & ~is_block

    spare = n_all + (step[:, None] & (PAD_BANKS - 1)) * MOE_ROWS + r
    src = jnp.clip(blk_src0[kc][:, None] + r, 0, n_all - 1)
    rows = jnp.where(is_block[:, None] & (r < nv[kc][:, None]), order[src], spare).astype(jnp.int32)
    last = jnp.clip(nblk - 1, 0, n_blocks - 1)
    ea = jnp.where(is_block, blk_a[kc], jnp.where(is_filler, blk_a[kn], blk_a[last]))
    eb = jnp.where(is_block, blk_b[kc], jnp.where(is_filler, blk_a[kn], blk_b[last]))
    staged_here = jnp.where(
        is_block, jnp.where(new_b[kc], blk_b[kc], jnp.where(new_a[kc], blk_a[kc], -1)),
        jnp.where(is_filler, blk_a[kn], -1))
    flag = (staged_here >= 0).astype(jnp.int32)
    far = jnp.int32(n_steps)
    next_evt = lax.cummin(jnp.where(flag > 0, step, far), axis=0, reverse=True)
    last_evt = jnp.max(jnp.where(flag > 0, step, 0))
    stage = staged_here[jnp.where(next_evt < far, next_evt, last_evt)]

    sink = n_all + SINK_BANK * MOE_ROWS + r
    source = n_all + SOURCE_BANK * MOE_ROWS + r
    nxt = jnp.concatenate([rows[2:], rows[-2:]], axis=0)
    prv = jnp.concatenate([rows[:1], rows[:-1]], axis=0)
    gtok = jnp.where((step + 2 < n_act)[:, None], nxt, source)
    stok = jnp.where(((step >= 1) & (step <= n_act))[:, None], prv, sink)
    shape3 = (n_steps, 1, MOE_ROWS)
    rows, gtok, stok = rows * SUBLANES, gtok * SUBLANES, stok * SUBLANES
    chg = jnp.stack([jnp.concatenate([jnp.ones((1,), jnp.int32), (e[1:] != e[:-1]).astype(jnp.int32)])
                     for e in (ea, eb)])
    return (rows[:2].reshape(2, 1, MOE_ROWS), gtok.reshape(shape3), stok.reshape(shape3), ea, eb, chg,
            stage, flag, n_act.reshape(1).astype(jnp.int32))


def _over_sublanes(w):
    return jnp.broadcast_to(w[:, None, :], (w.shape[0], SUBLANES, w.shape[1]))


def kernel(x, meta_tokens, norm_mix_g, w_in, conf_dw_w, conf_dw_b, conf_ln_g, conf_ln_b, sc_conv_w, w_out,
           norm_ffn_g, w_router_group, b_router_group, w_router_expert, b_router_expert, w_exp_gate,
           w_exp_up, w_exp_down, final_norm_g):
    bsz, seq, d = x.shape
    depth = w_in.shape[0]
    n_main = bsz * seq
    n_all = n_main + N_META
    n_blocks = n_all // MOE_ROWS + N_CLASSES
    f32 = jnp.float32

    zero_halo = (jnp.zeros((HALO_A, CONF_W), f32), jnp.zeros((HALO_C, SC_W), f32))
    assert d == SUBLANES * LANES
    h_all = None

    for l in range(depth):
        w_r = jnp.concatenate([w_router_group[l], w_router_expert[l]], axis=1).astype(f32)
        w_r = jnp.pad(w_r, ((0, 0), (0, ROUTER_COLS - w_r.shape[1])))
        w_r = w_r.astype(jnp.bfloat16)
        b_r = jnp.concatenate([b_router_group[l], b_router_expert[l]]).astype(f32)
        b_r = jnp.pad(b_r, (0, ROUTER_COLS - b_r.shape[0])).reshape(1, ROUTER_COLS)
        g_ffn = norm_ffn_g[l].reshape(1, d)
        wts = (norm_mix_g[l].reshape(1, d), w_in[l].astype(jnp.bfloat16), _over_sublanes(conf_dw_w[l]),
               conf_dw_b[l].reshape(1, CONF_W), conf_ln_g[l].reshape(1, CONF_W),
               conf_ln_b[l].reshape(1, CONF_W), _over_sublanes(sc_conv_w[l]), w_out[l].astype(jnp.bfloat16),
               g_ffn, w_r, b_r)

        main = dict(row0=0, n_batch=bsz, n_tiles=seq // MIX_TILE, tt=MIX_TILE)
        if h_all is None:
            h_meta, cls_meta, halo_a, halo_c = _mixer_call(
                meta_tokens.astype(x.dtype), *zero_halo, wts, row0=0, n_batch=1, n_tiles=1, tt=N_META,
                fresh="rows")
            tail = jnp.concatenate([h_meta, jnp.zeros((SPARE_ROWS, d), x.dtype)], axis=0)
            h_all, cls_main, _, _ = _mixer_call(
                x.reshape(n_main, d), halo_a[0], halo_c[0], wts, fresh="tokens", tail=tail, **main)
        else:
            h_all, cls_meta, halo_a, halo_c = _mixer_call(
                h_all, *zero_halo, wts, row0=n_main, n_batch=1, n_tiles=1, tt=N_META)
            h_all, cls_main, _, _ = _mixer_call(h_all, halo_a[0], halo_c[0], wts, **main)

        cls = jnp.concatenate([cls_main[:, 0], cls_meta[:, 0]])
        h_all = _moe_call(h_all, _dispatch_tables(cls, n_blocks), g_ffn, w_r, b_r,
                          w_exp_gate, w_exp_up, w_exp_down, layer=l)

    out = _final_norm_call(h_all, final_norm_g.reshape(1, d), n_main)
    return out.reshape(bsz, seq, d)
```

```python
import functools

import jax
import jax.numpy as jnp
from jax import lax
from jax.experimental import pallas as pl
from jax.experimental.pallas import tpu as pltpu

D_MODEL = 1024
N_META = 16
CONF_W = 512
SC_W = 512
PROJ_W = 2 * CONF_W + 3 * SC_W
CONF_KERNEL = 31
SC_KERNEL = 3
N_GROUPS = 4
EXPERTS_PER_GROUP = 8
N_EXPERTS = N_GROUPS * EXPERTS_PER_GROUP
D_EXPERT = 512
EPS = 1e-6

LANES = 128
SUBLANES = 8
HALO_A = 32
HALO_C = 8
CONV_ROWS = 32
MIX_TILE = 512
NORM_TILE = 2048
MOE_ROWS = 256
N_CLASS_IDS = N_GROUPS * 64
N_CLASSES = N_GROUPS * (EXPERTS_PER_GROUP * (EXPERTS_PER_GROUP - 1) // 2)
VMEM_LIMIT = 56 * 1024 * 1024
ROUTER_COLS = LANES
TABLE_STEPS = 8
MOE_BUFFERS = 3
PAD_BANKS = 8
SINK_BANK = PAD_BANKS
SOURCE_BANK = PAD_BANKS + 1
TAIL_TILES = 6
SPARE_ROWS = TAIL_TILES * MIX_TILE - N_META
assert SPARE_ROWS >= (PAD_BANKS + 2) * MOE_ROWS


def _load_tokens(ref, n, first=0):
    return jnp.concatenate(
        [ref[pl.ds(first * SUBLANES + s, n, stride=SUBLANES), :] for s in range(SUBLANES)], axis=-1)


def _store_tokens(ref, x, n):
    for s in range(SUBLANES):
        ref[pl.ds(s, n, stride=SUBLANES), :] = x[:, s * LANES:(s + 1) * LANES]


def _rms(x, g):
    return x * lax.rsqrt(jnp.mean(x * x, axis=-1, keepdims=True) + EPS) * g


def _router_logits(hn_bf16, w_ref, b_ref):
    return jnp.dot(hn_bf16, w_ref[...], preferred_element_type=jnp.float32) + b_ref[...]


N_MIXER_INPUTS = 14


def _mixer_kernel(h_ref, *refs, tt, src_tm, dst_tm, n_batch, has_tail):
    if not has_tail:
        return _mixer_tile(h_ref, *refs, tt=tt, src_tm=src_tm, dst_tm=dst_tm)
    tail_ref, refs = refs[0], refs[1:]
    b = pl.program_id(0)
    pl.when(b < n_batch)(lambda: _mixer_tile(h_ref, *refs, tt=tt, src_tm=src_tm, dst_tm=dst_tm))

    @pl.when(b == n_batch)
    def _():
        hout_ref = refs[N_MIXER_INPUTS - 1]
        _store_tokens(hout_ref, tail_ref[...], tt)


def _mixer_tile(h_ref, halo_a_ref, halo_c_ref, g_mix_ref, w_in_ref, dw_w_ref, dw_b_ref, ln_g_ref,
                ln_b_ref, sc_w_ref, w_out_ref, g_ffn_ref, wr_ref, br_ref,
                hout_ref, cls_ref, halo_a_out, halo_c_out,
                aext, ashift, cext, sb_buf, ybuf, *, tt, src_tm, dst_tm):
    j = pl.program_id(1)

    @pl.when(j == 0)
    def _():
        aext[0:HALO_A, :] = halo_a_ref[...]
        cext[0:HALO_C, :] = halo_c_ref[...]

    h = _load_tokens(h_ref, tt) if src_tm else h_ref[...]
    xn = _rms(h, g_mix_ref[...]).astype(jnp.bfloat16)
    ug = jnp.dot(xn, w_in_ref[:, 0:2 * CONF_W], preferred_element_type=jnp.float32)
    aext[HALO_A:HALO_A + tt, :] = ug[:, 0:CONF_W] * jax.nn.sigmoid(ug[:, CONF_W:2 * CONF_W])
    bcv = jnp.dot(xn, w_in_ref[:, 2 * CONF_W:PROJ_W], preferred_element_type=jnp.float32)
    sb_buf[...] = bcv[:, 0:SC_W]
    cext[HALO_C:HALO_C + tt, :] = bcv[:, SC_W:2 * SC_W] * bcv[:, 2 * SC_W:3 * SC_W]

    n_shift_rows = HALO_A + tt - SUBLANES
    for p in range(1, SUBLANES):
        ashift[p - 1, 0:n_shift_rows, :] = aext[p:p + n_shift_rows, :]

    ch = min(tt, CONV_ROWS)
    tiles = (ch // SUBLANES, SUBLANES, CONF_W)
    for c0 in range(0, tt, ch):
        acc = jnp.broadcast_to(dw_b_ref[...], tiles)
        for k in range(CONF_KERNEL):
            row = HALO_A - (CONF_KERNEL - 1) + c0 + k
            phase = row % SUBLANES
            if phase == 0:
                tap = aext[row:row + ch, :]
            else:
                tap = ashift[phase - 1, row - phase:row - phase + ch, :]
            acc = acc + tap.reshape(tiles) * dw_w_ref[k]
        acc = acc.reshape(ch, CONF_W)
        mu = jnp.mean(acc, axis=-1, keepdims=True)
        xc = acc - mu
        ln = xc * lax.rsqrt(jnp.mean(xc * xc, axis=-1, keepdims=True) + EPS) * ln_g_ref[...] + ln_b_ref[...]
        ybuf[c0:c0 + ch, 0:CONF_W] = jax.nn.silu(ln).astype(jnp.bfloat16)
        sacc = None
        for k in range(SC_KERNEL):
            row = HALO_C - (SC_KERNEL - 1) + c0 + k
            term = cext[row:row + ch, :].reshape(tiles) * sc_w_ref[k]
            sacc = term if sacc is None else sacc + term
        ybuf[c0:c0 + ch, CONF_W:CONF_W + SC_W] = (
            sb_buf[c0:c0 + ch, :] * sacc.reshape(ch, SC_W)).astype(jnp.bfloat16)

    new_halo_a = aext[tt:tt + HALO_A, :]
    new_halo_c = cext[tt:tt + HALO_C, :]
    aext[0:HALO_A, :] = new_halo_a
    cext[0:HALO_C, :] = new_halo_c
    halo_a_out[0] = new_halo_a
    halo_c_out[0] = new_halo_c

    h_new = h + jnp.dot(ybuf[...], w_out_ref[...], preferred_element_type=jnp.float32)
    if dst_tm:
        _store_tokens(hout_ref, h_new, tt)
    else:
        hout_ref[...] = h_new

    logits = _router_logits(_rms(h_new, g_ffn_ref[...]).astype(jnp.bfloat16), wr_ref, br_ref)
    lane = lax.broadcasted_iota(jnp.int32, logits.shape, 1)
    neg = jnp.float32(-jnp.inf)
    big = jnp.int32(1 << 20)
    glog = jnp.where(lane < N_GROUPS, logits, neg)
    gmax = jnp.max(glog, axis=-1, keepdims=True)
    g_idx = jnp.min(jnp.where(glog == gmax, lane, big), axis=-1, keepdims=True)
    el = lane - N_GROUPS
    in_group = (el >= 0) & (el < N_EXPERTS) & ((el >> 3) == g_idx)
    l1 = jnp.where(in_group, logits, neg)
    m1 = jnp.max(l1, axis=-1, keepdims=True)
    i1 = jnp.min(jnp.where(l1 == m1, el, big), axis=-1, keepdims=True)
    l2 = jnp.where(el == i1, neg, l1)
    m2 = jnp.max(l2, axis=-1, keepdims=True)
    i2 = jnp.min(jnp.where(l2 == m2, el, big), axis=-1, keepdims=True)
    lo_e = jnp.minimum(i1, i2) & (EXPERTS_PER_GROUP - 1)
    hi_e = jnp.maximum(i1, i2) & (EXPERTS_PER_GROUP - 1)
    cls_ref[...] = g_idx * 64 + lo_e * 8 + hi_e


def _mixer_call(h_src, halo_a, halo_c, wts, *, row0, n_batch, n_tiles, tt, fresh=None, tail=None):
    blk0 = row0 // tt
    n_rows = n_batch * n_tiles * tt
    last = n_batch * n_tiles - 1
    tail_tiles = 0 if tail is None else tail.shape[0] // tt
    const = lambda shape: pl.BlockSpec(shape, lambda b, j: (0,) * len(shape))
    tile = lambda b, j: jnp.minimum(b * n_tiles + j, last)
    tm_spec = pl.BlockSpec((tt * SUBLANES, LANES), lambda b, j: (blk0 + b * n_tiles + j, 0))
    row_spec = pl.BlockSpec((tt, D_MODEL), lambda b, j: (blk0 + tile(b, j), 0))
    if fresh is None:
        in_spec, out_spec, out_struct = tm_spec, tm_spec, jax.ShapeDtypeStruct(h_src.shape, jnp.float32)
    elif fresh == "rows":
        in_spec, out_spec, out_struct = row_spec, row_spec, jax.ShapeDtypeStruct(h_src.shape, jnp.float32)
    else:
        in_spec = row_spec
        tail_at = lambda b, j: jnp.where(b == n_batch, jnp.minimum(j, tail_tiles - 1), j)
        out_spec = pl.BlockSpec((tt * SUBLANES, LANES), lambda b, j: (b * n_tiles + tail_at(b, j), 0))
        out_struct = jax.ShapeDtypeStruct(((n_rows + tail.shape[0]) * SUBLANES, LANES), jnp.float32)
    tail_specs, tail_args = [], []
    if tail is not None:
        tail_specs = [pl.BlockSpec(
            (tt, D_MODEL), lambda b, j: (jnp.where(b == n_batch, jnp.minimum(j, tail_tiles - 1), 0), 0))]
        tail_args = [tail]
    return pl.pallas_call(
        functools.partial(_mixer_kernel, tt=tt, src_tm=fresh is None, dst_tm=fresh != "rows",
                          n_batch=n_batch, has_tail=tail is not None),
        grid=(n_batch + (tail is not None), n_tiles),
        in_specs=[
            in_spec, *tail_specs,
            const((HALO_A, CONF_W)), const((HALO_C, SC_W)),
            const((1, D_MODEL)), const((D_MODEL, PROJ_W)),
            const((CONF_KERNEL, SUBLANES, CONF_W)), const((1, CONF_W)), const((1, CONF_W)), const((1, CONF_W)),
            const((SC_KERNEL, SUBLANES, SC_W)), const((CONF_W + SC_W, D_MODEL)), const((1, D_MODEL)),
            const((D_MODEL, ROUTER_COLS)), const((1, ROUTER_COLS)),
        ],
        out_specs=[
            out_spec,
            pl.BlockSpec((tt, 1), lambda b, j: (tile(b, j), 0)),
            pl.BlockSpec((1, HALO_A, CONF_W), lambda b, j: (jnp.minimum(b, n_batch - 1), 0, 0)),
            pl.BlockSpec((1, HALO_C, SC_W), lambda b, j: (jnp.minimum(b, n_batch - 1), 0, 0)),
        ],
        out_shape=[
            out_struct,
            jax.ShapeDtypeStruct((n_rows, 1), jnp.int32),
            jax.ShapeDtypeStruct((n_batch, HALO_A, CONF_W), jnp.float32),
            jax.ShapeDtypeStruct((n_batch, HALO_C, SC_W), jnp.float32),
        ],
        scratch_shapes=[
            pltpu.VMEM((HALO_A + tt, CONF_W), jnp.float32),
            pltpu.VMEM((SUBLANES - 1, HALO_A + tt, CONF_W), jnp.float32),
            pltpu.VMEM((HALO_C + tt, SC_W), jnp.float32),
            pltpu.VMEM((tt, SC_W), jnp.float32),
            pltpu.VMEM((tt, CONF_W + SC_W), jnp.bfloat16),
        ],
        input_output_aliases={0: 0} if fresh is None else {},
        compiler_params=pltpu.CompilerParams(
            dimension_semantics=("arbitrary", "arbitrary"), vmem_limit_bytes=VMEM_LIMIT),
        name=f"mixer_t{tt}",
    )(h_src, *tail_args, halo_a, halo_c, *wts)


def _moe_kernel(ea_ref, eb_ref, chg_ref, stage_ref, flag_ref, nblk_ref,
                tok0_ref, gtok_ref, stok_ref, h_in, g_ffn_ref, wr_ref, br_ref,
                wg_st, wu_st, wd_st,
                h_out,
                xbuf0, xbuf1, xbuf2, obuf0, obuf1, obuf2, wg_c, wu_c, wd_c, wga, wua, wda, wgb, wub, wdb,
                gsem, ssem):
    i = pl.program_id(0)
    nblk = nblk_ref[0]
    xbuf = (xbuf0, xbuf1, xbuf2)
    obuf = (obuf0, obuf1, obuf2)
    cur_a = (wga, wua, wda)
    cur_b = (wgb, wub, wdb)

    def token_tile(tok_ref, r, blk=0):
        return pl.ds(pl.multiple_of(tok_ref[blk, 0, r], SUBLANES), SUBLANES)

    def gather_row(tok_ref, r, dst_slot, blk=0):
        return pltpu.make_async_copy(
            h_in.at[token_tile(tok_ref, r, blk), :], xbuf[dst_slot].at[pl.ds(r * SUBLANES, SUBLANES), :],
            gsem.at[dst_slot])

    def scatter_row(r, src_slot, blk):
        return pltpu.make_async_copy(
            obuf[src_slot].at[pl.ds(r * SUBLANES, SUBLANES), :], h_out.at[token_tile(stok_ref, r, blk), :],
            ssem.at[src_slot])

    def wait_gather(dst_slot):
        pltpu.make_async_copy(
            h_in.at[pl.ds(0, MOE_ROWS * SUBLANES), :], xbuf[dst_slot], gsem.at[dst_slot]).wait()

    def wait_scatter(src_slot):
        pltpu.make_async_copy(
            obuf[src_slot], h_out.at[pl.ds(0, MOE_ROWS * SUBLANES), :], ssem.at[src_slot]).wait()

    @pl.when(i == 0)
    def _():
        obuf2[...] = jnp.zeros_like(obuf2)

        def start(r, c):
            gather_row(tok0_ref, r, 0, blk=0).start()
            gather_row(tok0_ref, r, 1, blk=1).start()
            return c

        lax.fori_loop(0, MOE_ROWS, start, 0)

    def step(slot):
        ahead = (slot + 2) % MOE_BUFFERS
        behind = (slot - 1) % MOE_BUFFERS
        ea = ea_ref[i]
        eb = eb_ref[i]
        wait_gather(slot)

        @pl.when(i >= 2)
        def _():
            wait_scatter(slot)

        @pl.when(flag_ref[i] != 0)
        def _():
            dst = stage_ref[i] & (EXPERTS_PER_GROUP - 1)
            wg_c[dst] = wg_st[0, 0].astype(jnp.bfloat16)
            wu_c[dst] = wu_st[0, 0].astype(jnp.bfloat16)
            wd_c[dst] = wd_st[0, 0].astype(jnp.bfloat16)

        for e, changed, cur in ((ea, chg_ref[0, i], cur_a), (eb, chg_ref[1, i], cur_b)):
            @pl.when(changed != 0)
            def _(e=e, cur=cur):
                at = e & (EXPERTS_PER_GROUP - 1)
                cur[0][...] = wg_c[at]
                cur[1][...] = wu_c[at]
                cur[2][...] = wd_c[at]

        at_step = lax.rem(i, TABLE_STEPS)
        for r in range(MOE_ROWS):
            gather_row(gtok_ref, r, ahead, at_step).start()
        for r in range(MOE_ROWS):
            scatter_row(r, behind, at_step).start()

        x = _load_tokens(xbuf[slot], MOE_ROWS)
        xb = _rms(x, g_ffn_ref[...]).astype(jnp.bfloat16)
        logits = _router_logits(xb, wr_ref, br_ref)
        lane = lax.broadcasted_iota(jnp.int32, logits.shape, 1)
        neg = jnp.float32(-jnp.inf)
        glog = jnp.where(lane < N_GROUPS, logits, neg)
        gmax = jnp.max(glog, axis=-1, keepdims=True)
        gsum = jnp.sum(jnp.exp(glog - gmax), axis=-1, keepdims=True)
        grp = ea >> 3
        l_g = jnp.sum(jnp.where(lane == grp, logits, 0.0), axis=-1, keepdims=True)
        p_g = jnp.exp(l_g - gmax) / gsum
        l_a = jnp.sum(jnp.where(lane == N_GROUPS + ea, logits, 0.0), axis=-1, keepdims=True)
        l_b = jnp.sum(jnp.where(lane == N_GROUPS + eb, logits, 0.0), axis=-1, keepdims=True)
        w_a = p_g / (1.0 + jnp.exp(l_b - l_a))
        w_b = p_g / (1.0 + jnp.exp(l_a - l_b))

        def expert(cur):
            gate = jnp.dot(xb, cur[0][...], preferred_element_type=jnp.float32)
            up = jnp.dot(xb, cur[1][...], preferred_element_type=jnp.float32)
            hid = (jax.nn.silu(gate) * up).astype(jnp.bfloat16)
            return jnp.dot(hid, cur[2][...], preferred_element_type=jnp.float32)

        y = w_a * expert(cur_a) + w_b * expert(cur_b)
        _store_tokens(obuf[slot], x + y, MOE_ROWS)

        @pl.when(i == nblk)
        def _():
            wait_gather((slot + 1) % MOE_BUFFERS)
            wait_gather(ahead)
            wait_scatter((slot - 2) % MOE_BUFFERS)
            wait_scatter(behind)

    rotation = lax.rem(i, MOE_BUFFERS)
    for slot in range(MOE_BUFFERS):
        pl.when((i <= nblk) & (rotation == slot))(functools.partial(step, slot))


def _moe_call(h_all, tables, g_ffn, wr, br, w_gate, w_up, w_down, *, layer):
    tok0, gtok, stok, ea, eb, chg, stage, flag, nblk = tables
    n_steps = gtok.shape[0]
    const = lambda shape: pl.BlockSpec(shape, lambda i, *_: (0,) * len(shape))
    staged = lambda shape: pl.BlockSpec(
        (1, 1) + shape, lambda i, ea_, eb_, ch_, st_, fl_, nb_: (layer, st_[i], 0, 0))
    tok_spec = lambda imap: pl.BlockSpec((TABLE_STEPS, 1, MOE_ROWS), imap, memory_space=pltpu.SMEM)
    grid_spec = pltpu.PrefetchScalarGridSpec(
        num_scalar_prefetch=6,
        grid=(n_steps,),
        in_specs=[
            pl.BlockSpec((2, 1, MOE_ROWS), lambda i, *_: (0, 0, 0), memory_space=pltpu.SMEM),
            tok_spec(lambda i, *_: (i // TABLE_STEPS, 0, 0)), tok_spec(lambda i, *_: (i // TABLE_STEPS, 0, 0)),
            pl.BlockSpec(memory_space=pl.ANY),
            const((1, D_MODEL)), const((D_MODEL, ROUTER_COLS)), const((1, ROUTER_COLS)),
            staged((D_MODEL, D_EXPERT)), staged((D_MODEL, D_EXPERT)), staged((D_EXPERT, D_MODEL)),
        ],
        out_specs=pl.BlockSpec(memory_space=pl.ANY),
        scratch_shapes=[
            *[pltpu.VMEM((MOE_ROWS * SUBLANES, LANES), jnp.float32)] * (2 * MOE_BUFFERS),
            pltpu.VMEM((EXPERTS_PER_GROUP, D_MODEL, D_EXPERT), jnp.bfloat16),
            pltpu.VMEM((EXPERTS_PER_GROUP, D_MODEL, D_EXPERT), jnp.bfloat16),
            pltpu.VMEM((EXPERTS_PER_GROUP, D_EXPERT, D_MODEL), jnp.bfloat16),
            pltpu.VMEM((D_MODEL, D_EXPERT), jnp.bfloat16),
            pltpu.VMEM((D_MODEL, D_EXPERT), jnp.bfloat16),
            pltpu.VMEM((D_EXPERT, D_MODEL), jnp.bfloat16),
            pltpu.VMEM((D_MODEL, D_EXPERT), jnp.bfloat16),
            pltpu.VMEM((D_MODEL, D_EXPERT), jnp.bfloat16),
            pltpu.VMEM((D_EXPERT, D_MODEL), jnp.bfloat16),
            pltpu.SemaphoreType.DMA((MOE_BUFFERS,)),
            pltpu.SemaphoreType.DMA((MOE_BUFFERS,)),
        ],
    )
    return pl.pallas_call(
        _moe_kernel,
        grid_spec=grid_spec,
        out_shape=jax.ShapeDtypeStruct(h_all.shape, h_all.dtype),
        input_output_aliases={9: 0},
        compiler_params=pltpu.CompilerParams(
            dimension_semantics=("arbitrary",), vmem_limit_bytes=VMEM_LIMIT),
        name="moe_pairs",
    )(ea, eb, chg, stage, flag, nblk, tok0, gtok, stok, h_all, g_ffn, wr, br, w_gate, w_up, w_down)


def _final_norm_kernel(h_ref, g_ref, o_ref):
    for first in range(0, NORM_TILE, MIX_TILE):
        o_ref[first:first + MIX_TILE, :] = _rms(_load_tokens(h_ref, MIX_TILE, first), g_ref[...])


def _final_norm_call(h_all, g, n_rows):
    return pl.pallas_call(
        _final_norm_kernel,
        grid=(n_rows // NORM_TILE,),
        in_specs=[pl.BlockSpec((NORM_TILE * SUBLANES, LANES), lambda i: (i, 0)),
                  pl.BlockSpec((1, D_MODEL), lambda i: (0, 0))],
        out_specs=pl.BlockSpec((NORM_TILE, D_MODEL), lambda i: (i, 0)),
        out_shape=jax.ShapeDtypeStruct((n_rows, D_MODEL), jnp.float32),
        compiler_params=pltpu.CompilerParams(dimension_semantics=("arbitrary",), vmem_limit_bytes=VMEM_LIMIT),
        name="final_norm",
    )(h_all, g)


def _dispatch_tables(cls, n_blocks):
    n_all = cls.shape[0]
    tok_bits = (n_all - 1).bit_length()
    key = jnp.sort((cls << tok_bits) | jnp.arange(n_all, dtype=jnp.int32))
    order = key & ((1 << tok_bits) - 1)
    ids = jnp.arange(N_CLASS_IDS, dtype=jnp.int32)
    counts = jnp.sum((cls[:, None] == ids[None, :]).astype(jnp.int32), axis=0)
    start = jnp.cumsum(counts) - counts
    padded = (counts + MOE_ROWS - 1) // MOE_ROWS * MOE_ROWS
    pad_end = jnp.cumsum(padded)
    pad_start = pad_end - padded
    nblk = (pad_end[-1] // MOE_ROWS).astype(jnp.int32)

    kidx = jnp.arange(n_blocks, dtype=jnp.int32)
    real = kidx < nblk
    p0 = kidx * MOE_ROWS
    cb = jnp.sum((pad_end[None, :] <= p0[:, None]).astype(jnp.int32), axis=1)
    cb = jnp.minimum(cb, N_CLASS_IDS - 1)
    r0 = p0 - pad_start[cb]
    nv = jnp.where(real, jnp.clip(counts[cb] - r0, 0, MOE_ROWS), 0).astype(jnp.int32)
    grp = cb >> 6
    blk_a = grp * EXPERTS_PER_GROUP + ((cb >> 3) & 7)
    blk_b = grp * EXPERTS_PER_GROUP + (cb & 7)
    r = jnp.arange(MOE_ROWS, dtype=jnp.int32)[None, :]
    blk_src0 = start[cb] + r0

    never = jnp.int32(n_blocks)
    experts = jnp.arange(N_EXPERTS, dtype=jnp.int32)[:, None]
    used = real[None, :] & ((blk_a[None, :] == experts) | (blk_b[None, :] == experts))
    first_use = jnp.min(jnp.where(used, kidx[None, :], never), axis=1)
    new_a = real & (first_use[blk_a] == kidx)
    new_b = real & (first_use[blk_b] == kidx)
    filler = new_a & new_b
    blk_step = kidx + jnp.cumsum(filler.astype(jnp.int32))
    n_act = nblk + jnp.sum(filler.astype(jnp.int32))

    n_steps = -(-(n_blocks + N_EXPERTS // 2 + 1) // TABLE_STEPS) * TABLE_STEPS
    step = jnp.arange(n_steps, dtype=jnp.int32)
    k_done = jnp.sum((real[None, :] & (blk_step[None, :] <= step[:, None])).astype(jnp.int32), axis=1) - 1
    kc = jnp.clip(k_done, 0, n_blocks - 1)
    kn = jnp.clip(k_done + 1, 0, n_blocks - 1)
    active = step < n_act
    is_block = active & (k_done >= 0) & (blk_step[kc] == step)
    is_filler = active & ~is_block

    spare = n_all + (step[:, None] & (PAD_BANKS - 1)) * MOE_ROWS + r
    src = jnp.clip(blk_src0[kc][:, None] + r, 0, n_all - 1)
    rows = jnp.where(is_block[:, None] & (r < nv[kc][:, None]), order[src], spare).astype(jnp.int32)
    last = jnp.clip(nblk - 1, 0, n_blocks - 1)
    ea = jnp.where(is_block, blk_a[kc], jnp.where(is_filler, blk_a[kn], blk_a[last]))
    eb = jnp.where(is_block, blk_b[kc], jnp.where(is_filler, blk_a[kn], blk_b[last]))
    staged_here = jnp.where(
        is_block, jnp.where(new_b[kc], blk_b[kc], jnp.where(new_a[kc], blk_a[kc], -1)),
        jnp.where(is_filler, blk_a[kn], -1))
    flag = (staged_here >= 0).astype(jnp.int32)
    far = jnp.int32(n_steps)
    next_evt = lax.cummin(jnp.where(flag > 0, step, far), axis=0, reverse=True)
    last_evt = jnp.max(jnp.where(flag > 0, step, 0))
    stage = staged_here[jnp.where(next_evt < far, next_evt, last_evt)]

    sink = n_all + SINK_BANK * MOE_ROWS + r
    source = n_all + SOURCE_BANK * MOE_ROWS + r
    nxt = jnp.concatenate([rows[2:], rows[-2:]], axis=0)
    prv = jnp.concatenate([rows[:1], rows[:-1]], axis=0)
    gtok = jnp.where((step + 2 < n_act)[:, None], nxt, source)
    stok = jnp.where(((step >= 1) & (step <= n_act))[:, None], prv, sink)
    shape3 = (n_steps, 1, MOE_ROWS)
    rows, gtok, stok = rows * SUBLANES, gtok * SUBLANES, stok * SUBLANES
    chg = jnp.stack([jnp.concatenate([jnp.ones((1,), jnp.int32), (e[1:] != e[:-1]).astype(jnp.int32)])
                     for e in (ea, eb)])
    return (rows[:2].reshape(2, 1, MOE_ROWS), gtok.reshape(shape3), stok.reshape(shape3), ea, eb, chg,
            stage, flag, n_act.reshape(1).astype(jnp.int32))


def _over_sublanes(w):
    return jnp.broadcast_to(w[:, None, :], (w.shape[0], SUBLANES, w.shape[1]))


def kernel(x, meta_tokens, norm_mix_g, w_in, conf_dw_w, conf_dw_b, conf_ln_g, conf_ln_b, sc_conv_w, w_out,
           norm_ffn_g, w_router_group, b_router_group, w_router_expert, b_router_expert, w_exp_gate,
           w_exp_up, w_exp_down, final_norm_g):
    bsz, seq, d = x.shape
    depth = w_in.shape[0]
    n_main = bsz * seq
    n_all = n_main + N_META
    n_blocks = n_all // MOE_ROWS + N_CLASSES
    f32 = jnp.float32

    zero_halo = (jnp.zeros((HALO_A, CONF_W), f32), jnp.zeros((HALO_C, SC_W), f32))
    assert d == SUBLANES * LANES
    h_all = None

    for l in range(depth):
        w_r = jnp.concatenate([w_router_group[l], w_router_expert[l]], axis=1).astype(f32)
        w_r = jnp.pad(w_r, ((0, 0), (0, ROUTER_COLS - w_r.shape[1])))
        w_r = w_r.astype(jnp.bfloat16)
        b_r = jnp.concatenate([b_router_group[l], b_router_expert[l]]).astype(f32)
        b_r = jnp.pad(b_r, (0, ROUTER_COLS - b_r.shape[0])).reshape(1, ROUTER_COLS)
        g_ffn = norm_ffn_g[l].reshape(1, d)
        wts = (norm_mix_g[l].reshape(1, d), w_in[l].astype(jnp.bfloat16), _over_sublanes(conf_dw_w[l]),
               conf_dw_b[l].reshape(1, CONF_W), conf_ln_g[l].reshape(1, CONF_W),
               conf_ln_b[l].reshape(1, CONF_W), _over_sublanes(sc_conv_w[l]), w_out[l].astype(jnp.bfloat16),
               g_ffn, w_r, b_r)

        main = dict(row0=0, n_batch=bsz, n_tiles=seq // MIX_TILE, tt=MIX_TILE)
        if h_all is None:
            h_meta, cls_meta, halo_a, halo_c = _mixer_call(
                meta_tokens.astype(x.dtype), *zero_halo, wts, row0=0, n_batch=1, n_tiles=1, tt=N_META,
                fresh="rows")
            tail = jnp.concatenate([h_meta, jnp.zeros((SPARE_ROWS, d), x.dtype)], axis=0)
            h_all, cls_main, _, _ = _mixer_call(
                x.reshape(n_main, d), halo_a[0], halo_c[0], wts, fresh="tokens", tail=tail, **main)
        else:
            h_all, cls_meta, halo_a, halo_c = _mixer_call(
                h_all, *zero_halo, wts, row0=n_main, n_batch=1, n_tiles=1, tt=N_META)
            h_all, cls_main, _, _ = _mixer_call(h_all, halo_a[0], halo_c[0], wts, **main)

        cls = jnp.concatenate([cls_main[:, 0], cls_meta[:, 0]])
        h_all = _moe_call(h_all, _dispatch_tables(cls, n_blocks), g_ffn, w_r, b_r,
                          w_exp_gate, w_exp_up, w_exp_down, layer=l)

    out = _final_norm_call(h_all, final_norm_g.reshape(1, d), n_main)
    return out.reshape(bsz, seq, d)
```

```python
import functools

import jax
import jax.numpy as jnp
from jax import lax
from jax.experimental import pallas as pl
from jax.experimental.pallas import tpu as pltpu

D_MODEL = 1024
N_META = 16
CONF_W = 512
SC_W = 512
PROJ_W = 2 * CONF_W + 3 * SC_W
CONF_KERNEL = 31
SC_KERNEL = 3
N_GROUPS = 4
EXPERTS_PER_GROUP = 8
N_EXPERTS = N_GROUPS * EXPERTS_PER_GROUP
D_EXPERT = 512
EPS = 1e-6

LANES = 128
SUBLANES = 8
HALO_A = 32
HALO_C = 8
CONV_ROWS = 32
MIX_TILE = 512
NORM_TILE = 2048
MOE_ROWS = 256
N_CLASS_IDS = N_GROUPS * 64
N_CLASSES = N_GROUPS * (EXPERTS_PER_GROUP * (EXPERTS_PER_GROUP - 1) // 2)
VMEM_LIMIT = 56 * 1024 * 1024
ROUTER_COLS = LANES
MOE_BUFFERS = 3
PAD_BANKS = 8
SINK_BANK = PAD_BANKS
SOURCE_BANK = PAD_BANKS + 1
TAIL_TILES = 6
SPARE_ROWS = TAIL_TILES * MIX_TILE - N_META
assert SPARE_ROWS >= (PAD_BANKS + 2) * MOE_ROWS


def _load_tokens(ref, n, first=0):
    return jnp.concatenate(
        [ref[pl.ds(first * SUBLANES + s, n, stride=SUBLANES), :] for s in range(SUBLANES)], axis=-1)


def _store_tokens(ref, x, n):
    for s in range(SUBLANES):
        ref[pl.ds(s, n, stride=SUBLANES), :] = x[:, s * LANES:(s + 1) * LANES]


def _rms(x, g):
    return x * lax.rsqrt(jnp.mean(x * x, axis=-1, keepdims=True) + EPS) * g


def _router_logits(hn_bf16, w_ref, b_ref):
    return jnp.dot(hn_bf16, w_ref[...], preferred_element_type=jnp.float32) + b_ref[...]


N_MIXER_INPUTS = 14


def _mixer_kernel(h_ref, *refs, tt, src_tm, dst_tm, n_batch, has_tail):
    if not has_tail:
        return _mixer_tile(h_ref, *refs, tt=tt, src_tm=src_tm, dst_tm=dst_tm)
    tail_ref, refs = refs[0], refs[1:]
    b = pl.program_id(0)
    pl.when(b < n_batch)(lambda: _mixer_tile(h_ref, *refs, tt=tt, src_tm=src_tm, dst_tm=dst_tm))

    @pl.when(b == n_batch)
    def _():
        hout_ref = refs[N_MIXER_INPUTS - 1]
        _store_tokens(hout_ref, tail_ref[...], tt)


def _mixer_tile(h_ref, halo_a_ref, halo_c_ref, g_mix_ref, w_in_ref, dw_w_ref, dw_b_ref, ln_g_ref,
                ln_b_ref, sc_w_ref, w_out_ref, g_ffn_ref, wr_ref, br_ref,
                hout_ref, cls_ref, halo_a_out, halo_c_out,
                aext, ashift, cext, sb_buf, ybuf, *, tt, src_tm, dst_tm):
    j = pl.program_id(1)

    @pl.when(j == 0)
    def _():
        aext[0:HALO_A, :] = halo_a_ref[...]
        cext[0:HALO_C, :] = halo_c_ref[...]

    h = _load_tokens(h_ref, tt) if src_tm else h_ref[...]
    xn = _rms(h, g_mix_ref[...]).astype(jnp.bfloat16)
    ug = jnp.dot(xn, w_in_ref[:, 0:2 * CONF_W], preferred_element_type=jnp.float32)
    aext[HALO_A:HALO_A + tt, :] = ug[:, 0:CONF_W] * jax.nn.sigmoid(ug[:, CONF_W:2 * CONF_W])
    bcv = jnp.dot(xn, w_in_ref[:, 2 * CONF_W:PROJ_W], preferred_element_type=jnp.float32)
    sb_buf[...] = bcv[:, 0:SC_W]
    cext[HALO_C:HALO_C + tt, :] = bcv[:, SC_W:2 * SC_W] * bcv[:, 2 * SC_W:3 * SC_W]

    n_shift_rows = HALO_A + tt - SUBLANES
    for p in range(1, SUBLANES):
        ashift[p - 1, 0:n_shift_rows, :] = aext[p:p + n_shift_rows, :]

    ch = min(tt, CONV_ROWS)
    tiles = (ch // SUBLANES, SUBLANES, CONF_W)
    for c0 in range(0, tt, ch):
        acc = jnp.broadcast_to(dw_b_ref[...], tiles)
        for k in range(CONF_KERNEL):
            row = HALO_A - (CONF_KERNEL - 1) + c0 + k
            phase = row % SUBLANES
            if phase == 0:
                tap = aext[row:row + ch, :]
            else:
                tap = ashift[phase - 1, row - phase:row - phase + ch, :]
            acc = acc + tap.reshape(tiles) * dw_w_ref[k]
        acc = acc.reshape(ch, CONF_W)
        mu = jnp.mean(acc, axis=-1, keepdims=True)
        xc = acc - mu
        ln = xc * lax.rsqrt(jnp.mean(xc * xc, axis=-1, keepdims=True) + EPS) * ln_g_ref[...] + ln_b_ref[...]
        ybuf[c0:c0 + ch, 0:CONF_W] = jax.nn.silu(ln).astype(jnp.bfloat16)
        sacc = None
        for k in range(SC_KERNEL):
            row = HALO_C - (SC_KERNEL - 1) + c0 + k
            term = cext[row:row + ch, :].reshape(tiles) * sc_w_ref[k]
            sacc = term if sacc is None else sacc + term
        ybuf[c0:c0 + ch, CONF_W:CONF_W + SC_W] = (
            sb_buf[c0:c0 + ch, :] * sacc.reshape(ch, SC_W)).astype(jnp.bfloat16)

    new_halo_a = aext[tt:tt + HALO_A, :]
    new_halo_c = cext[tt:tt + HALO_C, :]
    aext[0:HALO_A, :] = new_halo_a
    cext[0:HALO_C, :] = new_halo_c
    halo_a_out[0] = new_halo_a
    halo_c_out[0] = new_halo_c

    h_new = h + jnp.dot(ybuf[...], w_out_ref[...], preferred_element_type=jnp.float32)
    if dst_tm:
        _store_tokens(hout_ref, h_new, tt)
    else:
        hout_ref[...] = h_new

    logits = _router_logits(_rms(h_new, g_ffn_ref[...]).astype(jnp.bfloat16), wr_ref, br_ref)
    lane = lax.broadcasted_iota(jnp.int32, logits.shape, 1)
    neg = jnp.float32(-jnp.inf)
    big = jnp.int32(1 << 20)
    glog = jnp.where(lane < N_GROUPS, logits, neg)
    gmax = jnp.max(glog, axis=-1, keepdims=True)
    g_idx = jnp.min(jnp.where(glog == gmax, lane, big), axis=-1, keepdims=True)
    el = lane - N_GROUPS
    in_group = (el >= 0) & (el < N_EXPERTS) & ((el >> 3) == g_idx)
    l1 = jnp.where(in_group, logits, neg)
    m1 = jnp.max(l1, axis=-1, keepdims=True)
    i1 = jnp.min(jnp.where(l1 == m1, el, big), axis=-1, keepdims=True)
    l2 = jnp.where(el == i1, neg, l1)
    m2 = jnp.max(l2, axis=-1, keepdims=True)
    i2 = jnp.min(jnp.where(l2 == m2, el, big), axis=-1, keepdims=True)
    lo_e = jnp.minimum(i1, i2) & (EXPERTS_PER_GROUP - 1)
    hi_e = jnp.maximum(i1, i2) & (EXPERTS_PER_GROUP - 1)
    cls_ref[...] = g_idx * 64 + lo_e * 8 + hi_e


def _mixer_call(h_src, halo_a, halo_c, wts, *, row0, n_batch, n_tiles, tt, fresh=None, tail=None):
    blk0 = row0 // tt
    n_rows = n_batch * n_tiles * tt
    last = n_batch * n_tiles - 1
    tail_tiles = 0 if tail is None else tail.shape[0] // tt
    const = lambda shape: pl.BlockSpec(shape, lambda b, j: (0,) * len(shape))
    tile = lambda b, j: jnp.minimum(b * n_tiles + j, last)
    tm_spec = pl.BlockSpec((tt * SUBLANES, LANES), lambda b, j: (blk0 + b * n_tiles + j, 0))
    row_spec = pl.BlockSpec((tt, D_MODEL), lambda b, j: (blk0 + tile(b, j), 0))
    if fresh is None:
        in_spec, out_spec, out_struct = tm_spec, tm_spec, jax.ShapeDtypeStruct(h_src.shape, jnp.float32)
    elif fresh == "rows":
        in_spec, out_spec, out_struct = row_spec, row_spec, jax.ShapeDtypeStruct(h_src.shape, jnp.float32)
    else:
        in_spec = row_spec
        tail_at = lambda b, j: jnp.where(b == n_batch, jnp.minimum(j, tail_tiles - 1), j)
        out_spec = pl.BlockSpec((tt * SUBLANES, LANES), lambda b, j: (b * n_tiles + tail_at(b, j), 0))
        out_struct = jax.ShapeDtypeStruct(((n_rows + tail.shape[0]) * SUBLANES, LANES), jnp.float32)
    tail_specs, tail_args = [], []
    if tail is not None:
        tail_specs = [pl.BlockSpec(
            (tt, D_MODEL), lambda b, j: (jnp.where(b == n_batch, jnp.minimum(j, tail_tiles - 1), 0), 0))]
        tail_args = [tail]
    return pl.pallas_call(
        functools.partial(_mixer_kernel, tt=tt, src_tm=fresh is None, dst_tm=fresh != "rows",
                          n_batch=n_batch, has_tail=tail is not None),
        grid=(n_batch + (tail is not None), n_tiles),
        in_specs=[
            in_spec, *tail_specs,
            const((HALO_A, CONF_W)), const((HALO_C, SC_W)),
            const((1, D_MODEL)), const((D_MODEL, PROJ_W)),
            const((CONF_KERNEL, SUBLANES, CONF_W)), const((1, CONF_W)), const((1, CONF_W)), const((1, CONF_W)),
            const((SC_KERNEL, SUBLANES, SC_W)), const((CONF_W + SC_W, D_MODEL)), const((1, D_MODEL)),
            const((D_MODEL, ROUTER_COLS)), const((1, ROUTER_COLS)),
        ],
        out_specs=[
            out_spec,
            pl.BlockSpec((tt, 1), lambda b, j: (tile(b, j), 0)),
            pl.BlockSpec((1, HALO_A, CONF_W), lambda b, j: (jnp.minimum(b, n_batch - 1), 0, 0)),
            pl.BlockSpec((1, HALO_C, SC_W), lambda b, j: (jnp.minimum(b, n_batch - 1), 0, 0)),
        ],
        out_shape=[
            out_struct,
            jax.ShapeDtypeStruct((n_rows, 1), jnp.int32),
            jax.ShapeDtypeStruct((n_batch, HALO_A, CONF_W), jnp.float32),
            jax.ShapeDtypeStruct((n_batch, HALO_C, SC_W), jnp.float32),
        ],
        scratch_shapes=[
            pltpu.VMEM((HALO_A + tt, CONF_W), jnp.float32),
            pltpu.VMEM((SUBLANES - 1, HALO_A + tt, CONF_W), jnp.float32),
            pltpu.VMEM((HALO_C + tt, SC_W), jnp.float32),
            pltpu.VMEM((tt, SC_W), jnp.float32),
            pltpu.VMEM((tt, CONF_W + SC_W), jnp.bfloat16),
        ],
        input_output_aliases={0: 0} if fresh is None else {},
        compiler_params=pltpu.CompilerParams(
            dimension_semantics=("arbitrary", "arbitrary"), vmem_limit_bytes=VMEM_LIMIT),
        name=f"mixer_t{tt}",
    )(h_src, *tail_args, halo_a, halo_c, *wts)


def _moe_kernel(ea_ref, eb_ref, chg_ref, stage_ref, flag_ref, nblk_ref,
                tok0_ref, gtok_ref, stok_ref, h_in, g_ffn_ref, wr_ref, br_ref,
                wg_st, wu_st, wd_st,
                h_out,
                xbuf0, xbuf1, xbuf2, obuf0, obuf1, obuf2, wg_c, wu_c, wd_c, wga, wua, wda, wgb, wub, wdb,
                gsem, ssem):
    i = pl.program_id(0)
    nblk = nblk_ref[0]
    xbuf = (xbuf0, xbuf1, xbuf2)
    obuf = (obuf0, obuf1, obuf2)
    cur_a = (wga, wua, wda)
    cur_b = (wgb, wub, wdb)

    def token_tile(tok_ref, r, blk=0):
        return pl.ds(pl.multiple_of(tok_ref[blk, 0, r], SUBLANES), SUBLANES)

    def gather_row(tok_ref, r, dst_slot, blk=0):
        return pltpu.make_async_copy(
            h_in.at[token_tile(tok_ref, r, blk), :], xbuf[dst_slot].at[pl.ds(r * SUBLANES, SUBLANES), :],
            gsem.at[dst_slot])

    def scatter_row(r, src_slot):
        return pltpu.make_async_copy(
            obuf[src_slot].at[pl.ds(r * SUBLANES, SUBLANES), :], h_out.at[token_tile(stok_ref, r), :],
            ssem.at[src_slot])

    def wait_gather(dst_slot):
        pltpu.make_async_copy(
            h_in.at[pl.ds(0, MOE_ROWS * SUBLANES), :], xbuf[dst_slot], gsem.at[dst_slot]).wait()

    def wait_scatter(src_slot):
        pltpu.make_async_copy(
            obuf[src_slot], h_out.at[pl.ds(0, MOE_ROWS * SUBLANES), :], ssem.at[src_slot]).wait()

    @pl.when(i == 0)
    def _():
        obuf2[...] = jnp.zeros_like(obuf2)

        def start(r, c):
            gather_row(tok0_ref, r, 0, blk=0).start()
            gather_row(tok0_ref, r, 1, blk=1).start()
            return c

        lax.fori_loop(0, MOE_ROWS, start, 0)

    def step(slot):
        ahead = (slot + 2) % MOE_BUFFERS
        behind = (slot - 1) % MOE_BUFFERS
        ea = ea_ref[i]
        eb = eb_ref[i]
        wait_gather(slot)

        @pl.when(i >= 2)
        def _():
            wait_scatter(slot)

        @pl.when(flag_ref[i] != 0)
        def _():
            dst = stage_ref[i] & (EXPERTS_PER_GROUP - 1)
            wg_c[dst] = wg_st[0, 0].astype(jnp.bfloat16)
            wu_c[dst] = wu_st[0, 0].astype(jnp.bfloat16)
            wd_c[dst] = wd_st[0, 0].astype(jnp.bfloat16)

        for e, changed, cur in ((ea, chg_ref[0, i], cur_a), (eb, chg_ref[1, i], cur_b)):
            @pl.when(changed != 0)
            def _(e=e, cur=cur):
                at = e & (EXPERTS_PER_GROUP - 1)
                cur[0][...] = wg_c[at]
                cur[1][...] = wu_c[at]
                cur[2][...] = wd_c[at]

        for r in range(MOE_ROWS):
            gather_row(gtok_ref, r, ahead).start()
        for r in range(MOE_ROWS):
            scatter_row(r, behind).start(priority=1)

        x = _load_tokens(xbuf[slot], MOE_ROWS)
        xb = _rms(x, g_ffn_ref[...]).astype(jnp.bfloat16)
        logits = _router_logits(xb, wr_ref, br_ref)
        lane = lax.broadcasted_iota(jnp.int32, logits.shape, 1)
        neg = jnp.float32(-jnp.inf)
        glog = jnp.where(lane < N_GROUPS, logits, neg)
        gmax = jnp.max(glog, axis=-1, keepdims=True)
        gsum = jnp.sum(jnp.exp(glog - gmax), axis=-1, keepdims=True)
        grp = ea >> 3
        l_g = jnp.sum(jnp.where(lane == grp, logits, 0.0), axis=-1, keepdims=True)
        p_g = jnp.exp(l_g - gmax) / gsum
        l_a = jnp.sum(jnp.where(lane == N_GROUPS + ea, logits, 0.0), axis=-1, keepdims=True)
        l_b = jnp.sum(jnp.where(lane == N_GROUPS + eb, logits, 0.0), axis=-1, keepdims=True)
        w_a = p_g / (1.0 + jnp.exp(l_b - l_a))
        w_b = p_g / (1.0 + jnp.exp(l_a - l_b))

        def expert(cur):
            gate = jnp.dot(xb, cur[0][...], preferred_element_type=jnp.float32)
            up = jnp.dot(xb, cur[1][...], preferred_element_type=jnp.float32)
            hid = (jax.nn.silu(gate) * up).astype(jnp.bfloat16)
            return jnp.dot(hid, cur[2][...], preferred_element_type=jnp.float32)

        y = w_a * expert(cur_a) + w_b * expert(cur_b)
        _store_tokens(obuf[slot], x + y, MOE_ROWS)

        @pl.when(i == nblk)
        def _():
            wait_gather((slot + 1) % MOE_BUFFERS)
            wait_gather(ahead)
            wait_scatter((slot - 2) % MOE_BUFFERS)
            wait_scatter(behind)

    rotation = lax.rem(i, MOE_BUFFERS)
    for slot in range(MOE_BUFFERS):
        pl.when((i <= nblk) & (rotation == slot))(functools.partial(step, slot))


def _moe_call(h_all, tables, g_ffn, wr, br, w_gate, w_up, w_down, *, layer):
    tok0, gtok, stok, ea, eb, chg, stage, flag, nblk = tables
    n_steps = gtok.shape[0]
    const = lambda shape: pl.BlockSpec(shape, lambda i, *_: (0,) * len(shape))
    staged = lambda shape: pl.BlockSpec(
        (1, 1) + shape, lambda i, ea_, eb_, ch_, st_, fl_, nb_: (layer, st_[i], 0, 0))
    tok_spec = lambda imap: pl.BlockSpec((1, 1, MOE_ROWS), imap, memory_space=pltpu.SMEM)
    grid_spec = pltpu.PrefetchScalarGridSpec(
        num_scalar_prefetch=6,
        grid=(n_steps,),
        in_specs=[
            pl.BlockSpec((2, 1, MOE_ROWS), lambda i, *_: (0, 0, 0), memory_space=pltpu.SMEM),
            tok_spec(lambda i, *_: (i, 0, 0)), tok_spec(lambda i, *_: (i, 0, 0)),
            pl.BlockSpec(memory_space=pl.ANY),
            const((1, D_MODEL)), const((D_MODEL, ROUTER_COLS)), const((1, ROUTER_COLS)),
            staged((D_MODEL, D_EXPERT)), staged((D_MODEL, D_EXPERT)), staged((D_EXPERT, D_MODEL)),
        ],
        out_specs=pl.BlockSpec(memory_space=pl.ANY),
        scratch_shapes=[
            *[pltpu.VMEM((MOE_ROWS * SUBLANES, LANES), jnp.float32)] * (2 * MOE_BUFFERS),
            pltpu.VMEM((EXPERTS_PER_GROUP, D_MODEL, D_EXPERT), jnp.bfloat16),
            pltpu.VMEM((EXPERTS_PER_GROUP, D_MODEL, D_EXPERT), jnp.bfloat16),
            pltpu.VMEM((EXPERTS_PER_GROUP, D_EXPERT, D_MODEL), jnp.bfloat16),
            pltpu.VMEM((D_MODEL, D_EXPERT), jnp.bfloat16),
            pltpu.VMEM((D_MODEL, D_EXPERT), jnp.bfloat16),
            pltpu.VMEM((D_EXPERT, D_MODEL), jnp.bfloat16),
            pltpu.VMEM((D_MODEL, D_EXPERT), jnp.bfloat16),
            pltpu.VMEM((D_MODEL, D_EXPERT), jnp.bfloat16),
            pltpu.VMEM((D_EXPERT, D_MODEL), jnp.bfloat16),
            pltpu.SemaphoreType.DMA((MOE_BUFFERS,)),
            pltpu.SemaphoreType.DMA((MOE_BUFFERS,)),
        ],
    )
    return pl.pallas_call(
        _moe_kernel,
        grid_spec=grid_spec,
        out_shape=jax.ShapeDtypeStruct(h_all.shape, h_all.dtype),
        input_output_aliases={9: 0},
        compiler_params=pltpu.CompilerParams(
            dimension_semantics=("arbitrary",), vmem_limit_bytes=VMEM_LIMIT),
        name="moe_pairs",
    )(ea, eb, chg, stage, flag, nblk, tok0, gtok, stok, h_all, g_ffn, wr, br, w_gate, w_up, w_down)


def _final_norm_kernel(h_ref, g_ref, o_ref):
    for first in range(0, NORM_TILE, MIX_TILE):
        o_ref[first:first + MIX_TILE, :] = _rms(_load_tokens(h_ref, MIX_TILE, first), g_ref[...])


def _final_norm_call(h_all, g, n_rows):
    return pl.pallas_call(
        _final_norm_kernel,
        grid=(n_rows // NORM_TILE,),
        in_specs=[pl.BlockSpec((NORM_TILE * SUBLANES, LANES), lambda i: (i, 0)),
                  pl.BlockSpec((1, D_MODEL), lambda i: (0, 0))],
        out_specs=pl.BlockSpec((NORM_TILE, D_MODEL), lambda i: (i, 0)),
        out_shape=jax.ShapeDtypeStruct((n_rows, D_MODEL), jnp.float32),
        compiler_params=pltpu.CompilerParams(dimension_semantics=("arbitrary",), vmem_limit_bytes=VMEM_LIMIT),
        name="final_norm",
    )(h_all, g)


def _dispatch_tables(cls, n_blocks):
    n_all = cls.shape[0]
    tok_bits = (n_all - 1).bit_length()
    key = jnp.sort((cls << tok_bits) | jnp.arange(n_all, dtype=jnp.int32))
    order = key & ((1 << tok_bits) - 1)
    ids = jnp.arange(N_CLASS_IDS, dtype=jnp.int32)
    counts = jnp.sum((cls[:, None] == ids[None, :]).astype(jnp.int32), axis=0)
    start = jnp.cumsum(counts) - counts
    padded = (counts + MOE_ROWS - 1) // MOE_ROWS * MOE_ROWS
    pad_end = jnp.cumsum(padded)
    pad_start = pad_end - padded
    nblk = (pad_end[-1] // MOE_ROWS).astype(jnp.int32)

    kidx = jnp.arange(n_blocks, dtype=jnp.int32)
    real = kidx < nblk
    p0 = kidx * MOE_ROWS
    cb = jnp.sum((pad_end[None, :] <= p0[:, None]).astype(jnp.int32), axis=1)
    cb = jnp.minimum(cb, N_CLASS_IDS - 1)
    r0 = p0 - pad_start[cb]
    nv = jnp.where(real, jnp.clip(counts[cb] - r0, 0, MOE_ROWS), 0).astype(jnp.int32)
    grp = cb >> 6
    blk_a = grp * EXPERTS_PER_GROUP + ((cb >> 3) & 7)
    blk_b = grp * EXPERTS_PER_GROUP + (cb & 7)
    r = jnp.arange(MOE_ROWS, dtype=jnp.int32)[None, :]
    blk_src0 = start[cb] + r0

    never = jnp.int32(n_blocks)
    experts = jnp.arange(N_EXPERTS, dtype=jnp.int32)[:, None]
    used = real[None, :] & ((blk_a[None, :] == experts) | (blk_b[None, :] == experts))
    first_use = jnp.min(jnp.where(used, kidx[None, :], never), axis=1)
    new_a = real & (first_use[blk_a] == kidx)
    new_b = real & (first_use[blk_b] == kidx)
    filler = new_a & new_b
    blk_step = kidx + jnp.cumsum(filler.astype(jnp.int32))
    n_act = nblk + jnp.sum(filler.astype(jnp.int32))

    n_steps = n_blocks + N_EXPERTS // 2 + 1
    step = jnp.arange(n_steps, dtype=jnp.int32)
    k_done = jnp.sum((real[None, :] & (blk_step[None, :] <= step[:, None])).astype(jnp.int32), axis=1) - 1
    kc = jnp.clip(k_done, 0, n_blocks - 1)
    kn = jnp.clip(k_done + 1, 0, n_blocks - 1)
    active = step < n_act
    is_block = active & (k_done >= 0) & (blk_step[kc] == step)
    is_filler = active & ~is_block

    spare = n_all + (step[:, None] & (PAD_BANKS - 1)) * MOE_ROWS + r
    order_ext = jnp.concatenate([order, jnp.zeros((MOE_ROWS,), order.dtype)])
    runs = jax.vmap(lambda at: lax.dynamic_slice(order_ext, (at,), (MOE_ROWS,)))(jnp.clip(blk_src0[kc], 0, n_all))
    rows = jnp.where(is_block[:, None] & (r < nv[kc][:, None]), runs, spare).astype(jnp.int32)
    last = jnp.clip(nblk - 1, 0, n_blocks - 1)
    ea = jnp.where(is_block, blk_a[kc], jnp.where(is_filler, blk_a[kn], blk_a[last]))
    eb = jnp.where(is_block, blk_b[kc], jnp.where(is_filler, blk_a[kn], blk_b[last]))
    staged_here = jnp.where(
        is_block, jnp.where(new_b[kc], blk_b[kc], jnp.where(new_a[kc], blk_a[kc], -1)),
        jnp.where(is_filler, blk_a[kn], -1))
    flag = (staged_here >= 0).astype(jnp.int32)
    far = jnp.int32(n_steps)
    next_evt = lax.cummin(jnp.where(flag > 0, step, far), axis=0, reverse=True)
    last_evt = jnp.max(jnp.where(flag > 0, step, 0))
    stage = staged_here[jnp.where(next_evt < far, next_evt, last_evt)]

    sink = n_all + SINK_BANK * MOE_ROWS + r
    source = n_all + SOURCE_BANK * MOE_ROWS + r
    nxt = jnp.concatenate([rows[2:], rows[-2:]], axis=0)
    prv = jnp.concatenate([rows[:1], rows[:-1]], axis=0)
    gtok = jnp.where((step + 2 < n_act)[:, None], nxt, source)
    stok = jnp.where(((step >= 1) & (step <= n_act))[:, None], prv, sink)
    shape3 = (n_steps, 1, MOE_ROWS)
    rows, gtok, stok = rows * SUBLANES, gtok * SUBLANES, stok * SUBLANES
    chg = jnp.stack([jnp.concatenate([jnp.ones((1,), jnp.int32), (e[1:] != e[:-1]).astype(jnp.int32)])
                     for e in (ea, eb)])
    return (rows[:2].reshape(2, 1, MOE_ROWS), gtok.reshape(shape3), stok.reshape(shape3), ea, eb, chg,
            stage, flag, n_act.reshape(1).astype(jnp.int32))


def _over_sublanes(w):
    return jnp.broadcast_to(w[:, None, :], (w.shape[0], SUBLANES, w.shape[1]))


def kernel(x, meta_tokens, norm_mix_g, w_in, conf_dw_w, conf_dw_b, conf_ln_g, conf_ln_b, sc_conv_w, w_out,
           norm_ffn_g, w_router_group, b_router_group, w_router_expert, b_router_expert, w_exp_gate,
           w_exp_up, w_exp_down, final_norm_g):
    bsz, seq, d = x.shape
    depth = w_in.shape[0]
    n_main = bsz * seq
    n_all = n_main + N_META
    n_blocks = n_all // MOE_ROWS + N_CLASSES
    f32 = jnp.float32

    zero_halo = (jnp.zeros((HALO_A, CONF_W), f32), jnp.zeros((HALO_C, SC_W), f32))
    assert d == SUBLANES * LANES
    h_all = None

    for l in range(depth):
        w_r = jnp.concatenate([w_router_group[l], w_router_expert[l]], axis=1).astype(f32)
        w_r = jnp.pad(w_r, ((0, 0), (0, ROUTER_COLS - w_r.shape[1])))
        w_r = w_r.astype(jnp.bfloat16)
        b_r = jnp.concatenate([b_router_group[l], b_router_expert[l]]).astype(f32)
        b_r = jnp.pad(b_r, (0, ROUTER_COLS - b_r.shape[0])).reshape(1, ROUTER_COLS)
        g_ffn = norm_ffn_g[l].reshape(1, d)
        wts = (norm_mix_g[l].reshape(1, d), w_in[l].astype(jnp.bfloat16), _over_sublanes(conf_dw_w[l]),
               conf_dw_b[l].reshape(1, CONF_W), conf_ln_g[l].reshape(1, CONF_W),
               conf_ln_b[l].reshape(1, CONF_W), _over_sublanes(sc_conv_w[l]), w_out[l].astype(jnp.bfloat16),
               g_ffn, w_r, b_r)

        main = dict(row0=0, n_batch=bsz, n_tiles=seq // MIX_TILE, tt=MIX_TILE)
        if h_all is None:
            h_meta, cls_meta, halo_a, halo_c = _mixer_call(
                meta_tokens.astype(x.dtype), *zero_halo, wts, row0=0, n_batch=1, n_tiles=1, tt=N_META,
                fresh="rows")
            tail = jnp.concatenate([h_meta, jnp.zeros((SPARE_ROWS, d), x.dtype)], axis=0)
            h_all, cls_main, _, _ = _mixer_call(
                x.reshape(n_main, d), halo_a[0], halo_c[0], wts, fresh="tokens", tail=tail, **main)
        else:
            h_all, cls_meta, halo_a, halo_c = _mixer_call(
                h_all, *zero_halo, wts, row0=n_main, n_batch=1, n_tiles=1, tt=N_META)
            h_all, cls_main, _, _ = _mixer_call(h_all, halo_a[0], halo_c[0], wts, **main)

        cls = jnp.concatenate([cls_main[:, 0], cls_meta[:, 0]])
        h_all = _moe_call(h_all, _dispatch_tables(cls, n_blocks), g_ffn, w_r, b_r,
                          w_exp_gate, w_exp_up, w_exp_down, layer=l)

    out = _final_norm_call(h_all, final_norm_g.reshape(1, d), n_main)
    return out.reshape(bsz, seq, d)
```

```python
import functools

import jax
import jax.numpy as jnp
from jax import lax
from jax.experimental import pallas as pl
from jax.experimental.pallas import tpu as pltpu

D_MODEL = 1024
N_META = 16
CONF_W = 512
SC_W = 512
PROJ_W = 2 * CONF_W + 3 * SC_W
CONF_KERNEL = 31
SC_KERNEL = 3
N_GROUPS = 4
EXPERTS_PER_GROUP = 8
N_EXPERTS = N_GROUPS * EXPERTS_PER_GROUP
D_EXPERT = 512
EPS = 1e-6

LANES = 128
SUBLANES = 8
HALO_A = 32
HALO_C = 8
CONV_ROWS = 32
MIX_TILE = 512
NORM_TILE = 2048
MOE_ROWS = 256
N_CLASS_IDS = N_GROUPS * 64
N_CLASSES = N_GROUPS * (EXPERTS_PER_GROUP * (EXPERTS_PER_GROUP - 1) // 2)
VMEM_LIMIT = 56 * 1024 * 1024
ROUTER_COLS = LANES
MOE_BUFFERS = 3
PAD_BANKS = 8
SINK_BANK = PAD_BANKS
SOURCE_BANK = PAD_BANKS + 1
TAIL_TILES = 6
SPARE_ROWS = TAIL_TILES * MIX_TILE - N_META
assert SPARE_ROWS >= (PAD_BANKS + 2) * MOE_ROWS


def _load_tokens(ref, n, first=0):
    return jnp.concatenate(
        [ref[pl.ds(first * SUBLANES + s, n, stride=SUBLANES), :] for s in range(SUBLANES)], axis=-1)


def _store_tokens(ref, x, n):
    for s in range(SUBLANES):
        ref[pl.ds(s, n, stride=SUBLANES), :] = x[:, s * LANES:(s + 1) * LANES]


def _rms(x, g):
    return x * lax.rsqrt(jnp.mean(x * x, axis=-1, keepdims=True) + EPS) * g


def _router_logits(hn_bf16, w_ref, b_ref):
    return jnp.dot(hn_bf16, w_ref[...], preferred_element_type=jnp.float32) + b_ref[...]


N_MIXER_INPUTS = 14


def _mixer_kernel(h_ref, *refs, tt, src_tm, dst_tm, n_batch, has_tail):
    if not has_tail:
        return _mixer_tile(h_ref, *refs, tt=tt, src_tm=src_tm, dst_tm=dst_tm)
    tail_ref, refs = refs[0], refs[1:]
    b = pl.program_id(0)
    pl.when(b < n_batch)(lambda: _mixer_tile(h_ref, *refs, tt=tt, src_tm=src_tm, dst_tm=dst_tm))

    @pl.when(b == n_batch)
    def _():
        hout_ref = refs[N_MIXER_INPUTS - 1]
        _store_tokens(hout_ref, tail_ref[...], tt)


def _mixer_tile(h_ref, halo_a_ref, halo_c_ref, g_mix_ref, w_in_ref, dw_w_ref, dw_b_ref, ln_g_ref,
                ln_b_ref, sc_w_ref, w_out_ref, g_ffn_ref, wr_ref, br_ref,
                hout_ref, cls_ref, halo_a_out, halo_c_out,
                aext, ashift, cext, sb_buf, ybuf, *, tt, src_tm, dst_tm):
    j = pl.program_id(1)

    @pl.when(j == 0)
    def _():
        aext[0:HALO_A, :] = halo_a_ref[...]
        cext[0:HALO_C, :] = halo_c_ref[...]

    h = _load_tokens(h_ref, tt) if src_tm else h_ref[...]
    xn = _rms(h, g_mix_ref[...]).astype(jnp.bfloat16)
    ug = jnp.dot(xn, w_in_ref[:, 0:2 * CONF_W], preferred_element_type=jnp.float32)
    aext[HALO_A:HALO_A + tt, :] = ug[:, 0:CONF_W] * jax.nn.sigmoid(ug[:, CONF_W:2 * CONF_W])
    bcv = jnp.dot(xn, w_in_ref[:, 2 * CONF_W:PROJ_W], preferred_element_type=jnp.float32)
    sb_buf[...] = bcv[:, 0:SC_W]
    cext[HALO_C:HALO_C + tt, :] = bcv[:, SC_W:2 * SC_W] * bcv[:, 2 * SC_W:3 * SC_W]

    n_shift_rows = HALO_A + tt - SUBLANES
    for p in range(1, SUBLANES):
        ashift[p - 1, 0:n_shift_rows, :] = aext[p:p + n_shift_rows, :]

    ch = min(tt, CONV_ROWS)
    tiles = (ch // SUBLANES, SUBLANES, CONF_W)
    for c0 in range(0, tt, ch):
        acc = jnp.broadcast_to(dw_b_ref[...], tiles)
        for k in range(CONF_KERNEL):
            row = HALO_A - (CONF_KERNEL - 1) + c0 + k
            phase = row % SUBLANES
            if phase == 0:
                tap = aext[row:row + ch, :]
            else:
                tap = ashift[phase - 1, row - phase:row - phase + ch, :]
            acc = acc + tap.reshape(tiles) * dw_w_ref[k]
        acc = acc.reshape(ch, CONF_W)
        mu = jnp.mean(acc, axis=-1, keepdims=True)
        xc = acc - mu
        ln = xc * lax.rsqrt(jnp.mean(xc * xc, axis=-1, keepdims=True) + EPS) * ln_g_ref[...] + ln_b_ref[...]
        ybuf[c0:c0 + ch, 0:CONF_W] = jax.nn.silu(ln).astype(jnp.bfloat16)
        sacc = None
        for k in range(SC_KERNEL):
            row = HALO_C - (SC_KERNEL - 1) + c0 + k
            term = cext[row:row + ch, :].reshape(tiles) * sc_w_ref[k]
            sacc = term if sacc is None else sacc + term
        ybuf[c0:c0 + ch, CONF_W:CONF_W + SC_W] = (
            sb_buf[c0:c0 + ch, :] * sacc.reshape(ch, SC_W)).astype(jnp.bfloat16)

    new_halo_a = aext[tt:tt + HALO_A, :]
    new_halo_c = cext[tt:tt + HALO_C, :]
    aext[0:HALO_A, :] = new_halo_a
    cext[0:HALO_C, :] = new_halo_c
    halo_a_out[0] = new_halo_a
    halo_c_out[0] = new_halo_c

    h_new = h + jnp.dot(ybuf[...], w_out_ref[...], preferred_element_type=jnp.float32)
    if dst_tm:
        _store_tokens(hout_ref, h_new, tt)
    else:
        hout_ref[...] = h_new

    logits = _router_logits(_rms(h_new, g_ffn_ref[...]).astype(jnp.bfloat16), wr_ref, br_ref)
    lane = lax.broadcasted_iota(jnp.int32, logits.shape, 1)
    neg = jnp.float32(-jnp.inf)
    big = jnp.int32(1 << 20)
    glog = jnp.where(lane < N_GROUPS, logits, neg)
    gmax = jnp.max(glog, axis=-1, keepdims=True)
    g_idx = jnp.min(jnp.where(glog == gmax, lane, big), axis=-1, keepdims=True)
    el = lane - N_GROUPS
    in_group = (el >= 0) & (el < N_EXPERTS) & ((el >> 3) == g_idx)
    l1 = jnp.where(in_group, logits, neg)
    m1 = jnp.max(l1, axis=-1, keepdims=True)
    i1 = jnp.min(jnp.where(l1 == m1, el, big), axis=-1, keepdims=True)
    l2 = jnp.where(el == i1, neg, l1)
    m2 = jnp.max(l2, axis=-1, keepdims=True)
    i2 = jnp.min(jnp.where(l2 == m2, el, big), axis=-1, keepdims=True)
    lo_e = jnp.minimum(i1, i2) & (EXPERTS_PER_GROUP - 1)
    hi_e = jnp.maximum(i1, i2) & (EXPERTS_PER_GROUP - 1)
    cls_ref[...] = g_idx * 64 + lo_e * 8 + hi_e


def _mixer_call(h_src, halo_a, halo_c, wts, *, row0, n_batch, n_tiles, tt, fresh=None, tail=None):
    blk0 = row0 // tt
    n_rows = n_batch * n_tiles * tt
    last = n_batch * n_tiles - 1
    tail_tiles = 0 if tail is None else tail.shape[0] // tt
    const = lambda shape: pl.BlockSpec(shape, lambda b, j: (0,) * len(shape))
    tile = lambda b, j: jnp.minimum(b * n_tiles + j, last)
    tm_spec = pl.BlockSpec((tt * SUBLANES, LANES), lambda b, j: (blk0 + b * n_tiles + j, 0))
    row_spec = pl.BlockSpec((tt, D_MODEL), lambda b, j: (blk0 + tile(b, j), 0))
    if fresh is None:
        in_spec, out_spec, out_struct = tm_spec, tm_spec, jax.ShapeDtypeStruct(h_src.shape, jnp.float32)
    elif fresh == "rows":
        in_spec, out_spec, out_struct = row_spec, row_spec, jax.ShapeDtypeStruct(h_src.shape, jnp.float32)
    else:
        in_spec = row_spec
        tail_at = lambda b, j: jnp.where(b == n_batch, jnp.minimum(j, tail_tiles - 1), j)
        out_spec = pl.BlockSpec((tt * SUBLANES, LANES), lambda b, j: (b * n_tiles + tail_at(b, j), 0))
        out_struct = jax.ShapeDtypeStruct(((n_rows + tail.shape[0]) * SUBLANES, LANES), jnp.float32)
    tail_specs, tail_args = [], []
    if tail is not None:
        tail_specs = [pl.BlockSpec(
            (tt, D_MODEL), lambda b, j: (jnp.where(b == n_batch, jnp.minimum(j, tail_tiles - 1), 0), 0))]
        tail_args = [tail]
    return pl.pallas_call(
        functools.partial(_mixer_kernel, tt=tt, src_tm=fresh is None, dst_tm=fresh != "rows",
                          n_batch=n_batch, has_tail=tail is not None),
        grid=(n_batch + (tail is not None), n_tiles),
        in_specs=[
            in_spec, *tail_specs,
            const((HALO_A, CONF_W)), const((HALO_C, SC_W)),
            const((1, D_MODEL)), const((D_MODEL, PROJ_W)),
            const((CONF_KERNEL, SUBLANES, CONF_W)), const((1, CONF_W)), const((1, CONF_W)), const((1, CONF_W)),
            const((SC_KERNEL, SUBLANES, SC_W)), const((CONF_W + SC_W, D_MODEL)), const((1, D_MODEL)),
            const((D_MODEL, ROUTER_COLS)), const((1, ROUTER_COLS)),
        ],
        out_specs=[
            out_spec,
            pl.BlockSpec((tt, 1), lambda b, j: (tile(b, j), 0)),
            pl.BlockSpec((1, HALO_A, CONF_W), lambda b, j: (jnp.minimum(b, n_batch - 1), 0, 0)),
            pl.BlockSpec((1, HALO_C, SC_W), lambda b, j: (jnp.minimum(b, n_batch - 1), 0, 0)),
        ],
        out_shape=[
            out_struct,
            jax.ShapeDtypeStruct((n_rows, 1), jnp.int32),
            jax.ShapeDtypeStruct((n_batch, HALO_A, CONF_W), jnp.float32),
            jax.ShapeDtypeStruct((n_batch, HALO_C, SC_W), jnp.float32),
        ],
        scratch_shapes=[
            pltpu.VMEM((HALO_A + tt, CONF_W), jnp.float32),
            pltpu.VMEM((SUBLANES - 1, HALO_A + tt, CONF_W), jnp.float32),
            pltpu.VMEM((HALO_C + tt, SC_W), jnp.float32),
            pltpu.VMEM((tt, SC_W), jnp.float32),
            pltpu.VMEM((tt, CONF_W + SC_W), jnp.bfloat16),
        ],
        input_output_aliases={0: 0} if fresh is None else {},
        compiler_params=pltpu.CompilerParams(
            dimension_semantics=("arbitrary", "arbitrary"), vmem_limit_bytes=VMEM_LIMIT),
        name=f"mixer_t{tt}",
    )(h_src, *tail_args, halo_a, halo_c, *wts)


def _moe_kernel(ea_ref, eb_ref, chg_ref, stage_ref, flag_ref, nblk_ref,
                tok0_ref, gtok_ref, stok_ref, h_in, g_ffn_ref, wr_ref, br_ref,
                wg_st, wu_st, wd_st,
                h_out,
                xbuf0, xbuf1, xbuf2, obuf0, obuf1, obuf2, wg_c, wu_c, wd_c, wga, wua, wda, wgb, wub, wdb,
                gsem, ssem):
    i = pl.program_id(0)
    nblk = nblk_ref[0]
    xbuf = (xbuf0, xbuf1, xbuf2)
    obuf = (obuf0, obuf1, obuf2)
    cur_a = (wga, wua, wda)
    cur_b = (wgb, wub, wdb)

    def token_tile(tok_ref, r, blk=0):
        return pl.ds(pl.multiple_of(tok_ref[blk, 0, r], SUBLANES), SUBLANES)

    def gather_row(tok_ref, r, dst_slot, blk=0):
        return pltpu.make_async_copy(
            h_in.at[token_tile(tok_ref, r, blk), :], xbuf[dst_slot].at[pl.ds(r * SUBLANES, SUBLANES), :],
            gsem.at[dst_slot])

    def scatter_row(r, src_slot):
        return pltpu.make_async_copy(
            obuf[src_slot].at[pl.ds(r * SUBLANES, SUBLANES), :], h_out.at[token_tile(stok_ref, r), :],
            ssem.at[src_slot])

    def wait_gather(dst_slot):
        pltpu.make_async_copy(
            h_in.at[pl.ds(0, MOE_ROWS * SUBLANES), :], xbuf[dst_slot], gsem.at[dst_slot]).wait()

    def wait_scatter(src_slot):
        pltpu.make_async_copy(
            obuf[src_slot], h_out.at[pl.ds(0, MOE_ROWS * SUBLANES), :], ssem.at[src_slot]).wait()

    @pl.when(i == 0)
    def _():
        obuf2[...] = jnp.zeros_like(obuf2)

        def start(r, c):
            gather_row(tok0_ref, r, 0, blk=0).start()
            gather_row(tok0_ref, r, 1, blk=1).start()
            return c

        lax.fori_loop(0, MOE_ROWS, start, 0)

    def step(slot):
        ahead = (slot + 2) % MOE_BUFFERS
        behind = (slot - 1) % MOE_BUFFERS
        ea = ea_ref[i]
        eb = eb_ref[i]
        wait_gather(slot)

        @pl.when(i >= 2)
        def _():
            wait_scatter(slot)

        @pl.when(flag_ref[i] != 0)
        def _():
            dst = stage_ref[i] & (EXPERTS_PER_GROUP - 1)
            wg_c[dst] = wg_st[0, 0].astype(jnp.bfloat16)
            wu_c[dst] = wu_st[0, 0].astype(jnp.bfloat16)
            wd_c[dst] = wd_st[0, 0].astype(jnp.bfloat16)

        for e, changed, cur in ((ea, chg_ref[0, i], cur_a), (eb, chg_ref[1, i], cur_b)):
            @pl.when(changed != 0)
            def _(e=e, cur=cur):
                at = e & (EXPERTS_PER_GROUP - 1)
                cur[0][...] = wg_c[at]
                cur[1][...] = wu_c[at]
                cur[2][...] = wd_c[at]

        for r in range(MOE_ROWS):
            gather_row(gtok_ref, r, ahead).start()
        for r in range(MOE_ROWS):
            scatter_row(r, behind).start()

        x = _load_tokens(xbuf[slot], MOE_ROWS)
        xb = _rms(x, g_ffn_ref[...]).astype(jnp.bfloat16)
        logits = _router_logits(xb, wr_ref, br_ref)
        lane = lax.broadcasted_iota(jnp.int32, logits.shape, 1)
        neg = jnp.float32(-jnp.inf)
        glog = jnp.where(lane < N_GROUPS, logits, neg)
        gmax = jnp.max(glog, axis=-1, keepdims=True)
        gsum = jnp.sum(jnp.exp(glog - gmax), axis=-1, keepdims=True)
        grp = ea >> 3
        l_g = jnp.sum(jnp.where(lane == grp, logits, 0.0), axis=-1, keepdims=True)
        p_g = jnp.exp(l_g - gmax) / gsum
        l_a = jnp.sum(jnp.where(lane == N_GROUPS + ea, logits, 0.0), axis=-1, keepdims=True)
        l_b = jnp.sum(jnp.where(lane == N_GROUPS + eb, logits, 0.0), axis=-1, keepdims=True)
        w_a = p_g / (1.0 + jnp.exp(l_b - l_a))
        w_b = p_g / (1.0 + jnp.exp(l_a - l_b))

        def expert(cur):
            gate = jnp.dot(xb, cur[0][...], preferred_element_type=jnp.float32)
            up = jnp.dot(xb, cur[1][...], preferred_element_type=jnp.float32)
            hid = (jax.nn.silu(gate) * up).astype(jnp.bfloat16)
            return jnp.dot(hid, cur[2][...], preferred_element_type=jnp.float32)

        y = w_a * expert(cur_a) + w_b * expert(cur_b)
        _store_tokens(obuf[slot], x + y, MOE_ROWS)

        @pl.when(i == nblk)
        def _():
            wait_gather((slot + 1) % MOE_BUFFERS)
            wait_gather(ahead)
            wait_scatter((slot - 2) % MOE_BUFFERS)
            wait_scatter(behind)

    rotation = lax.rem(i, MOE_BUFFERS)
    for slot in range(MOE_BUFFERS):
        pl.when((i <= nblk) & (rotation == slot))(functools.partial(step, slot))


def _moe_call(h_all, tables, g_ffn, wr, br, w_gate, w_up, w_down, *, layer):
    tok0, gtok, stok, ea, eb, chg, stage, flag, nblk = tables
    const = lambda shape: pl.BlockSpec(shape, lambda i, *_: (0,) * len(shape))
    staged = lambda shape: pl.BlockSpec(
        (1, 1) + shape, lambda i, ea_, eb_, ch_, st_, fl_, nb_: (layer, st_[i], 0, 0))
    tok_spec = lambda imap: pl.BlockSpec((1, 1, MOE_ROWS), imap, memory_space=pltpu.SMEM)
    grid_spec = pltpu.PrefetchScalarGridSpec(
        num_scalar_prefetch=6,
        grid=(nblk[0] + 1,),
        in_specs=[
            pl.BlockSpec((2, 1, MOE_ROWS), lambda i, *_: (0, 0, 0), memory_space=pltpu.SMEM),
            tok_spec(lambda i, *_: (i, 0, 0)), tok_spec(lambda i, *_: (i, 0, 0)),
            pl.BlockSpec(memory_space=pl.ANY),
            const((1, D_MODEL)), const((D_MODEL, ROUTER_COLS)), const((1, ROUTER_COLS)),
            staged((D_MODEL, D_EXPERT)), staged((D_MODEL, D_EXPERT)), staged((D_EXPERT, D_MODEL)),
        ],
        out_specs=pl.BlockSpec(memory_space=pl.ANY),
        scratch_shapes=[
            *[pltpu.VMEM((MOE_ROWS * SUBLANES, LANES), jnp.float32)] * (2 * MOE_BUFFERS),
            pltpu.VMEM((EXPERTS_PER_GROUP, D_MODEL, D_EXPERT), jnp.bfloat16),
            pltpu.VMEM((EXPERTS_PER_GROUP, D_MODEL, D_EXPERT), jnp.bfloat16),
            pltpu.VMEM((EXPERTS_PER_GROUP, D_EXPERT, D_MODEL), jnp.bfloat16),
            pltpu.VMEM((D_MODEL, D_EXPERT), jnp.bfloat16),
            pltpu.VMEM((D_MODEL, D_EXPERT), jnp.bfloat16),
            pltpu.VMEM((D_EXPERT, D_MODEL), jnp.bfloat16),
            pltpu.VMEM((D_MODEL, D_EXPERT), jnp.bfloat16),
            pltpu.VMEM((D_MODEL, D_EXPERT), jnp.bfloat16),
            pltpu.VMEM((D_EXPERT, D_MODEL), jnp.bfloat16),
            pltpu.SemaphoreType.DMA((MOE_BUFFERS,)),
            pltpu.SemaphoreType.DMA((MOE_BUFFERS,)),
        ],
    )
    return pl.pallas_call(
        _moe_kernel,
        grid_spec=grid_spec,
        out_shape=jax.ShapeDtypeStruct(h_all.shape, h_all.dtype),
        input_output_aliases={9: 0},
        compiler_params=pltpu.CompilerParams(
            dimension_semantics=("arbitrary",), vmem_limit_bytes=VMEM_LIMIT),
        name="moe_pairs",
    )(ea, eb, chg, stage, flag, nblk, tok0, gtok, stok, h_all, g_ffn, wr, br, w_gate, w_up, w_down)


def _final_norm_kernel(h_ref, g_ref, o_ref):
    for first in range(0, NORM_TILE, MIX_TILE):
        o_ref[first:first + MIX_TILE, :] = _rms(_load_tokens(h_ref, MIX_TILE, first), g_ref[...])


def _final_norm_call(h_all, g, n_rows):
    return pl.pallas_call(
        _final_norm_kernel,
        grid=(n_rows // NORM_TILE,),
        in_specs=[pl.BlockSpec((NORM_TILE * SUBLANES, LANES), lambda i: (i, 0)),
                  pl.BlockSpec((1, D_MODEL), lambda i: (0, 0))],
        out_specs=pl.BlockSpec((NORM_TILE, D_MODEL), lambda i: (i, 0)),
        out_shape=jax.ShapeDtypeStruct((n_rows, D_MODEL), jnp.float32),
        compiler_params=pltpu.CompilerParams(dimension_semantics=("arbitrary",), vmem_limit_bytes=VMEM_LIMIT),
        name="final_norm",
    )(h_all, g)


def _dispatch_tables(cls, n_blocks):
    n_all = cls.shape[0]
    tok_bits = (n_all - 1).bit_length()
    key = jnp.sort((cls << tok_bits) | jnp.arange(n_all, dtype=jnp.int32))
    order = key & ((1 << tok_bits) - 1)
    ids = jnp.arange(N_CLASS_IDS, dtype=jnp.int32)
    counts = jnp.sum((cls[:, None] == ids[None, :]).astype(jnp.int32), axis=0)
    start = jnp.cumsum(counts) - counts
    padded = (counts + MOE_ROWS - 1) // MOE_ROWS * MOE_ROWS
    pad_end = jnp.cumsum(padded)
    pad_start = pad_end - padded
    nblk = (pad_end[-1] // MOE_ROWS).astype(jnp.int32)

    kidx = jnp.arange(n_blocks, dtype=jnp.int32)
    real = kidx < nblk
    p0 = kidx * MOE_ROWS
    cb = jnp.sum((pad_end[None, :] <= p0[:, None]).astype(jnp.int32), axis=1)
    cb = jnp.minimum(cb, N_CLASS_IDS - 1)
    r0 = p0 - pad_start[cb]
    nv = jnp.where(real, jnp.clip(counts[cb] - r0, 0, MOE_ROWS), 0).astype(jnp.int32)
    grp = cb >> 6
    blk_a = grp * EXPERTS_PER_GROUP + ((cb >> 3) & 7)
    blk_b = grp * EXPERTS_PER_GROUP + (cb & 7)
    r = jnp.arange(MOE_ROWS, dtype=jnp.int32)[None, :]
    blk_src0 = start[cb] + r0

    never = jnp.int32(n_blocks)
    experts = jnp.arange(N_EXPERTS, dtype=jnp.int32)[:, None]
    used = real[None, :] & ((blk_a[None, :] == experts) | (blk_b[None, :] == experts))
    first_use = jnp.min(jnp.where(used, kidx[None, :], never), axis=1)
    new_a = real & (first_use[blk_a] == kidx)
    new_b = real & (first_use[blk_b] == kidx)
    filler = new_a & new_b
    blk_step = kidx + jnp.cumsum(filler.astype(jnp.int32))
    n_act = nblk + jnp.sum(filler.astype(jnp.int32))

    n_steps = n_blocks + N_EXPERTS // 2 + 1
    step = jnp.arange(n_steps, dtype=jnp.int32)
    k_done = jnp.sum((real[None, :] & (blk_step[None, :] <= step[:, None])).astype(jnp.int32), axis=1) - 1
    kc = jnp.clip(k_done, 0, n_blocks - 1)
    kn = jnp.clip(k_done + 1, 0, n_blocks - 1)
    active = step < n_act
    is_block = active & (k_done >= 0) & (blk_step[kc] == step)
    is_filler = active & ~is_block

    spare = n_all + (step[:, None] & (PAD_BANKS - 1)) * MOE_ROWS + r
    src = jnp.clip(blk_src0[kc][:, None] + r, 0, n_all - 1)
    rows = jnp.where(is_block[:, None] & (r < nv[kc][:, None]), order[src], spare).astype(jnp.int32)
    last = jnp.clip(nblk - 1, 0, n_blocks - 1)
    ea = jnp.where(is_block, blk_a[kc], jnp.where(is_filler, blk_a[kn], blk_a[last]))
    eb = jnp.where(is_block, blk_b[kc], jnp.where(is_filler, blk_a[kn], blk_b[last]))
    staged_here = jnp.where(
        is_block, jnp.where(new_b[kc], blk_b[kc], jnp.where(new_a[kc], blk_a[kc], -1)),
        jnp.where(is_filler, blk_a[kn], -1))
    flag = (staged_here >= 0).astype(jnp.int32)
    far = jnp.int32(n_steps)
    next_evt = lax.cummin(jnp.where(flag > 0, step, far), axis=0, reverse=True)
    last_evt = jnp.max(jnp.where(flag > 0, step, 0))
    stage = staged_here[jnp.where(next_evt < far, next_evt, last_evt)]

    sink = n_all + SINK_BANK * MOE_ROWS + r
    source = n_all + SOURCE_BANK * MOE_ROWS + r
    nxt = jnp.concatenate([rows[2:], rows[-2:]], axis=0)
    prv = jnp.concatenate([rows[:1], rows[:-1]], axis=0)
    gtok = jnp.where((step + 2 < n_act)[:, None], nxt, source)
    stok = jnp.where(((step >= 1) & (step <= n_act))[:, None], prv, sink)
    shape3 = (n_steps, 1, MOE_ROWS)
    rows, gtok, stok = rows * SUBLANES, gtok * SUBLANES, stok * SUBLANES
    chg = jnp.stack([jnp.concatenate([jnp.ones((1,), jnp.int32), (e[1:] != e[:-1]).astype(jnp.int32)])
                     for e in (ea, eb)])
    return (rows[:2].reshape(2, 1, MOE_ROWS), gtok.reshape(shape3), stok.reshape(shape3), ea, eb, chg,
            stage, flag, n_act.reshape(1).astype(jnp.int32))


def _over_sublanes(w):
    return jnp.broadcast_to(w[:, None, :], (w.shape[0], SUBLANES, w.shape[1]))


def kernel(x, meta_tokens, norm_mix_g, w_in, conf_dw_w, conf_dw_b, conf_ln_g, conf_ln_b, sc_conv_w, w_out,
           norm_ffn_g, w_router_group, b_router_group, w_router_expert, b_router_expert, w_exp_gate,
           w_exp_up, w_exp_down, final_norm_g):
    bsz, seq, d = x.shape
    depth = w_in.shape[0]
    n_main = bsz * seq
    n_all = n_main + N_META
    n_blocks = n_all // MOE_ROWS + N_CLASSES
    f32 = jnp.float32

    zero_halo = (jnp.zeros((HALO_A, CONF_W), f32), jnp.zeros((HALO_C, SC_W), f32))
    assert d == SUBLANES * LANES
    h_all = None

    for l in range(depth):
        w_r = jnp.concatenate([w_router_group[l], w_router_expert[l]], axis=1).astype(f32)
        w_r = jnp.pad(w_r, ((0, 0), (0, ROUTER_COLS - w_r.shape[1])))
        w_r = w_r.astype(jnp.bfloat16)
        b_r = jnp.concatenate([b_router_group[l], b_router_expert[l]]).astype(f32)
        b_r = jnp.pad(b_r, (0, ROUTER_COLS - b_r.shape[0])).reshape(1, ROUTER_COLS)
        g_ffn = norm_ffn_g[l].reshape(1, d)
        wts = (norm_mix_g[l].reshape(1, d), w_in[l].astype(jnp.bfloat16), _over_sublanes(conf_dw_w[l]),
               conf_dw_b[l].reshape(1, CONF_W), conf_ln_g[l].reshape(1, CONF_W),
               conf_ln_b[l].reshape(1, CONF_W), _over_sublanes(sc_conv_w[l]), w_out[l].astype(jnp.bfloat16),
               g_ffn, w_r, b_r)

        main = dict(row0=0, n_batch=bsz, n_tiles=seq // MIX_TILE, tt=MIX_TILE)
        if h_all is None:
            h_meta, cls_meta, halo_a, halo_c = _mixer_call(
                meta_tokens.astype(x.dtype), *zero_halo, wts, row0=0, n_batch=1, n_tiles=1, tt=N_META,
                fresh="rows")
            tail = jnp.concatenate([h_meta, jnp.zeros((SPARE_ROWS, d), x.dtype)], axis=0)
            h_all, cls_main, _, _ = _mixer_call(
                x.reshape(n_main, d), halo_a[0], halo_c[0], wts, fresh="tokens", tail=tail, **main)
        else:
            h_all, cls_meta, halo_a, halo_c = _mixer_call(
                h_all, *zero_halo, wts, row0=n_main, n_batch=1, n_tiles=1, tt=N_META)
            h_all, cls_main, _, _ = _mixer_call(h_all, halo_a[0], halo_c[0], wts, **main)

        cls = jnp.concatenate([cls_main[:, 0], cls_meta[:, 0]])
        h_all = _moe_call(h_all, _dispatch_tables(cls, n_blocks), g_ffn, w_r, b_r,
                          w_exp_gate, w_exp_up, w_exp_down, layer=l)

    out = _final_norm_call(h_all, final_norm_g.reshape(1, d), n_main)
    return out.reshape(bsz, seq, d)
```

```python
import functools

import jax
import jax.numpy as jnp
from jax import lax
from jax.experimental import pallas as pl
from jax.experimental.pallas import tpu as pltpu

D_MODEL = 1024
N_META = 16
CONF_W = 512
SC_W = 512
PROJ_W = 2 * CONF_W + 3 * SC_W
CONF_KERNEL = 31
SC_KERNEL = 3
N_GROUPS = 4
EXPERTS_PER_GROUP = 8
N_EXPERTS = N_GROUPS * EXPERTS_PER_GROUP
D_EXPERT = 512
EPS = 1e-6

LANES = 128
SUBLANES = 8
HALO_A = 32
HALO_C = 8
CONV_ROWS = 32
MIX_TILE = 512
NORM_TILE = 2048
MOE_ROWS = 256
N_CLASS_IDS = N_GROUPS * 64
N_CLASSES = N_GROUPS * (EXPERTS_PER_GROUP * (EXPERTS_PER_GROUP - 1) // 2)
VMEM_LIMIT = 56 * 1024 * 1024
ROUTER_COLS = LANES
MOE_BUFFERS = 3
PAD_BANKS = 8
SINK_BANK = PAD_BANKS
SOURCE_BANK = PAD_BANKS + 1
TAIL_TILES = 6
SPARE_ROWS = TAIL_TILES * MIX_TILE - N_META
assert SPARE_ROWS >= (PAD_BANKS + 2) * MOE_ROWS


def _load_tokens(ref, n, first=0):
    return jnp.concatenate(
        [ref[pl.ds(first * SUBLANES + s, n, stride=SUBLANES), :] for s in range(SUBLANES)], axis=-1)


def _store_tokens(ref, x, n):
    for s in range(SUBLANES):
        ref[pl.ds(s, n, stride=SUBLANES), :] = x[:, s * LANES:(s + 1) * LANES]


def _rms(x, g):
    return x * lax.rsqrt(jnp.mean(x * x, axis=-1, keepdims=True) + EPS) * g


def _router_logits(hn_bf16, w_ref, b_ref):
    return jnp.dot(hn_bf16, w_ref[...], preferred_element_type=jnp.float32) + b_ref[...]


N_MIXER_INPUTS = 14


def _mixer_kernel(h_ref, *refs, tt, src_tm, dst_tm, n_batch, has_tail):
    if not has_tail:
        return _mixer_tile(h_ref, *refs, tt=tt, src_tm=src_tm, dst_tm=dst_tm)
    tail_ref, refs = refs[0], refs[1:]
    b = pl.program_id(0)
    pl.when(b < n_batch)(lambda: _mixer_tile(h_ref, *refs, tt=tt, src_tm=src_tm, dst_tm=dst_tm))

    @pl.when(b == n_batch)
    def _():
        hout_ref = refs[N_MIXER_INPUTS - 1]
        _store_tokens(hout_ref, tail_ref[...], tt)


def _mixer_tile(h_ref, halo_a_ref, halo_c_ref, g_mix_ref, w_in_ref, dw_w_ref, dw_b_ref, ln_g_ref,
                ln_b_ref, sc_w_ref, w_out_ref, g_ffn_ref, wr_ref, br_ref,
                hout_ref, cls_ref, halo_a_out, halo_c_out,
                aext, ashift, cext, sb_buf, ybuf, *, tt, src_tm, dst_tm):
    j = pl.program_id(1)

    @pl.when(j == 0)
    def _():
        aext[0:HALO_A, :] = halo_a_ref[...]
        cext[0:HALO_C, :] = halo_c_ref[...]

    h = _load_tokens(h_ref, tt) if src_tm else h_ref[...]
    xn = _rms(h, g_mix_ref[...]).astype(jnp.bfloat16)
    ug = jnp.dot(xn, w_in_ref[:, 0:2 * CONF_W], preferred_element_type=jnp.float32)
    aext[HALO_A:HALO_A + tt, :] = ug[:, 0:CONF_W] * jax.nn.sigmoid(ug[:, CONF_W:2 * CONF_W])
    bcv = jnp.dot(xn, w_in_ref[:, 2 * CONF_W:PROJ_W], preferred_element_type=jnp.float32)
    sb_buf[...] = bcv[:, 0:SC_W]
    cext[HALO_C:HALO_C + tt, :] = bcv[:, SC_W:2 * SC_W] * bcv[:, 2 * SC_W:3 * SC_W]

    n_shift_rows = HALO_A + tt - SUBLANES
    for p in range(1, SUBLANES):
        ashift[p - 1, 0:n_shift_rows, :] = aext[p:p + n_shift_rows, :]

    ch = min(tt, CONV_ROWS)
    tiles = (ch // SUBLANES, SUBLANES, CONF_W)
    for c0 in range(0, tt, ch):
        acc = jnp.broadcast_to(dw_b_ref[...], tiles)
        for k in range(CONF_KERNEL):
            row = HALO_A - (CONF_KERNEL - 1) + c0 + k
            phase = row % SUBLANES
            if phase == 0:
                tap = aext[row:row + ch, :]
            else:
                tap = ashift[phase - 1, row - phase:row - phase + ch, :]
            acc = acc + tap.reshape(tiles) * dw_w_ref[k]
        acc = acc.reshape(ch, CONF_W)
        mu = jnp.mean(acc, axis=-1, keepdims=True)
        xc = acc - mu
        ln = xc * lax.rsqrt(jnp.mean(xc * xc, axis=-1, keepdims=True) + EPS) * ln_g_ref[...] + ln_b_ref[...]
        ybuf[c0:c0 + ch, 0:CONF_W] = jax.nn.silu(ln).astype(jnp.bfloat16)
        sacc = None
        for k in range(SC_KERNEL):
            row = HALO_C - (SC_KERNEL - 1) + c0 + k
            term = cext[row:row + ch, :].reshape(tiles) * sc_w_ref[k]
            sacc = term if sacc is None else sacc + term
        ybuf[c0:c0 + ch, CONF_W:CONF_W + SC_W] = (
            sb_buf[c0:c0 + ch, :] * sacc.reshape(ch, SC_W)).astype(jnp.bfloat16)

    new_halo_a = aext[tt:tt + HALO_A, :]
    new_halo_c = cext[tt:tt + HALO_C, :]
    aext[0:HALO_A, :] = new_halo_a
    cext[0:HALO_C, :] = new_halo_c
    halo_a_out[0] = new_halo_a
    halo_c_out[0] = new_halo_c

    h_new = h + jnp.dot(ybuf[...], w_out_ref[...], preferred_element_type=jnp.float32)
    if dst_tm:
        _store_tokens(hout_ref, h_new, tt)
    else:
        hout_ref[...] = h_new

    logits = _router_logits(_rms(h_new, g_ffn_ref[...]).astype(jnp.bfloat16), wr_ref, br_ref)
    lane = lax.broadcasted_iota(jnp.int32, logits.shape, 1)
    neg = jnp.float32(-jnp.inf)
    big = jnp.int32(1 << 20)
    glog = jnp.where(lane < N_GROUPS, logits, neg)
    gmax = jnp.max(glog, axis=-1, keepdims=True)
    g_idx = jnp.min(jnp.where(glog == gmax, lane, big), axis=-1, keepdims=True)
    el = lane - N_GROUPS
    in_group = (el >= 0) & (el < N_EXPERTS) & ((el >> 3) == g_idx)
    l1 = jnp.where(in_group, logits, neg)
    m1 = jnp.max(l1, axis=-1, keepdims=True)
    i1 = jnp.min(jnp.where(l1 == m1, el, big), axis=-1, keepdims=True)
    l2 = jnp.where(el == i1, neg, l1)
    m2 = jnp.max(l2, axis=-1, keepdims=True)
    i2 = jnp.min(jnp.where(l2 == m2, el, big), axis=-1, keepdims=True)
    lo_e = jnp.minimum(i1, i2) & (EXPERTS_PER_GROUP - 1)
    hi_e = jnp.maximum(i1, i2) & (EXPERTS_PER_GROUP - 1)
    cls_ref[...] = g_idx * 64 + lo_e * 8 + hi_e


def _mixer_call(h_src, halo_a, halo_c, wts, *, row0, n_batch, n_tiles, tt, fresh=None, tail=None):
    blk0 = row0 // tt
    n_rows = n_batch * n_tiles * tt
    last = n_batch * n_tiles - 1
    tail_tiles = 0 if tail is None else tail.shape[0] // tt
    const = lambda shape: pl.BlockSpec(shape, lambda b, j: (0,) * len(shape))
    tile = lambda b, j: jnp.minimum(b * n_tiles + j, last)
    tm_spec = pl.BlockSpec((tt * SUBLANES, LANES), lambda b, j: (blk0 + b * n_tiles + j, 0))
    row_spec = pl.BlockSpec((tt, D_MODEL), lambda b, j: (blk0 + tile(b, j), 0))
    if fresh is None:
        in_spec, out_spec, out_struct = tm_spec, tm_spec, jax.ShapeDtypeStruct(h_src.shape, jnp.float32)
    elif fresh == "rows":
        in_spec, out_spec, out_struct = row_spec, row_spec, jax.ShapeDtypeStruct(h_src.shape, jnp.float32)
    else:
        in_spec = row_spec
        tail_at = lambda b, j: jnp.where(b == n_batch, jnp.minimum(j, tail_tiles - 1), j)
        out_spec = pl.BlockSpec((tt * SUBLANES, LANES), lambda b, j: (b * n_tiles + tail_at(b, j), 0))
        out_struct = jax.ShapeDtypeStruct(((n_rows + tail.shape[0]) * SUBLANES, LANES), jnp.float32)
    tail_specs, tail_args = [], []
    if tail is not None:
        tail_specs = [pl.BlockSpec(
            (tt, D_MODEL), lambda b, j: (jnp.where(b == n_batch, jnp.minimum(j, tail_tiles - 1), 0), 0))]
        tail_args = [tail]
    return pl.pallas_call(
        functools.partial(_mixer_kernel, tt=tt, src_tm=fresh is None, dst_tm=fresh != "rows",
                          n_batch=n_batch, has_tail=tail is not None),
        grid=(n_batch + (tail is not None), n_tiles),
        in_specs=[
            in_spec, *tail_specs,
            const((HALO_A, CONF_W)), const((HALO_C, SC_W)),
            const((1, D_MODEL)), const((D_MODEL, PROJ_W)),
            const((CONF_KERNEL, SUBLANES, CONF_W)), const((1, CONF_W)), const((1, CONF_W)), const((1, CONF_W)),
            const((SC_KERNEL, SUBLANES, SC_W)), const((CONF_W + SC_W, D_MODEL)), const((1, D_MODEL)),
            const((D_MODEL, ROUTER_COLS)), const((1, ROUTER_COLS)),
        ],
        out_specs=[
            out_spec,
            pl.BlockSpec((tt, 1), lambda b, j: (tile(b, j), 0)),
            pl.BlockSpec((1, HALO_A, CONF_W), lambda b, j: (jnp.minimum(b, n_batch - 1), 0, 0)),
            pl.BlockSpec((1, HALO_C, SC_W), lambda b, j: (jnp.minimum(b, n_batch - 1), 0, 0)),
        ],
        out_shape=[
            out_struct,
            jax.ShapeDtypeStruct((n_rows, 1), jnp.int32),
            jax.ShapeDtypeStruct((n_batch, HALO_A, CONF_W), jnp.float32),
            jax.ShapeDtypeStruct((n_batch, HALO_C, SC_W), jnp.float32),
        ],
        scratch_shapes=[
            pltpu.VMEM((HALO_A + tt, CONF_W), jnp.float32),
            pltpu.VMEM((SUBLANES - 1, HALO_A + tt, CONF_W), jnp.float32),
            pltpu.VMEM((HALO_C + tt, SC_W), jnp.float32),
            pltpu.VMEM((tt, SC_W), jnp.float32),
            pltpu.VMEM((tt, CONF_W + SC_W), jnp.bfloat16),
        ],
        input_output_aliases={0: 0} if fresh is None else {},
        compiler_params=pltpu.CompilerParams(
            dimension_semantics=("arbitrary", "arbitrary"), vmem_limit_bytes=VMEM_LIMIT),
        name=f"mixer_t{tt}",
    )(h_src, *tail_args, halo_a, halo_c, *wts)


def _moe_kernel(ea_ref, eb_ref, chg_ref, stage_ref, flag_ref, nblk_ref,
                tok0_ref, gtok_ref, stok_ref, h_in, g_ffn_ref, wr_ref, br_ref,
                wg_st, wu_st, wd_st,
                h_out,
                xbuf0, xbuf1, xbuf2, obuf0, obuf1, obuf2, wg_c, wu_c, wd_c, wga, wua, wda, wgb, wub, wdb,
                gsem, ssem):
    i = pl.program_id(0)
    nblk = nblk_ref[0]
    xbuf = (xbuf0, xbuf1, xbuf2)
    obuf = (obuf0, obuf1, obuf2)
    cur_a = (wga, wua, wda)
    cur_b = (wgb, wub, wdb)

    def token_tile(tok_ref, r, blk=0):
        return pl.ds(pl.multiple_of(tok_ref[blk, 0, r], SUBLANES), SUBLANES)

    def gather_row(tok_ref, r, dst_slot, blk=0):
        return pltpu.make_async_copy(
            h_in.at[token_tile(tok_ref, r, blk), :], xbuf[dst_slot].at[pl.ds(r * SUBLANES, SUBLANES), :],
            gsem.at[dst_slot])

    def scatter_row(r, src_slot):
        return pltpu.make_async_copy(
            obuf[src_slot].at[pl.ds(r * SUBLANES, SUBLANES), :], h_out.at[token_tile(stok_ref, r), :],
            ssem.at[src_slot])

    def wait_gather(dst_slot):
        pltpu.make_async_copy(
            h_in.at[pl.ds(0, MOE_ROWS * SUBLANES), :], xbuf[dst_slot], gsem.at[dst_slot]).wait()

    def wait_scatter(src_slot):
        pltpu.make_async_copy(
            obuf[src_slot], h_out.at[pl.ds(0, MOE_ROWS * SUBLANES), :], ssem.at[src_slot]).wait()

    @pl.when(i == 0)
    def _():
        obuf2[...] = jnp.zeros_like(obuf2)

        def start(r, c):
            gather_row(tok0_ref, r, 0, blk=0).start()
            gather_row(tok0_ref, r, 1, blk=1).start()
            return c

        lax.fori_loop(0, MOE_ROWS, start, 0)

    def step(slot):
        ahead = (slot + 2) % MOE_BUFFERS
        behind = (slot - 1) % MOE_BUFFERS
        ea = ea_ref[i]
        eb = eb_ref[i]
        wait_gather(slot)

        @pl.when(i >= 2)
        def _():
            wait_scatter(slot)

        @pl.when(flag_ref[i] != 0)
        def _():
            dst = stage_ref[i] & (EXPERTS_PER_GROUP - 1)
            wg_c[dst] = wg_st[0, 0].astype(jnp.bfloat16)
            wu_c[dst] = wu_st[0, 0].astype(jnp.bfloat16)
            wd_c[dst] = wd_st[0, 0].astype(jnp.bfloat16)

        for e, changed, cur in ((ea, chg_ref[0, i], cur_a), (eb, chg_ref[1, i], cur_b)):
            @pl.when(changed != 0)
            def _(e=e, cur=cur):
                at = e & (EXPERTS_PER_GROUP - 1)
                cur[0][...] = wg_c[at]
                cur[1][...] = wu_c[at]
                cur[2][...] = wd_c[at]

        for r in range(MOE_ROWS):
            gather_row(gtok_ref, r, ahead).start()
        for r in range(MOE_ROWS):
            scatter_row(r, behind).start()

        x = _load_tokens(xbuf[slot], MOE_ROWS)
        xb = _rms(x, g_ffn_ref[...]).astype(jnp.bfloat16)
        logits = _router_logits(xb, wr_ref, br_ref)
        lane = lax.broadcasted_iota(jnp.int32, logits.shape, 1)
        neg = jnp.float32(-jnp.inf)
        glog = jnp.where(lane < N_GROUPS, logits, neg)
        gmax = jnp.max(glog, axis=-1, keepdims=True)
        gsum = jnp.sum(jnp.exp(glog - gmax), axis=-1, keepdims=True)
        grp = ea >> 3
        l_g = jnp.sum(jnp.where(lane == grp, logits, 0.0), axis=-1, keepdims=True)
        p_g = jnp.exp(l_g - gmax) / gsum
        l_a = jnp.sum(jnp.where(lane == N_GROUPS + ea, logits, 0.0), axis=-1, keepdims=True)
        l_b = jnp.sum(jnp.where(lane == N_GROUPS + eb, logits, 0.0), axis=-1, keepdims=True)
        w_a = p_g / (1.0 + jnp.exp(l_b - l_a))
        w_b = p_g / (1.0 + jnp.exp(l_a - l_b))

        def expert(cur):
            gate = jnp.dot(xb, cur[0][...], preferred_element_type=jnp.float32)
            up = jnp.dot(xb, cur[1][...], preferred_element_type=jnp.float32)
            hid = (jax.nn.silu(gate) * up).astype(jnp.bfloat16)
            return jnp.dot(hid, cur[2][...], preferred_element_type=jnp.float32)

        y = w_a * expert(cur_a) + w_b * expert(cur_b)
        _store_tokens(obuf[slot], x + y, MOE_ROWS)

        @pl.when(i == nblk)
        def _():
            wait_gather((slot + 1) % MOE_BUFFERS)
            wait_gather(ahead)
            wait_scatter((slot - 2) % MOE_BUFFERS)
            wait_scatter(behind)

    rotation = lax.rem(i, MOE_BUFFERS)
    for slot in range(MOE_BUFFERS):
        pl.when((i <= nblk) & (rotation == slot))(functools.partial(step, slot))


def _moe_call(h_all, tables, g_ffn, wr, br, w_gate, w_up, w_down, *, layer):
    tok0, gtok, stok, ea, eb, chg, stage, flag, nblk = tables
    const = lambda shape: pl.BlockSpec(shape, lambda i, *_: (0,) * len(shape))
    staged = lambda shape: pl.BlockSpec(
        (1, 1) + shape, lambda i, ea_, eb_, ch_, st_, fl_, nb_: (layer, st_[i], 0, 0))
    tok_spec = lambda imap: pl.BlockSpec((1, 1, MOE_ROWS), imap, memory_space=pltpu.SMEM)
    grid_spec = pltpu.PrefetchScalarGridSpec(
        num_scalar_prefetch=6,
        grid=(nblk[0] + 1,),
        in_specs=[
            pl.BlockSpec((2, 1, MOE_ROWS), lambda i, *_: (0, 0, 0), memory_space=pltpu.SMEM),
            tok_spec(lambda i, *_: (i, 0, 0)), tok_spec(lambda i, *_: (i, 0, 0)),
            pl.BlockSpec(memory_space=pl.ANY),
            const((1, D_MODEL)), const((D_MODEL, ROUTER_COLS)), const((1, ROUTER_COLS)),
            staged((D_MODEL, D_EXPERT)), staged((D_MODEL, D_EXPERT)), staged((D_EXPERT, D_MODEL)),
        ],
        out_specs=pl.BlockSpec(memory_space=pl.ANY),
        scratch_shapes=[
            *[pltpu.VMEM((MOE_ROWS * SUBLANES, LANES), jnp.float32)] * (2 * MOE_BUFFERS),
            pltpu.VMEM((EXPERTS_PER_GROUP, D_MODEL, D_EXPERT), jnp.bfloat16),
            pltpu.VMEM((EXPERTS_PER_GROUP, D_MODEL, D_EXPERT), jnp.bfloat16),
            pltpu.VMEM((EXPERTS_PER_GROUP, D_EXPERT, D_MODEL), jnp.bfloat16),
            pltpu.VMEM((D_MODEL, D_EXPERT), jnp.bfloat16),
            pltpu.VMEM((D_MODEL, D_EXPERT), jnp.bfloat16),
            pltpu.VMEM((D_EXPERT, D_MODEL), jnp.bfloat16),
            pltpu.VMEM((D_MODEL, D_EXPERT), jnp.bfloat16),
            pltpu.VMEM((D_MODEL, D_EXPERT), jnp.bfloat16),
            pltpu.VMEM((D_EXPERT, D_MODEL), jnp.bfloat16),
            pltpu.SemaphoreType.DMA((MOE_BUFFERS,)),
            pltpu.SemaphoreType.DMA((MOE_BUFFERS,)),
        ],
    )
    return pl.pallas_call(
        _moe_kernel,
        grid_spec=grid_spec,
        out_shape=jax.ShapeDtypeStruct(h_all.shape, h_all.dtype),
        input_output_aliases={9: 0},
        compiler_params=pltpu.CompilerParams(
            dimension_semantics=("arbitrary",), vmem_limit_bytes=VMEM_LIMIT),
        name="moe_pairs",
    )(ea, eb, chg, stage, flag, nblk, tok0, gtok, stok, h_all, g_ffn, wr, br, w_gate, w_up, w_down)


def _final_norm_kernel(h_ref, g_ref, o_ref):
    for first in range(0, NORM_TILE, MIX_TILE):
        o_ref[first:first + MIX_TILE, :] = _rms(_load_tokens(h_ref, MIX_TILE, first), g_ref[...])


def _final_norm_call(h_all, g, n_rows):
    return pl.pallas_call(
        _final_norm_kernel,
        grid=(n_rows // NORM_TILE,),
        in_specs=[pl.BlockSpec((NORM_TILE * SUBLANES, LANES), lambda i: (i, 0)),
                  pl.BlockSpec((1, D_MODEL), lambda i: (0, 0))],
        out_specs=pl.BlockSpec((NORM_TILE, D_MODEL), lambda i: (i, 0)),
        out_shape=jax.ShapeDtypeStruct((n_rows, D_MODEL), jnp.float32),
        compiler_params=pltpu.CompilerParams(dimension_semantics=("arbitrary",), vmem_limit_bytes=VMEM_LIMIT),
        name="final_norm",
    )(h_all, g)


def _pick(tables, idx):
    hit = idx[None, :, None] == jnp.arange(tables.shape[1], dtype=jnp.int32)[None, None, :]
    return jnp.sum(jnp.where(hit, tables[:, None, :], 0), axis=-1)


def _dispatch_tables(cls, n_blocks):
    n_all = cls.shape[0]
    tok_bits = (n_all - 1).bit_length()
    key = jnp.sort((cls << tok_bits) | jnp.arange(n_all, dtype=jnp.int32))
    order = key & ((1 << tok_bits) - 1)
    ids = jnp.arange(N_CLASS_IDS, dtype=jnp.int32)
    counts = jnp.sum((cls[:, None] == ids[None, :]).astype(jnp.int32), axis=0)
    start = jnp.cumsum(counts) - counts
    padded = (counts + MOE_ROWS - 1) // MOE_ROWS * MOE_ROWS
    pad_end = jnp.cumsum(padded)
    pad_start = pad_end - padded
    nblk = (pad_end[-1] // MOE_ROWS).astype(jnp.int32)

    kidx = jnp.arange(n_blocks, dtype=jnp.int32)
    real = kidx < nblk
    p0 = kidx * MOE_ROWS
    cb = jnp.sum((pad_end[None, :] <= p0[:, None]).astype(jnp.int32), axis=1)
    cb = jnp.minimum(cb, N_CLASS_IDS - 1)
    cls_pad_start, cls_count, cls_start = _pick(jnp.stack([pad_start, counts, start]), cb)
    r0 = p0 - cls_pad_start
    nv = jnp.where(real, jnp.clip(cls_count - r0, 0, MOE_ROWS), 0).astype(jnp.int32)
    grp = cb >> 6
    blk_a = grp * EXPERTS_PER_GROUP + ((cb >> 3) & 7)
    blk_b = grp * EXPERTS_PER_GROUP + (cb & 7)
    r = jnp.arange(MOE_ROWS, dtype=jnp.int32)[None, :]
    blk_src0 = cls_start + r0

    never = jnp.int32(n_blocks)
    experts = jnp.arange(N_EXPERTS, dtype=jnp.int32)[:, None]
    used = real[None, :] & ((blk_a[None, :] == experts) | (blk_b[None, :] == experts))
    first_use = jnp.min(jnp.where(used, kidx[None, :], never), axis=1)
    new_a = real & (_pick(first_use[None, :], blk_a)[0] == kidx)
    new_b = real & (_pick(first_use[None, :], blk_b)[0] == kidx)
    filler = new_a & new_b
    blk_step = kidx + jnp.cumsum(filler.astype(jnp.int32))
    n_act = nblk + jnp.sum(filler.astype(jnp.int32))

    n_steps = n_blocks + N_EXPERTS // 2 + 1
    step = jnp.arange(n_steps, dtype=jnp.int32)
    k_done = jnp.sum((real[None, :] & (blk_step[None, :] <= step[:, None])).astype(jnp.int32), axis=1) - 1
    kc = jnp.clip(k_done, 0, n_blocks - 1)
    kn = jnp.clip(k_done + 1, 0, n_blocks - 1)
    active = step < n_act
    per_block = jnp.stack([blk_step, nv, blk_src0, blk_a, blk_b, new_a.astype(jnp.int32), new_b.astype(jnp.int32)])
    c_step, c_nv, c_src0, c_a, c_b, c_new_a, c_new_b = _pick(per_block, kc)
    n_a = _pick(blk_a[None, :], kn)[0]
    is_block = active & (k_done >= 0) & (c_step == step)
    is_filler = active & ~is_block

    spare = n_all + (step[:, None] & (PAD_BANKS - 1)) * MOE_ROWS + r
    src = jnp.clip(c_src0[:, None] + r, 0, n_all - 1)
    rows = jnp.where(is_block[:, None] & (r < c_nv[:, None]), order[src], spare).astype(jnp.int32)
    last = jnp.clip(nblk - 1, 0, n_blocks - 1)
    ea = jnp.where(is_block, c_a, jnp.where(is_filler, n_a, blk_a[last]))
    eb = jnp.where(is_block, c_b, jnp.where(is_filler, n_a, blk_b[last]))
    staged_here = jnp.where(
        is_block, jnp.where(c_new_b > 0, c_b, jnp.where(c_new_a > 0, c_a, -1)),
        jnp.where(is_filler, n_a, -1))
    flag = (staged_here >= 0).astype(jnp.int32)
    far = jnp.int32(n_steps)
    next_evt = lax.cummin(jnp.where(flag > 0, step, far), axis=0, reverse=True)
    last_evt = jnp.max(jnp.where(flag > 0, step, 0))
    stage = _pick(staged_here[None, :], jnp.where(next_evt < far, next_evt, last_evt))[0]

    sink = n_all + SINK_BANK * MOE_ROWS + r
    source = n_all + SOURCE_BANK * MOE_ROWS + r
    nxt = jnp.concatenate([rows[2:], rows[-2:]], axis=0)
    prv = jnp.concatenate([rows[:1], rows[:-1]], axis=0)
    gtok = jnp.where((step + 2 < n_act)[:, None], nxt, source)
    stok = jnp.where(((step >= 1) & (step <= n_act))[:, None], prv, sink)
    shape3 = (n_steps, 1, MOE_ROWS)
    rows, gtok, stok = rows * SUBLANES, gtok * SUBLANES, stok * SUBLANES
    chg = jnp.stack([jnp.concatenate([jnp.ones((1,), jnp.int32), (e[1:] != e[:-1]).astype(jnp.int32)])
                     for e in (ea, eb)])
    return (rows[:2].reshape(2, 1, MOE_ROWS), gtok.reshape(shape3), stok.reshape(shape3), ea, eb, chg,
            stage, flag, n_act.reshape(1).astype(jnp.int32))


def _over_sublanes(w):
    return jnp.broadcast_to(w[:, None, :], (w.shape[0], SUBLANES, w.shape[1]))


def kernel(x, meta_tokens, norm_mix_g, w_in, conf_dw_w, conf_dw_b, conf_ln_g, conf_ln_b, sc_conv_w, w_out,
           norm_ffn_g, w_router_group, b_router_group, w_router_expert, b_router_expert, w_exp_gate,
           w_exp_up, w_exp_down, final_norm_g):
    bsz, seq, d = x.shape
    depth = w_in.shape[0]
    n_main = bsz * seq
    n_all = n_main + N_META
    n_blocks = n_all // MOE_ROWS + N_CLASSES
    f32 = jnp.float32

    zero_halo = (jnp.zeros((HALO_A, CONF_W), f32), jnp.zeros((HALO_C, SC_W), f32))
    assert d == SUBLANES * LANES
    h_all = None

    for l in range(depth):
        w_r = jnp.concatenate([w_router_group[l], w_router_expert[l]], axis=1).astype(f32)
        w_r = jnp.pad(w_r, ((0, 0), (0, ROUTER_COLS - w_r.shape[1])))
        w_r = w_r.astype(jnp.bfloat16)
        b_r = jnp.concatenate([b_router_group[l], b_router_expert[l]]).astype(f32)
        b_r = jnp.pad(b_r, (0, ROUTER_COLS - b_r.shape[0])).reshape(1, ROUTER_COLS)
        g_ffn = norm_ffn_g[l].reshape(1, d)
        wts = (norm_mix_g[l].reshape(1, d), w_in[l].astype(jnp.bfloat16), _over_sublanes(conf_dw_w[l]),
               conf_dw_b[l].reshape(1, CONF_W), conf_ln_g[l].reshape(1, CONF_W),
               conf_ln_b[l].reshape(1, CONF_W), _over_sublanes(sc_conv_w[l]), w_out[l].astype(jnp.bfloat16),
               g_ffn, w_r, b_r)

        main = dict(row0=0, n_batch=bsz, n_tiles=seq // MIX_TILE, tt=MIX_TILE)
        if h_all is None:
            h_meta, cls_meta, halo_a, halo_c = _mixer_call(
                meta_tokens.astype(x.dtype), *zero_halo, wts, row0=0, n_batch=1, n_tiles=1, tt=N_META,
                fresh="rows")
            tail = jnp.concatenate([h_meta, jnp.zeros((SPARE_ROWS, d), x.dtype)], axis=0)
            h_all, cls_main, _, _ = _mixer_call(
                x.reshape(n_main, d), halo_a[0], halo_c[0], wts, fresh="tokens", tail=tail, **main)
        else:
            h_all, cls_meta, halo_a, halo_c = _mixer_call(
                h_all, *zero_halo, wts, row0=n_main, n_batch=1, n_tiles=1, tt=N_META)
            h_all, cls_main, _, _ = _mixer_call(h_all, halo_a[0], halo_c[0], wts, **main)

        cls = jnp.concatenate([cls_main[:, 0], cls_meta[:, 0]])
        h_all = _moe_call(h_all, _dispatch_tables(cls, n_blocks), g_ffn, w_r, b_r,
                          w_exp_gate, w_exp_up, w_exp_down, layer=l)

    out = _final_norm_call(h_all, final_norm_g.reshape(1, d), n_main)
    return out.reshape(bsz, seq, d)
```

```python
import functools

import jax
import jax.numpy as jnp
from jax import lax
from jax.experimental import pallas as pl
from jax.experimental.pallas import tpu as pltpu

D_MODEL = 1024
N_META = 16
CONF_W = 512
SC_W = 512
PROJ_W = 2 * CONF_W + 3 * SC_W
CONF_KERNEL = 31
SC_KERNEL = 3
N_GROUPS = 4
EXPERTS_PER_GROUP = 8
N_EXPERTS = N_GROUPS * EXPERTS_PER_GROUP
D_EXPERT = 512
EPS = 1e-6

LANES = 128
SUBLANES = 8
HALO_A = 32
HALO_C = 8
CONV_ROWS = 32
MIX_TILE = 512
NORM_TILE = 2048
MOE_ROWS = 256
N_CLASS_IDS = N_GROUPS * 64
N_CLASSES = N_GROUPS * (EXPERTS_PER_GROUP * (EXPERTS_PER_GROUP - 1) // 2)
VMEM_LIMIT = 56 * 1024 * 1024
ROUTER_COLS = LANES
MOE_BUFFERS = 3
PAD_BANKS = 8
SINK_BANK = PAD_BANKS
SOURCE_BANK = PAD_BANKS + 1
TAIL_TILES = 6
SPARE_ROWS = TAIL_TILES * MIX_TILE - N_META
assert SPARE_ROWS >= (PAD_BANKS + 2) * MOE_ROWS


def _load_tokens(ref, n, first=0):
    return jnp.concatenate(
        [ref[pl.ds(first * SUBLANES + s, n, stride=SUBLANES), :] for s in range(SUBLANES)], axis=-1)


def _store_tokens(ref, x, n):
    for s in range(SUBLANES):
        ref[pl.ds(s, n, stride=SUBLANES), :] = x[:, s * LANES:(s + 1) * LANES]


def _rms(x, g):
    return x * lax.rsqrt(jnp.mean(x * x, axis=-1, keepdims=True) + EPS) * g


def _router_logits(hn_bf16, w_ref, b_ref):
    return jnp.dot(hn_bf16, w_ref[...], preferred_element_type=jnp.float32) + b_ref[...]


N_MIXER_INPUTS = 14


def _mixer_kernel(h_ref, *refs, tt, src_tm, dst_tm, n_batch, has_tail):
    if not has_tail:
        return _mixer_tile(h_ref, *refs, tt=tt, src_tm=src_tm, dst_tm=dst_tm)
    tail_ref, refs = refs[0], refs[1:]
    b = pl.program_id(0)
    pl.when(b < n_batch)(lambda: _mixer_tile(h_ref, *refs, tt=tt, src_tm=src_tm, dst_tm=dst_tm))

    @pl.when(b == n_batch)
    def _():
        hout_ref = refs[N_MIXER_INPUTS - 1]
        _store_tokens(hout_ref, tail_ref[...], tt)


def _mixer_tile(h_ref, halo_a_ref, halo_c_ref, g_mix_ref, w_in_ref, dw_w_ref, dw_b_ref, ln_g_ref,
                ln_b_ref, sc_w_ref, w_out_ref, g_ffn_ref, wr_ref, br_ref,
                hout_ref, cls_ref, halo_a_out, halo_c_out,
                aext, ashift, cext, sb_buf, ybuf, *, tt, src_tm, dst_tm):
    j = pl.program_id(1)

    @pl.when(j == 0)
    def _():
        aext[0:HALO_A, :] = halo_a_ref[...]
        cext[0:HALO_C, :] = halo_c_ref[...]

    h = _load_tokens(h_ref, tt) if src_tm else h_ref[...]
    xn = _rms(h, g_mix_ref[...]).astype(jnp.bfloat16)
    ug = jnp.dot(xn, w_in_ref[:, 0:2 * CONF_W], preferred_element_type=jnp.float32)
    aext[HALO_A:HALO_A + tt, :] = ug[:, 0:CONF_W] * jax.nn.sigmoid(ug[:, CONF_W:2 * CONF_W])
    bcv = jnp.dot(xn, w_in_ref[:, 2 * CONF_W:PROJ_W], preferred_element_type=jnp.float32)
    sb_buf[...] = bcv[:, 0:SC_W]
    cext[HALO_C:HALO_C + tt, :] = bcv[:, SC_W:2 * SC_W] * bcv[:, 2 * SC_W:3 * SC_W]

    n_shift_rows = HALO_A + tt - SUBLANES
    for p in range(1, SUBLANES):
        ashift[p - 1, 0:n_shift_rows, :] = aext[p:p + n_shift_rows, :]

    ch = min(tt, CONV_ROWS)
    tiles = (ch // SUBLANES, SUBLANES, CONF_W)
    for c0 in range(0, tt, ch):
        acc = jnp.broadcast_to(dw_b_ref[...], tiles)
        for k in range(CONF_KERNEL):
            row = HALO_A - (CONF_KERNEL - 1) + c0 + k
            phase = row % SUBLANES
            if phase == 0:
                tap = aext[row:row + ch, :]
            else:
                tap = ashift[phase - 1, row - phase:row - phase + ch, :]
            acc = acc + tap.reshape(tiles) * dw_w_ref[k]
        acc = acc.reshape(ch, CONF_W)
        mu = jnp.mean(acc, axis=-1, keepdims=True)
        xc = acc - mu
        ln = xc * lax.rsqrt(jnp.mean(xc * xc, axis=-1, keepdims=True) + EPS) * ln_g_ref[...] + ln_b_ref[...]
        ybuf[c0:c0 + ch, 0:CONF_W] = jax.nn.silu(ln).astype(jnp.bfloat16)
        sacc = None
        for k in range(SC_KERNEL):
            row = HALO_C - (SC_KERNEL - 1) + c0 + k
            term = cext[row:row + ch, :].reshape(tiles) * sc_w_ref[k]
            sacc = term if sacc is None else sacc + term
        ybuf[c0:c0 + ch, CONF_W:CONF_W + SC_W] = (
            sb_buf[c0:c0 + ch, :] * sacc.reshape(ch, SC_W)).astype(jnp.bfloat16)

    new_halo_a = aext[tt:tt + HALO_A, :]
    new_halo_c = cext[tt:tt + HALO_C, :]
    aext[0:HALO_A, :] = new_halo_a
    cext[0:HALO_C, :] = new_halo_c
    halo_a_out[0] = new_halo_a
    halo_c_out[0] = new_halo_c

    h_new = h + jnp.dot(ybuf[...], w_out_ref[...], preferred_element_type=jnp.float32)
    if dst_tm:
        _store_tokens(hout_ref, h_new, tt)
    else:
        hout_ref[...] = h_new

    logits = _router_logits(_rms(h_new, g_ffn_ref[...]).astype(jnp.bfloat16), wr_ref, br_ref)
    lane = lax.broadcasted_iota(jnp.int32, logits.shape, 1)
    neg = jnp.float32(-jnp.inf)
    big = jnp.int32(1 << 20)
    glog = jnp.where(lane < N_GROUPS, logits, neg)
    gmax = jnp.max(glog, axis=-1, keepdims=True)
    g_idx = jnp.min(jnp.where(glog == gmax, lane, big), axis=-1, keepdims=True)
    el = lane - N_GROUPS
    in_group = (el >= 0) & (el < N_EXPERTS) & ((el >> 3) == g_idx)
    l1 = jnp.where(in_group, logits, neg)
    m1 = jnp.max(l1, axis=-1, keepdims=True)
    i1 = jnp.min(jnp.where(l1 == m1, el, big), axis=-1, keepdims=True)
    l2 = jnp.where(el == i1, neg, l1)
    m2 = jnp.max(l2, axis=-1, keepdims=True)
    i2 = jnp.min(jnp.where(l2 == m2, el, big), axis=-1, keepdims=True)
    lo_e = jnp.minimum(i1, i2) & (EXPERTS_PER_GROUP - 1)
    hi_e = jnp.maximum(i1, i2) & (EXPERTS_PER_GROUP - 1)
    cls_ref[...] = g_idx * 64 + lo_e * 8 + hi_e


def _mixer_call(h_src, halo_a, halo_c, wts, *, row0, n_batch, n_tiles, tt, fresh=None, tail=None):
    blk0 = row0 // tt
    n_rows = n_batch * n_tiles * tt
    last = n_batch * n_tiles - 1
    tail_tiles = 0 if tail is None else tail.shape[0] // tt
    const = lambda shape: pl.BlockSpec(shape, lambda b, j: (0,) * len(shape))
    tile = lambda b, j: jnp.minimum(b * n_tiles + j, last)
    tm_spec = pl.BlockSpec((tt * SUBLANES, LANES), lambda b, j: (blk0 + b * n_tiles + j, 0))
    row_spec = pl.BlockSpec((tt, D_MODEL), lambda b, j: (blk0 + tile(b, j), 0))
    if fresh is None:
        in_spec, out_spec, out_struct = tm_spec, tm_spec, jax.ShapeDtypeStruct(h_src.shape, jnp.float32)
    elif fresh == "rows":
        in_spec, out_spec, out_struct = row_spec, row_spec, jax.ShapeDtypeStruct(h_src.shape, jnp.float32)
    else:
        in_spec = row_spec
        tail_at = lambda b, j: jnp.where(b == n_batch, jnp.minimum(j, tail_tiles - 1), j)
        out_spec = pl.BlockSpec((tt * SUBLANES, LANES), lambda b, j: (b * n_tiles + tail_at(b, j), 0))
        out_struct = jax.ShapeDtypeStruct(((n_rows + tail.shape[0]) * SUBLANES, LANES), jnp.float32)
    tail_specs, tail_args = [], []
    if tail is not None:
        tail_specs = [pl.BlockSpec(
            (tt, D_MODEL), lambda b, j: (jnp.where(b == n_batch, jnp.minimum(j, tail_tiles - 1), 0), 0))]
        tail_args = [tail]
    return pl.pallas_call(
        functools.partial(_mixer_kernel, tt=tt, src_tm=fresh is None, dst_tm=fresh != "rows",
                          n_batch=n_batch, has_tail=tail is not None),
        grid=(n_batch + (tail is not None), n_tiles),
        in_specs=[
            in_spec, *tail_specs,
            const((HALO_A, CONF_W)), const((HALO_C, SC_W)),
            const((1, D_MODEL)), const((D_MODEL, PROJ_W)),
            const((CONF_KERNEL, SUBLANES, CONF_W)), const((1, CONF_W)), const((1, CONF_W)), const((1, CONF_W)),
            const((SC_KERNEL, SUBLANES, SC_W)), const((CONF_W + SC_W, D_MODEL)), const((1, D_MODEL)),
            const((D_MODEL, ROUTER_COLS)), const((1, ROUTER_COLS)),
        ],
        out_specs=[
            out_spec,
            pl.BlockSpec((tt, 1), lambda b, j: (tile(b, j), 0)),
            pl.BlockSpec((1, HALO_A, CONF_W), lambda b, j: (jnp.minimum(b, n_batch - 1), 0, 0)),
            pl.BlockSpec((1, HALO_C, SC_W), lambda b, j: (jnp.minimum(b, n_batch - 1), 0, 0)),
        ],
        out_shape=[
            out_struct,
            jax.ShapeDtypeStruct((n_rows, 1), jnp.int32),
            jax.ShapeDtypeStruct((n_batch, HALO_A, CONF_W), jnp.float32),
            jax.ShapeDtypeStruct((n_batch, HALO_C, SC_W), jnp.float32),
        ],
        scratch_shapes=[
            pltpu.VMEM((HALO_A + tt, CONF_W), jnp.float32),
            pltpu.VMEM((SUBLANES - 1, HALO_A + tt, CONF_W), jnp.float32),
            pltpu.VMEM((HALO_C + tt, SC_W), jnp.float32),
            pltpu.VMEM((tt, SC_W), jnp.float32),
            pltpu.VMEM((tt, CONF_W + SC_W), jnp.bfloat16),
        ],
        input_output_aliases={0: 0} if fresh is None else {},
        compiler_params=pltpu.CompilerParams(
            dimension_semantics=("arbitrary", "arbitrary"), vmem_limit_bytes=VMEM_LIMIT),
        name=f"mixer_t{tt}",
    )(h_src, *tail_args, halo_a, halo_c, *wts)


def _moe_kernel(ea_ref, eb_ref, chg_ref, stage_ref, flag_ref, nblk_ref,
                tok0_ref, gtok_ref, stok_ref, h_in, g_ffn_ref, wr_ref, br_ref,
                wg_st, wu_st, wd_st,
                h_out,
                xbuf0, xbuf1, xbuf2, obuf0, obuf1, obuf2, wg_c, wu_c, wd_c, wga, wua, wda, wgb, wub, wdb,
                gsem, ssem):
    i = pl.program_id(0)
    nblk = nblk_ref[0]
    xbuf = (xbuf0, xbuf1, xbuf2)
    obuf = (obuf0, obuf1, obuf2)
    cur_a = (wga, wua, wda)
    cur_b = (wgb, wub, wdb)

    def token_tile(tok_ref, r, blk=0):
        return pl.ds(pl.multiple_of(tok_ref[blk, 0, r], SUBLANES), SUBLANES)

    def gather_row(tok_ref, r, dst_slot, blk=0):
        return pltpu.make_async_copy(
            h_in.at[token_tile(tok_ref, r, blk), :], xbuf[dst_slot].at[pl.ds(r * SUBLANES, SUBLANES), :],
            gsem.at[dst_slot])

    def scatter_row(r, src_slot):
        return pltpu.make_async_copy(
            obuf[src_slot].at[pl.ds(r * SUBLANES, SUBLANES), :], h_out.at[token_tile(stok_ref, r), :],
            ssem.at[src_slot])

    def wait_gather(dst_slot):
        pltpu.make_async_copy(
            h_in.at[pl.ds(0, MOE_ROWS * SUBLANES), :], xbuf[dst_slot], gsem.at[dst_slot]).wait()

    def wait_scatter(src_slot):
        pltpu.make_async_copy(
            obuf[src_slot], h_out.at[pl.ds(0, MOE_ROWS * SUBLANES), :], ssem.at[src_slot]).wait()

    @pl.when(i == 0)
    def _():
        obuf2[...] = jnp.zeros_like(obuf2)

        def start(r, c):
            gather_row(tok0_ref, r, 0, blk=0).start()
            gather_row(tok0_ref, r, 1, blk=1).start()
            return c

        lax.fori_loop(0, MOE_ROWS, start, 0)

    def step(slot):
        ahead = (slot + 2) % MOE_BUFFERS
        behind = (slot - 1) % MOE_BUFFERS
        ea = ea_ref[i]
        eb = eb_ref[i]
        wait_gather(slot)

        @pl.when(i >= 2)
        def _():
            wait_scatter(slot)

        @pl.when(flag_ref[i] != 0)
        def _():
            dst = stage_ref[i] & (EXPERTS_PER_GROUP - 1)
            wg_c[dst] = wg_st[0, 0].astype(jnp.bfloat16)
            wu_c[dst] = wu_st[0, 0].astype(jnp.bfloat16)
            wd_c[dst] = wd_st[0, 0].astype(jnp.bfloat16)

        for e, changed, cur in ((ea, chg_ref[0, i], cur_a), (eb, chg_ref[1, i], cur_b)):
            @pl.when(changed != 0)
            def _(e=e, cur=cur):
                at = e & (EXPERTS_PER_GROUP - 1)
                cur[0][...] = wg_c[at]
                cur[1][...] = wu_c[at]
                cur[2][...] = wd_c[at]

        for r in range(MOE_ROWS):
            gather_row(gtok_ref, r, ahead).start()
        for r in range(MOE_ROWS):
            scatter_row(r, behind).start()

        x = _load_tokens(xbuf[slot], MOE_ROWS)
        xb = _rms(x, g_ffn_ref[...]).astype(jnp.bfloat16)
        logits = _router_logits(xb, wr_ref, br_ref)
        lane = lax.broadcasted_iota(jnp.int32, logits.shape, 1)
        neg = jnp.float32(-jnp.inf)
        glog = jnp.where(lane < N_GROUPS, logits, neg)
        gmax = jnp.max(glog, axis=-1, keepdims=True)
        gsum = jnp.sum(jnp.exp(glog - gmax), axis=-1, keepdims=True)
        grp = ea >> 3
        l_g = jnp.sum(jnp.where(lane == grp, logits, 0.0), axis=-1, keepdims=True)
        p_g = jnp.exp(l_g - gmax) / gsum
        l_a = jnp.sum(jnp.where(lane == N_GROUPS + ea, logits, 0.0), axis=-1, keepdims=True)
        l_b = jnp.sum(jnp.where(lane == N_GROUPS + eb, logits, 0.0), axis=-1, keepdims=True)
        w_a = p_g / (1.0 + jnp.exp(l_b - l_a))
        w_b = p_g / (1.0 + jnp.exp(l_a - l_b))

        def expert(cur):
            gate = jnp.dot(xb, cur[0][...], preferred_element_type=jnp.float32)
            up = jnp.dot(xb, cur[1][...], preferred_element_type=jnp.float32)
            hid = (jax.nn.silu(gate) * up).astype(jnp.bfloat16)
            return jnp.dot(hid, cur[2][...], preferred_element_type=jnp.float32)

        y = w_a * expert(cur_a) + w_b * expert(cur_b)
        _store_tokens(obuf[slot], x + y, MOE_ROWS)

        @pl.when(i == nblk)
        def _():
            wait_gather((slot + 1) % MOE_BUFFERS)
            wait_gather(ahead)
            wait_scatter((slot - 2) % MOE_BUFFERS)
            wait_scatter(behind)

    rotation = lax.rem(i, MOE_BUFFERS)
    for slot in range(MOE_BUFFERS):
        pl.when((i <= nblk) & (rotation == slot))(functools.partial(step, slot))


def _moe_call(h_all, tables, g_ffn, wr, br, w_gate, w_up, w_down, *, layer):
    tok0, gtok, stok, ea, eb, chg, stage, flag, nblk = tables
    const = lambda shape: pl.BlockSpec(shape, lambda i, *_: (0,) * len(shape))
    staged = lambda shape: pl.BlockSpec(
        (1, 1) + shape, lambda i, ea_, eb_, ch_, st_, fl_, nb_: (layer, st_[i], 0, 0))
    tok_spec = lambda imap: pl.BlockSpec((1, 1, MOE_ROWS), imap, memory_space=pltpu.SMEM)
    grid_spec = pltpu.PrefetchScalarGridSpec(
        num_scalar_prefetch=6,
        grid=(nblk[0] + 1,),
        in_specs=[
            pl.BlockSpec((2, 1, MOE_ROWS), lambda i, *_: (0, 0, 0), memory_space=pltpu.SMEM),
            tok_spec(lambda i, *_: (i, 0, 0)), tok_spec(lambda i, *_: (i, 0, 0)),
            pl.BlockSpec(memory_space=pl.ANY),
            const((1, D_MODEL)), const((D_MODEL, ROUTER_COLS)), const((1, ROUTER_COLS)),
            staged((D_MODEL, D_EXPERT)), staged((D_MODEL, D_EXPERT)), staged((D_EXPERT, D_MODEL)),
        ],
        out_specs=pl.BlockSpec(memory_space=pl.ANY),
        scratch_shapes=[
            *[pltpu.VMEM((MOE_ROWS * SUBLANES, LANES), jnp.float32)] * (2 * MOE_BUFFERS),
            pltpu.VMEM((EXPERTS_PER_GROUP, D_MODEL, D_EXPERT), jnp.bfloat16),
            pltpu.VMEM((EXPERTS_PER_GROUP, D_MODEL, D_EXPERT), jnp.bfloat16),
            pltpu.VMEM((EXPERTS_PER_GROUP, D_EXPERT, D_MODEL), jnp.bfloat16),
            pltpu.VMEM((D_MODEL, D_EXPERT), jnp.bfloat16),
            pltpu.VMEM((D_MODEL, D_EXPERT), jnp.bfloat16),
            pltpu.VMEM((D_EXPERT, D_MODEL), jnp.bfloat16),
            pltpu.VMEM((D_MODEL, D_EXPERT), jnp.bfloat16),
            pltpu.VMEM((D_MODEL, D_EXPERT), jnp.bfloat16),
            pltpu.VMEM((D_EXPERT, D_MODEL), jnp.bfloat16),
            pltpu.SemaphoreType.DMA((MOE_BUFFERS,)),
            pltpu.SemaphoreType.DMA((MOE_BUFFERS,)),
        ],
    )
    return pl.pallas_call(
        _moe_kernel,
        grid_spec=grid_spec,
        out_shape=jax.ShapeDtypeStruct(h_all.shape, h_all.dtype),
        input_output_aliases={9: 0},
        compiler_params=pltpu.CompilerParams(
            dimension_semantics=("arbitrary",), vmem_limit_bytes=VMEM_LIMIT),
        name="moe_pairs",
    )(ea, eb, chg, stage, flag, nblk, tok0, gtok, stok, h_all, g_ffn, wr, br, w_gate, w_up, w_down)


def _final_norm_kernel(h_ref, g_ref, o_ref):
    for first in range(0, NORM_TILE, MIX_TILE):
        o_ref[first:first + MIX_TILE, :] = _rms(_load_tokens(h_ref, MIX_TILE, first), g_ref[...])


def _final_norm_call(h_all, g, n_rows):
    return pl.pallas_call(
        _final_norm_kernel,
        grid=(n_rows // NORM_TILE,),
        in_specs=[pl.BlockSpec((NORM_TILE * SUBLANES, LANES), lambda i: (i, 0)),
                  pl.BlockSpec((1, D_MODEL), lambda i: (0, 0))],
        out_specs=pl.BlockSpec((NORM_TILE, D_MODEL), lambda i: (i, 0)),
        out_shape=jax.ShapeDtypeStruct((n_rows, D_MODEL), jnp.float32),
        compiler_params=pltpu.CompilerParams(dimension_semantics=("arbitrary",), vmem_limit_bytes=VMEM_LIMIT),
        name="final_norm",
    )(h_all, g)


def _pick(tables, idx):
    hit = idx[None, :, None] == jnp.arange(tables.shape[1], dtype=jnp.int32)[None, None, :]
    return jnp.sum(jnp.where(hit, tables[:, None, :], 0), axis=-1)


def _dispatch_tables(cls, n_blocks):
    n_all = cls.shape[0]
    tok_bits = (n_all - 1).bit_length()
    key = jnp.sort((cls << tok_bits) | jnp.arange(n_all, dtype=jnp.int32))
    order = key & ((1 << tok_bits) - 1)
    nib = jnp.arange(16, dtype=jnp.int32)[None, :]
    hi_hot = ((cls >> 4)[:, None] == nib).astype(jnp.bfloat16)
    lo_hot = ((cls & 15)[:, None] == nib).astype(jnp.bfloat16)
    counts = jnp.dot(hi_hot.T, lo_hot, preferred_element_type=jnp.float32).reshape(N_CLASS_IDS).astype(jnp.int32)
    start = jnp.cumsum(counts) - counts
    padded = (counts + MOE_ROWS - 1) // MOE_ROWS * MOE_ROWS
    pad_end = jnp.cumsum(padded)
    pad_start = pad_end - padded
    nblk = (pad_end[-1] // MOE_ROWS).astype(jnp.int32)

    kidx = jnp.arange(n_blocks, dtype=jnp.int32)
    real = kidx < nblk
    p0 = kidx * MOE_ROWS
    cb = jnp.sum((pad_end[None, :] <= p0[:, None]).astype(jnp.int32), axis=1)
    cb = jnp.minimum(cb, N_CLASS_IDS - 1)
    cls_pad_start, cls_count, cls_start = _pick(jnp.stack([pad_start, counts, start]), cb)
    r0 = p0 - cls_pad_start
    nv = jnp.where(real, jnp.clip(cls_count - r0, 0, MOE_ROWS), 0).astype(jnp.int32)
    grp = cb >> 6
    blk_a = grp * EXPERTS_PER_GROUP + ((cb >> 3) & 7)
    blk_b = grp * EXPERTS_PER_GROUP + (cb & 7)
    r = jnp.arange(MOE_ROWS, dtype=jnp.int32)[None, :]
    blk_src0 = cls_start + r0

    never = jnp.int32(n_blocks)
    experts = jnp.arange(N_EXPERTS, dtype=jnp.int32)[:, None]
    used = real[None, :] & ((blk_a[None, :] == experts) | (blk_b[None, :] == experts))
    first_use = jnp.min(jnp.where(used, kidx[None, :], never), axis=1)
    new_a = real & (_pick(first_use[None, :], blk_a)[0] == kidx)
    new_b = real & (_pick(first_use[None, :], blk_b)[0] == kidx)
    filler = new_a & new_b
    blk_step = kidx + jnp.cumsum(filler.astype(jnp.int32))
    n_act = nblk + jnp.sum(filler.astype(jnp.int32))

    n_steps = n_blocks + N_EXPERTS // 2 + 1
    step = jnp.arange(n_steps, dtype=jnp.int32)
    k_done = jnp.sum((real[None, :] & (blk_step[None, :] <= step[:, None])).astype(jnp.int32), axis=1) - 1
    kc = jnp.clip(k_done, 0, n_blocks - 1)
    kn = jnp.clip(k_done + 1, 0, n_blocks - 1)
    active = step < n_act
    per_block = jnp.stack([blk_step, nv, blk_src0, blk_a, blk_b, new_a.astype(jnp.int32), new_b.astype(jnp.int32)])
    c_step, c_nv, c_src0, c_a, c_b, c_new_a, c_new_b = _pick(per_block, kc)
    n_a = _pick(blk_a[None, :], kn)[0]
    is_block = active & (k_done >= 0) & (c_step == step)
    is_filler = active & ~is_block

    spare = n_all + (step[:, None] & (PAD_BANKS - 1)) * MOE_ROWS + r
    src = jnp.clip(c_src0[:, None] + r, 0, n_all - 1)
    rows = jnp.where(is_block[:, None] & (r < c_nv[:, None]), order[src], spare).astype(jnp.int32)
    last = jnp.clip(nblk - 1, 0, n_blocks - 1)
    ea = jnp.where(is_block, c_a, jnp.where(is_filler, n_a, blk_a[last]))
    eb = jnp.where(is_block, c_b, jnp.where(is_filler, n_a, blk_b[last]))
    staged_here = jnp.where(
        is_block, jnp.where(c_new_b > 0, c_b, jnp.where(c_new_a > 0, c_a, -1)),
        jnp.where(is_filler, n_a, -1))
    flag = (staged_here >= 0).astype(jnp.int32)
    far = jnp.int32(n_steps)
    next_evt = lax.cummin(jnp.where(flag > 0, step, far), axis=0, reverse=True)
    last_evt = jnp.max(jnp.where(flag > 0, step, 0))
    stage = _pick(staged_here[None, :], jnp.where(next_evt < far, next_evt, last_evt))[0]

    sink = n_all + SINK_BANK * MOE_ROWS + r
    source = n_all + SOURCE_BANK * MOE_ROWS + r
    nxt = jnp.concatenate([rows[2:], rows[-2:]], axis=0)
    prv = jnp.concatenate([rows[:1], rows[:-1]], axis=0)
    gtok = jnp.where((step + 2 < n_act)[:, None], nxt, source)
    stok = jnp.where(((step >= 1) & (step <= n_act))[:, None], prv, sink)
    shape3 = (n_steps, 1, MOE_ROWS)
    rows, gtok, stok = rows * SUBLANES, gtok * SUBLANES, stok * SUBLANES
    chg = jnp.stack([jnp.concatenate([jnp.ones((1,), jnp.int32), (e[1:] != e[:-1]).astype(jnp.int32)])
                     for e in (ea, eb)])
    return (rows[:2].reshape(2, 1, MOE_ROWS), gtok.reshape(shape3), stok.reshape(shape3), ea, eb, chg,
            stage, flag, n_act.reshape(1).astype(jnp.int32))


def _over_sublanes(w):
    return jnp.broadcast_to(w[:, None, :], (w.shape[0], SUBLANES, w.shape[1]))


def kernel(x, meta_tokens, norm_mix_g, w_in, conf_dw_w, conf_dw_b, conf_ln_g, conf_ln_b, sc_conv_w, w_out,
           norm_ffn_g, w_router_group, b_router_group, w_router_expert, b_router_expert, w_exp_gate,
           w_exp_up, w_exp_down, final_norm_g):
    bsz, seq, d = x.shape
    depth = w_in.shape[0]
    n_main = bsz * seq
    n_all = n_main + N_META
    n_blocks = n_all // MOE_ROWS + N_CLASSES
    f32 = jnp.float32

    zero_halo = (jnp.zeros((HALO_A, CONF_W), f32), jnp.zeros((HALO_C, SC_W), f32))
    assert d == SUBLANES * LANES
    h_all = None

    for l in range(depth):
        w_r = jnp.concatenate([w_router_group[l], w_router_expert[l]], axis=1).astype(f32)
        w_r = jnp.pad(w_r, ((0, 0), (0, ROUTER_COLS - w_r.shape[1])))
        w_r = w_r.astype(jnp.bfloat16)
        b_r = jnp.concatenate([b_router_group[l], b_router_expert[l]]).astype(f32)
        b_r = jnp.pad(b_r, (0, ROUTER_COLS - b_r.shape[0])).reshape(1, ROUTER_COLS)
        g_ffn = norm_ffn_g[l].reshape(1, d)
        wts = (norm_mix_g[l].reshape(1, d), w_in[l].astype(jnp.bfloat16), _over_sublanes(conf_dw_w[l]),
               conf_dw_b[l].reshape(1, CONF_W), conf_ln_g[l].reshape(1, CONF_W),
               conf_ln_b[l].reshape(1, CONF_W), _over_sublanes(sc_conv_w[l]), w_out[l].astype(jnp.bfloat16),
               g_ffn, w_r, b_r)

        main = dict(row0=0, n_batch=bsz, n_tiles=seq // MIX_TILE, tt=MIX_TILE)
        if h_all is None:
            h_meta, cls_meta, halo_a, halo_c = _mixer_call(
                meta_tokens.astype(x.dtype), *zero_halo, wts, row0=0, n_batch=1, n_tiles=1, tt=N_META,
                fresh="rows")
            tail = jnp.concatenate([h_meta, jnp.zeros((SPARE_ROWS, d), x.dtype)], axis=0)
            h_all, cls_main, _, _ = _mixer_call(
                x.reshape(n_main, d), halo_a[0], halo_c[0], wts, fresh="tokens", tail=tail, **main)
        else:
            h_all, cls_meta, halo_a, halo_c = _mixer_call(
                h_all, *zero_halo, wts, row0=n_main, n_batch=1, n_tiles=1, tt=N_META)
            h_all, cls_main, _, _ = _mixer_call(h_all, halo_a[0], halo_c[0], wts, **main)

        cls = jnp.concatenate([cls_main[:, 0], cls_meta[:, 0]])
        h_all = _moe_call(h_all, _dispatch_tables(cls, n_blocks), g_ffn, w_r, b_r,
                          w_exp_gate, w_exp_up, w_exp_down, layer=l)

    out = _final_norm_call(h_all, final_norm_g.reshape(1, d), n_main)
    return out.reshape(bsz, seq, d)
```

```python
import functools

import jax
import jax.numpy as jnp
from jax import lax
from jax.experimental import pallas as pl
from jax.experimental.pallas import tpu as pltpu

D_MODEL = 1024
N_META = 16
CONF_W = 512
SC_W = 512
PROJ_W = 2 * CONF_W + 3 * SC_W
CONF_KERNEL = 31
SC_KERNEL = 3
N_GROUPS = 4
EXPERTS_PER_GROUP = 8
N_EXPERTS = N_GROUPS * EXPERTS_PER_GROUP
D_EXPERT = 512
EPS = 1e-6

LANES = 128
SUBLANES = 8
HALO_A = 32
HALO_C = 8
CONV_ROWS = 32
MIX_TILE = 1024
NORM_TILE = 2048
MOE_ROWS = 256
N_CLASS_IDS = N_GROUPS * 64
N_CLASSES = N_GROUPS * (EXPERTS_PER_GROUP * (EXPERTS_PER_GROUP - 1) // 2)
VMEM_LIMIT = 56 * 1024 * 1024
ROUTER_COLS = LANES
MOE_BUFFERS = 3
PAD_BANKS = 8
SINK_BANK = PAD_BANKS
SOURCE_BANK = PAD_BANKS + 1
TAIL_TILES = 3
SPARE_ROWS = TAIL_TILES * MIX_TILE - N_META
assert SPARE_ROWS >= (PAD_BANKS + 2) * MOE_ROWS


def _load_tokens(ref, n, first=0):
    return jnp.concatenate(
        [ref[pl.ds(first * SUBLANES + s, n, stride=SUBLANES), :] for s in range(SUBLANES)], axis=-1)


def _store_tokens(ref, x, n):
    for s in range(SUBLANES):
        ref[pl.ds(s, n, stride=SUBLANES), :] = x[:, s * LANES:(s + 1) * LANES]


def _rms(x, g):
    return x * lax.rsqrt(jnp.mean(x * x, axis=-1, keepdims=True) + EPS) * g


def _router_logits(hn_bf16, w_ref, b_ref):
    return jnp.dot(hn_bf16, w_ref[...], preferred_element_type=jnp.float32) + b_ref[...]


N_MIXER_INPUTS = 14


def _mixer_kernel(h_ref, *refs, tt, src_tm, dst_tm, n_batch, has_tail):
    if not has_tail:
        return _mixer_tile(h_ref, *refs, tt=tt, src_tm=src_tm, dst_tm=dst_tm)
    tail_ref, refs = refs[0], refs[1:]
    b = pl.program_id(0)
    pl.when(b < n_batch)(lambda: _mixer_tile(h_ref, *refs, tt=tt, src_tm=src_tm, dst_tm=dst_tm))

    @pl.when(b == n_batch)
    def _():
        hout_ref = refs[N_MIXER_INPUTS - 1]
        _store_tokens(hout_ref, tail_ref[...], tt)


def _mixer_tile(h_ref, halo_a_ref, halo_c_ref, g_mix_ref, w_in_ref, dw_w_ref, dw_b_ref, ln_g_ref,
                ln_b_ref, sc_w_ref, w_out_ref, g_ffn_ref, wr_ref, br_ref,
                hout_ref, cls_ref, halo_a_out, halo_c_out,
                aext, ashift, cext, sb_buf, ybuf, *, tt, src_tm, dst_tm):
    j = pl.program_id(1)

    @pl.when(j == 0)
    def _():
        aext[0:HALO_A, :] = halo_a_ref[...]
        cext[0:HALO_C, :] = halo_c_ref[...]

    h = _load_tokens(h_ref, tt) if src_tm else h_ref[...]
    xn = _rms(h, g_mix_ref[...]).astype(jnp.bfloat16)
    ug = jnp.dot(xn, w_in_ref[:, 0:2 * CONF_W], preferred_element_type=jnp.float32)
    aext[HALO_A:HALO_A + tt, :] = ug[:, 0:CONF_W] * jax.nn.sigmoid(ug[:, CONF_W:2 * CONF_W])
    bcv = jnp.dot(xn, w_in_ref[:, 2 * CONF_W:PROJ_W], preferred_element_type=jnp.float32)
    sb_buf[...] = bcv[:, 0:SC_W]
    cext[HALO_C:HALO_C + tt, :] = bcv[:, SC_W:2 * SC_W] * bcv[:, 2 * SC_W:3 * SC_W]

    n_shift_rows = HALO_A + tt - SUBLANES
    for p in range(1, SUBLANES):
        ashift[p - 1, 0:n_shift_rows, :] = aext[p:p + n_shift_rows, :]

    ch = min(tt, CONV_ROWS)
    tiles = (ch // SUBLANES, SUBLANES, CONF_W)
    for c0 in range(0, tt, ch):
        acc = jnp.broadcast_to(dw_b_ref[...], tiles)
        for k in range(CONF_KERNEL):
            row = HALO_A - (CONF_KERNEL - 1) + c0 + k
            phase = row % SUBLANES
            if phase == 0:
                tap = aext[row:row + ch, :]
            else:
                tap = ashift[phase - 1, row - phase:row - phase + ch, :]
            acc = acc + tap.reshape(tiles) * dw_w_ref[k]
        acc = acc.reshape(ch, CONF_W)
        mu = jnp.mean(acc, axis=-1, keepdims=True)
        xc = acc - mu
        ln = xc * lax.rsqrt(jnp.mean(xc * xc, axis=-1, keepdims=True) + EPS) * ln_g_ref[...] + ln_b_ref[...]
        ybuf[c0:c0 + ch, 0:CONF_W] = jax.nn.silu(ln).astype(jnp.bfloat16)
        sacc = None
        for k in range(SC_KERNEL):
            row = HALO_C - (SC_KERNEL - 1) + c0 + k
            term = cext[row:row + ch, :].reshape(tiles) * sc_w_ref[k]
            sacc = term if sacc is None else sacc + term
        ybuf[c0:c0 + ch, CONF_W:CONF_W + SC_W] = (
            sb_buf[c0:c0 + ch, :] * sacc.reshape(ch, SC_W)).astype(jnp.bfloat16)

    new_halo_a = aext[tt:tt + HALO_A, :]
    new_halo_c = cext[tt:tt + HALO_C, :]
    aext[0:HALO_A, :] = new_halo_a
    cext[0:HALO_C, :] = new_halo_c
    halo_a_out[0] = new_halo_a
    halo_c_out[0] = new_halo_c

    h_new = h + jnp.dot(ybuf[...], w_out_ref[...], preferred_element_type=jnp.float32)
    if dst_tm:
        _store_tokens(hout_ref, h_new, tt)
    else:
        hout_ref[...] = h_new

    logits = _router_logits(_rms(h_new, g_ffn_ref[...]).astype(jnp.bfloat16), wr_ref, br_ref)
    lane = lax.broadcasted_iota(jnp.int32, logits.shape, 1)
    neg = jnp.float32(-jnp.inf)
    big = jnp.int32(1 << 20)
    glog = jnp.where(lane < N_GROUPS, logits, neg)
    gmax = jnp.max(glog, axis=-1, keepdims=True)
    g_idx = jnp.min(jnp.where(glog == gmax, lane, big), axis=-1, keepdims=True)
    el = lane - N_GROUPS
    in_group = (el >= 0) & (el < N_EXPERTS) & ((el >> 3) == g_idx)
    l1 = jnp.where(in_group, logits, neg)
    m1 = jnp.max(l1, axis=-1, keepdims=True)
    i1 = jnp.min(jnp.where(l1 == m1, el, big), axis=-1, keepdims=True)
    l2 = jnp.where(el == i1, neg, l1)
    m2 = jnp.max(l2, axis=-1, keepdims=True)
    i2 = jnp.min(jnp.where(l2 == m2, el, big), axis=-1, keepdims=True)
    lo_e = jnp.minimum(i1, i2) & (EXPERTS_PER_GROUP - 1)
    hi_e = jnp.maximum(i1, i2) & (EXPERTS_PER_GROUP - 1)
    cls_ref[...] = g_idx * 64 + lo_e * 8 + hi_e


def _mixer_call(h_src, halo_a, halo_c, wts, *, row0, n_batch, n_tiles, tt, fresh=None, tail=None):
    blk0 = row0 // tt
    n_rows = n_batch * n_tiles * tt
    last = n_batch * n_tiles - 1
    tail_tiles = 0 if tail is None else tail.shape[0] // tt
    const = lambda shape: pl.BlockSpec(shape, lambda b, j: (0,) * len(shape))
    tile = lambda b, j: jnp.minimum(b * n_tiles + j, last)
    tm_spec = pl.BlockSpec((tt * SUBLANES, LANES), lambda b, j: (blk0 + b * n_tiles + j, 0))
    row_spec = pl.BlockSpec((tt, D_MODEL), lambda b, j: (blk0 + tile(b, j), 0))
    if fresh is None:
        in_spec, out_spec, out_struct = tm_spec, tm_spec, jax.ShapeDtypeStruct(h_src.shape, jnp.float32)
    elif fresh == "rows":
        in_spec, out_spec, out_struct = row_spec, row_spec, jax.ShapeDtypeStruct(h_src.shape, jnp.float32)
    else:
        in_spec = row_spec
        tail_at = lambda b, j: jnp.where(b == n_batch, jnp.minimum(j, tail_tiles - 1), j)
        out_spec = pl.BlockSpec((tt * SUBLANES, LANES), lambda b, j: (b * n_tiles + tail_at(b, j), 0))
        out_struct = jax.ShapeDtypeStruct(((n_rows + tail.shape[0]) * SUBLANES, LANES), jnp.float32)
    tail_specs, tail_args = [], []
    if tail is not None:
        tail_specs = [pl.BlockSpec(
            (tt, D_MODEL), lambda b, j: (jnp.where(b == n_batch, jnp.minimum(j, tail_tiles - 1), 0), 0))]
        tail_args = [tail]
    return pl.pallas_call(
        functools.partial(_mixer_kernel, tt=tt, src_tm=fresh is None, dst_tm=fresh != "rows",
                          n_batch=n_batch, has_tail=tail is not None),
        grid=(n_batch + (tail is not None), n_tiles),
        in_specs=[
            in_spec, *tail_specs,
            const((HALO_A, CONF_W)), const((HALO_C, SC_W)),
            const((1, D_MODEL)), const((D_MODEL, PROJ_W)),
            const((CONF_KERNEL, SUBLANES, CONF_W)), const((1, CONF_W)), const((1, CONF_W)), const((1, CONF_W)),
            const((SC_KERNEL, SUBLANES, SC_W)), const((CONF_W + SC_W, D_MODEL)), const((1, D_MODEL)),
            const((D_MODEL, ROUTER_COLS)), const((1, ROUTER_COLS)),
        ],
        out_specs=[
            out_spec,
            pl.BlockSpec((tt, 1), lambda b, j: (tile(b, j), 0)),
            pl.BlockSpec((1, HALO_A, CONF_W), lambda b, j: (jnp.minimum(b, n_batch - 1), 0, 0)),
            pl.BlockSpec((1, HALO_C, SC_W), lambda b, j: (jnp.minimum(b, n_batch - 1), 0, 0)),
        ],
        out_shape=[
            out_struct,
            jax.ShapeDtypeStruct((n_rows, 1), jnp.int32),
            jax.ShapeDtypeStruct((n_batch, HALO_A, CONF_W), jnp.float32),
            jax.ShapeDtypeStruct((n_batch, HALO_C, SC_W), jnp.float32),
        ],
        scratch_shapes=[
            pltpu.VMEM((HALO_A + tt, CONF_W), jnp.float32),
            pltpu.VMEM((SUBLANES - 1, HALO_A + tt, CONF_W), jnp.float32),
            pltpu.VMEM((HALO_C + tt, SC_W), jnp.float32),
            pltpu.VMEM((tt, SC_W), jnp.float32),
            pltpu.VMEM((tt, CONF_W + SC_W), jnp.bfloat16),
        ],
        input_output_aliases={0: 0} if fresh is None else {},
        compiler_params=pltpu.CompilerParams(
            dimension_semantics=("arbitrary", "arbitrary"), vmem_limit_bytes=VMEM_LIMIT),
        name=f"mixer_t{tt}",
    )(h_src, *tail_args, halo_a, halo_c, *wts)


def _moe_kernel(ea_ref, eb_ref, chg_ref, stage_ref, flag_ref, nblk_ref,
                tok0_ref, gtok_ref, stok_ref, h_in, g_ffn_ref, wr_ref, br_ref,
                wg_st, wu_st, wd_st,
                h_out,
                xbuf0, xbuf1, xbuf2, obuf0, obuf1, obuf2, wg_c, wu_c, wd_c, wga, wua, wda, wgb, wub, wdb,
                gsem, ssem):
    i = pl.program_id(0)
    nblk = nblk_ref[0]
    xbuf = (xbuf0, xbuf1, xbuf2)
    obuf = (obuf0, obuf1, obuf2)
    cur_a = (wga, wua, wda)
    cur_b = (wgb, wub, wdb)

    def token_tile(tok_ref, r, blk=0):
        return pl.ds(pl.multiple_of(tok_ref[blk, 0, r], SUBLANES), SUBLANES)

    def gather_row(tok_ref, r, dst_slot, blk=0):
        return pltpu.make_async_copy(
            h_in.at[token_tile(tok_ref, r, blk), :], xbuf[dst_slot].at[pl.ds(r * SUBLANES, SUBLANES), :],
            gsem.at[dst_slot])

    def scatter_row(r, src_slot):
        return pltpu.make_async_copy(
            obuf[src_slot].at[pl.ds(r * SUBLANES, SUBLANES), :], h_out.at[token_tile(stok_ref, r), :],
            ssem.at[src_slot])

    def wait_gather(dst_slot):
        pltpu.make_async_copy(
            h_in.at[pl.ds(0, MOE_ROWS * SUBLANES), :], xbuf[dst_slot], gsem.at[dst_slot]).wait()

    def wait_scatter(src_slot):
        pltpu.make_async_copy(
            obuf[src_slot], h_out.at[pl.ds(0, MOE_ROWS * SUBLANES), :], ssem.at[src_slot]).wait()

    @pl.when(i == 0)
    def _():
        obuf2[...] = jnp.zeros_like(obuf2)

        def start(r, c):
            gather_row(tok0_ref, r, 0, blk=0).start()
            gather_row(tok0_ref, r, 1, blk=1).start()
            return c

        lax.fori_loop(0, MOE_ROWS, start, 0)

    def step(slot):
        ahead = (slot + 2) % MOE_BUFFERS
        behind = (slot - 1) % MOE_BUFFERS
        ea = ea_ref[i]
        eb = eb_ref[i]
        wait_gather(slot)

        @pl.when(i >= 2)
        def _():
            wait_scatter(slot)

        @pl.when(flag_ref[i] != 0)
        def _():
            dst = stage_ref[i] & (EXPERTS_PER_GROUP - 1)
            wg_c[dst] = wg_st[0, 0].astype(jnp.bfloat16)
            wu_c[dst] = wu_st[0, 0].astype(jnp.bfloat16)
            wd_c[dst] = wd_st[0, 0].astype(jnp.bfloat16)

        for e, changed, cur in ((ea, chg_ref[0, i], cur_a), (eb, chg_ref[1, i], cur_b)):
            @pl.when(changed != 0)
            def _(e=e, cur=cur):
                at = e & (EXPERTS_PER_GROUP - 1)
                cur[0][...] = wg_c[at]
                cur[1][...] = wu_c[at]
                cur[2][...] = wd_c[at]

        for r in range(MOE_ROWS):
            gather_row(gtok_ref, r, ahead).start()
        for r in range(MOE_ROWS):
            scatter_row(r, behind).start()

        x = _load_tokens(xbuf[slot], MOE_ROWS)
        xb = _rms(x, g_ffn_ref[...]).astype(jnp.bfloat16)
        logits = _router_logits(xb, wr_ref, br_ref)
        lane = lax.broadcasted_iota(jnp.int32, logits.shape, 1)
        neg = jnp.float32(-jnp.inf)
        glog = jnp.where(lane < N_GROUPS, logits, neg)
        gmax = jnp.max(glog, axis=-1, keepdims=True)
        gsum = jnp.sum(jnp.exp(glog - gmax), axis=-1, keepdims=True)
        grp = ea >> 3
        l_g = jnp.sum(jnp.where(lane == grp, logits, 0.0), axis=-1, keepdims=True)
        p_g = jnp.exp(l_g - gmax) / gsum
        l_a = jnp.sum(jnp.where(lane == N_GROUPS + ea, logits, 0.0), axis=-1, keepdims=True)
        l_b = jnp.sum(jnp.where(lane == N_GROUPS + eb, logits, 0.0), axis=-1, keepdims=True)
        w_a = p_g / (1.0 + jnp.exp(l_b - l_a))
        w_b = p_g / (1.0 + jnp.exp(l_a - l_b))

        def expert(cur):
            gate = jnp.dot(xb, cur[0][...], preferred_element_type=jnp.float32)
            up = jnp.dot(xb, cur[1][...], preferred_element_type=jnp.float32)
            hid = (jax.nn.silu(gate) * up).astype(jnp.bfloat16)
            return jnp.dot(hid, cur[2][...], preferred_element_type=jnp.float32)

        y = w_a * expert(cur_a) + w_b * expert(cur_b)
        _store_tokens(obuf[slot], x + y, MOE_ROWS)

        @pl.when(i == nblk)
        def _():
            wait_gather((slot + 1) % MOE_BUFFERS)
            wait_gather(ahead)
            wait_scatter((slot - 2) % MOE_BUFFERS)
            wait_scatter(behind)

    rotation = lax.rem(i, MOE_BUFFERS)
    for slot in range(MOE_BUFFERS):
        pl.when((i <= nblk) & (rotation == slot))(functools.partial(step, slot))


def _moe_call(h_all, tables, g_ffn, wr, br, w_gate, w_up, w_down, *, layer):
    tok0, gtok, stok, ea, eb, chg, stage, flag, nblk = tables
    const = lambda shape: pl.BlockSpec(shape, lambda i, *_: (0,) * len(shape))
    staged = lambda shape: pl.BlockSpec(
        (1, 1) + shape, lambda i, ea_, eb_, ch_, st_, fl_, nb_: (layer, st_[i], 0, 0))
    tok_spec = lambda imap: pl.BlockSpec((1, 1, MOE_ROWS), imap, memory_space=pltpu.SMEM)
    grid_spec = pltpu.PrefetchScalarGridSpec(
        num_scalar_prefetch=6,
        grid=(nblk[0] + 1,),
        in_specs=[
            pl.BlockSpec((2, 1, MOE_ROWS), lambda i, *_: (0, 0, 0), memory_space=pltpu.SMEM),
            tok_spec(lambda i, *_: (i, 0, 0)), tok_spec(lambda i, *_: (i, 0, 0)),
            pl.BlockSpec(memory_space=pl.ANY),
            const((1, D_MODEL)), const((D_MODEL, ROUTER_COLS)), const((1, ROUTER_COLS)),
            staged((D_MODEL, D_EXPERT)), staged((D_MODEL, D_EXPERT)), staged((D_EXPERT, D_MODEL)),
        ],
        out_specs=pl.BlockSpec(memory_space=pl.ANY),
        scratch_shapes=[
            *[pltpu.VMEM((MOE_ROWS * SUBLANES, LANES), jnp.float32)] * (2 * MOE_BUFFERS),
            pltpu.VMEM((EXPERTS_PER_GROUP, D_MODEL, D_EXPERT), jnp.bfloat16),
            pltpu.VMEM((EXPERTS_PER_GROUP, D_MODEL, D_EXPERT), jnp.bfloat16),
            pltpu.VMEM((EXPERTS_PER_GROUP, D_EXPERT, D_MODEL), jnp.bfloat16),
            pltpu.VMEM((D_MODEL, D_EXPERT), jnp.bfloat16),
            pltpu.VMEM((D_MODEL, D_EXPERT), jnp.bfloat16),
            pltpu.VMEM((D_EXPERT, D_MODEL), jnp.bfloat16),
            pltpu.VMEM((D_MODEL, D_EXPERT), jnp.bfloat16),
            pltpu.VMEM((D_MODEL, D_EXPERT), jnp.bfloat16),
            pltpu.VMEM((D_EXPERT, D_MODEL), jnp.bfloat16),
            pltpu.SemaphoreType.DMA((MOE_BUFFERS,)),
            pltpu.SemaphoreType.DMA((MOE_BUFFERS,)),
        ],
    )
    return pl.pallas_call(
        _moe_kernel,
        grid_spec=grid_spec,
        out_shape=jax.ShapeDtypeStruct(h_all.shape, h_all.dtype),
        input_output_aliases={9: 0},
        compiler_params=pltpu.CompilerParams(
            dimension_semantics=("arbitrary",), vmem_limit_bytes=VMEM_LIMIT),
        name="moe_pairs",
    )(ea, eb, chg, stage, flag, nblk, tok0, gtok, stok, h_all, g_ffn, wr, br, w_gate, w_up, w_down)


def _final_norm_kernel(h_ref, g_ref, o_ref):
    for first in range(0, NORM_TILE, MIX_TILE):
        o_ref[first:first + MIX_TILE, :] = _rms(_load_tokens(h_ref, MIX_TILE, first), g_ref[...])


def _final_norm_call(h_all, g, n_rows):
    return pl.pallas_call(
        _final_norm_kernel,
        grid=(n_rows // NORM_TILE,),
        in_specs=[pl.BlockSpec((NORM_TILE * SUBLANES, LANES), lambda i: (i, 0)),
                  pl.BlockSpec((1, D_MODEL), lambda i: (0, 0))],
        out_specs=pl.BlockSpec((NORM_TILE, D_MODEL), lambda i: (i, 0)),
        out_shape=jax.ShapeDtypeStruct((n_rows, D_MODEL), jnp.float32),
        compiler_params=pltpu.CompilerParams(dimension_semantics=("arbitrary",), vmem_limit_bytes=VMEM_LIMIT),
        name="final_norm",
    )(h_all, g)


def _pick(tables, idx):
    hit = idx[None, :, None] == jnp.arange(tables.shape[1], dtype=jnp.int32)[None, None, :]
    return jnp.sum(jnp.where(hit, tables[:, None, :], 0), axis=-1)


def _dispatch_tables(cls, n_blocks):
    n_all = cls.shape[0]
    tok_bits = (n_all - 1).bit_length()
    key = jnp.sort((cls << tok_bits) | jnp.arange(n_all, dtype=jnp.int32))
    order = key & ((1 << tok_bits) - 1)
    nib = jnp.arange(16, dtype=jnp.int32)[None, :]
    hi_hot = ((cls >> 4)[:, None] == nib).astype(jnp.bfloat16)
    lo_hot = ((cls & 15)[:, None] == nib).astype(jnp.bfloat16)
    counts = jnp.dot(hi_hot.T, lo_hot, preferred_element_type=jnp.float32).reshape(N_CLASS_IDS).astype(jnp.int32)
    start = jnp.cumsum(counts) - counts
    padded = (counts + MOE_ROWS - 1) // MOE_ROWS * MOE_ROWS
    pad_end = jnp.cumsum(padded)
    pad_start = pad_end - padded
    nblk = (pad_end[-1] // MOE_ROWS).astype(jnp.int32)

    kidx = jnp.arange(n_blocks, dtype=jnp.int32)
    real = kidx < nblk
    p0 = kidx * MOE_ROWS
    cb = jnp.sum((pad_end[None, :] <= p0[:, None]).astype(jnp.int32), axis=1)
    cb = jnp.minimum(cb, N_CLASS_IDS - 1)
    cls_pad_start, cls_count, cls_start = _pick(jnp.stack([pad_start, counts, start]), cb)
    r0 = p0 - cls_pad_start
    nv = jnp.where(real, jnp.clip(cls_count - r0, 0, MOE_ROWS), 0).astype(jnp.int32)
    grp = cb >> 6
    blk_a = grp * EXPERTS_PER_GROUP + ((cb >> 3) & 7)
    blk_b = grp * EXPERTS_PER_GROUP + (cb & 7)
    r = jnp.arange(MOE_ROWS, dtype=jnp.int32)[None, :]
    blk_src0 = cls_start + r0

    never = jnp.int32(n_blocks)
    experts = jnp.arange(N_EXPERTS, dtype=jnp.int32)[:, None]
    used = real[None, :] & ((blk_a[None, :] == experts) | (blk_b[None, :] == experts))
    first_use = jnp.min(jnp.where(used, kidx[None, :], never), axis=1)
    new_a = real & (_pick(first_use[None, :], blk_a)[0] == kidx)
    new_b = real & (_pick(first_use[None, :], blk_b)[0] == kidx)
    filler = new_a & new_b
    blk_step = kidx + jnp.cumsum(filler.astype(jnp.int32))
    n_act = nblk + jnp.sum(filler.astype(jnp.int32))

    n_steps = n_blocks + N_EXPERTS // 2 + 1
    step = jnp.arange(n_steps, dtype=jnp.int32)
    k_done = jnp.sum((real[None, :] & (blk_step[None, :] <= step[:, None])).astype(jnp.int32), axis=1) - 1
    kc = jnp.clip(k_done, 0, n_blocks - 1)
    kn = jnp.clip(k_done + 1, 0, n_blocks - 1)
    active = step < n_act
    per_block = jnp.stack([blk_step, nv, blk_src0, blk_a, blk_b, new_a.astype(jnp.int32), new_b.astype(jnp.int32)])
    c_step, c_nv, c_src0, c_a, c_b, c_new_a, c_new_b = _pick(per_block, kc)
    n_a = _pick(blk_a[None, :], kn)[0]
    is_block = active & (k_done >= 0) & (c_step == step)
    is_filler = active & ~is_block

    spare = n_all + (step[:, None] & (PAD_BANKS - 1)) * MOE_ROWS + r
    src = jnp.clip(c_src0[:, None] + r, 0, n_all - 1)
    rows = jnp.where(is_block[:, None] & (r < c_nv[:, None]), order[src], spare).astype(jnp.int32)
    last = jnp.clip(nblk - 1, 0, n_blocks - 1)
    ea = jnp.where(is_block, c_a, jnp.where(is_filler, n_a, blk_a[last]))
    eb = jnp.where(is_block, c_b, jnp.where(is_filler, n_a, blk_b[last]))
    staged_here = jnp.where(
        is_block, jnp.where(c_new_b > 0, c_b, jnp.where(c_new_a > 0, c_a, -1)),
        jnp.where(is_filler, n_a, -1))
    flag = (staged_here >= 0).astype(jnp.int32)
    far = jnp.int32(n_steps)
    next_evt = lax.cummin(jnp.where(flag > 0, step, far), axis=0, reverse=True)
    last_evt = jnp.max(jnp.where(flag > 0, step, 0))
    stage = _pick(staged_here[None, :], jnp.where(next_evt < far, next_evt, last_evt))[0]

    sink = n_all + SINK_BANK * MOE_ROWS + r
    source = n_all + SOURCE_BANK * MOE_ROWS + r
    nxt = jnp.concatenate([rows[2:], rows[-2:]], axis=0)
    prv = jnp.concatenate([rows[:1], rows[:-1]], axis=0)
    gtok = jnp.where((step + 2 < n_act)[:, None], nxt, source)
    stok = jnp.where(((step >= 1) & (step <= n_act))[:, None], prv, sink)
    shape3 = (n_steps, 1, MOE_ROWS)
    rows, gtok, stok = rows * SUBLANES, gtok * SUBLANES, stok * SUBLANES
    chg = jnp.stack([jnp.concatenate([jnp.ones((1,), jnp.int32), (e[1:] != e[:-1]).astype(jnp.int32)])
                     for e in (ea, eb)])
    return (rows[:2].reshape(2, 1, MOE_ROWS), gtok.reshape(shape3), stok.reshape(shape3), ea, eb, chg,
            stage, flag, n_act.reshape(1).astype(jnp.int32))


def _over_sublanes(w):
    return jnp.broadcast_to(w[:, None, :], (w.shape[0], SUBLANES, w.shape[1]))


def kernel(x, meta_tokens, norm_mix_g, w_in, conf_dw_w, conf_dw_b, conf_ln_g, conf_ln_b, sc_conv_w, w_out,
           norm_ffn_g, w_router_group, b_router_group, w_router_expert, b_router_expert, w_exp_gate,
           w_exp_up, w_exp_down, final_norm_g):
    bsz, seq, d = x.shape
    depth = w_in.shape[0]
    n_main = bsz * seq
    n_all = n_main + N_META
    n_blocks = n_all // MOE_ROWS + N_CLASSES
    f32 = jnp.float32

    zero_halo = (jnp.zeros((HALO_A, CONF_W), f32), jnp.zeros((HALO_C, SC_W), f32))
    assert d == SUBLANES * LANES
    h_all = None

    for l in range(depth):
        w_r = jnp.concatenate([w_router_group[l], w_router_expert[l]], axis=1).astype(f32)
        w_r = jnp.pad(w_r, ((0, 0), (0, ROUTER_COLS - w_r.shape[1])))
        w_r = w_r.astype(jnp.bfloat16)
        b_r = jnp.concatenate([b_router_group[l], b_router_expert[l]]).astype(f32)
        b_r = jnp.pad(b_r, (0, ROUTER_COLS - b_r.shape[0])).reshape(1, ROUTER_COLS)
        g_ffn = norm_ffn_g[l].reshape(1, d)
        wts = (norm_mix_g[l].reshape(1, d), w_in[l].astype(jnp.bfloat16), _over_sublanes(conf_dw_w[l]),
               conf_dw_b[l].reshape(1, CONF_W), conf_ln_g[l].reshape(1, CONF_W),
               conf_ln_b[l].reshape(1, CONF_W), _over_sublanes(sc_conv_w[l]), w_out[l].astype(jnp.bfloat16),
               g_ffn, w_r, b_r)

        main = dict(row0=0, n_batch=bsz, n_tiles=seq // MIX_TILE, tt=MIX_TILE)
        if h_all is None:
            h_meta, cls_meta, halo_a, halo_c = _mixer_call(
                meta_tokens.astype(x.dtype), *zero_halo, wts, row0=0, n_batch=1, n_tiles=1, tt=N_META,
                fresh="rows")
            tail = jnp.concatenate([h_meta, jnp.zeros((SPARE_ROWS, d), x.dtype)], axis=0)
            h_all, cls_main, _, _ = _mixer_call(
                x.reshape(n_main, d), halo_a[0], halo_c[0], wts, fresh="tokens", tail=tail, **main)
        else:
            h_all, cls_meta, halo_a, halo_c = _mixer_call(
                h_all, *zero_halo, wts, row0=n_main, n_batch=1, n_tiles=1, tt=N_META)
            h_all, cls_main, _, _ = _mixer_call(h_all, halo_a[0], halo_c[0], wts, **main)

        cls = jnp.concatenate([cls_main[:, 0], cls_meta[:, 0]])
        h_all = _moe_call(h_all, _dispatch_tables(cls, n_blocks), g_ffn, w_r, b_r,
                          w_exp_gate, w_exp_up, w_exp_down, layer=l)

    out = _final_norm_call(h_all, final_norm_g.reshape(1, d), n_main)
    return out.reshape(bsz, seq, d)
```

```python
import functools

import jax
import jax.numpy as jnp
from jax import lax
from jax.experimental import pallas as pl
from jax.experimental.pallas import tpu as pltpu

D_MODEL = 1024
N_META = 16
CONF_W = 512
SC_W = 512
PROJ_W = 2 * CONF_W + 3 * SC_W
CONF_KERNEL = 31
SC_KERNEL = 3
N_GROUPS = 4
EXPERTS_PER_GROUP = 8
N_EXPERTS = N_GROUPS * EXPERTS_PER_GROUP
D_EXPERT = 512
EPS = 1e-6

LANES = 128
SUBLANES = 8
HALO_A = 32
HALO_C = 8
CONV_ROWS = 32
MIX_TILE = 1024
NORM_TILE = 2048
MOE_ROWS = 192
N_CLASS_IDS = N_GROUPS * 64
N_CLASSES = N_GROUPS * (EXPERTS_PER_GROUP * (EXPERTS_PER_GROUP - 1) // 2)
VMEM_LIMIT = 56 * 1024 * 1024
ROUTER_COLS = LANES
MOE_BUFFERS = 3
PAD_BANKS = 8
SINK_BANK = PAD_BANKS
SOURCE_BANK = PAD_BANKS + 1
TAIL_TILES = 3
SPARE_ROWS = TAIL_TILES * MIX_TILE - N_META
assert SPARE_ROWS >= (PAD_BANKS + 2) * MOE_ROWS


def _load_tokens(ref, n, first=0):
    return jnp.concatenate(
        [ref[pl.ds(first * SUBLANES + s, n, stride=SUBLANES), :] for s in range(SUBLANES)], axis=-1)


def _store_tokens(ref, x, n):
    for s in range(SUBLANES):
        ref[pl.ds(s, n, stride=SUBLANES), :] = x[:, s * LANES:(s + 1) * LANES]


def _rms(x, g):
    return x * lax.rsqrt(jnp.mean(x * x, axis=-1, keepdims=True) + EPS) * g


def _router_logits(hn_bf16, w_ref, b_ref):
    return jnp.dot(hn_bf16, w_ref[...], preferred_element_type=jnp.float32) + b_ref[...]


N_MIXER_INPUTS = 14


def _mixer_kernel(h_ref, *refs, tt, src_tm, dst_tm, n_batch, has_tail):
    if not has_tail:
        return _mixer_tile(h_ref, *refs, tt=tt, src_tm=src_tm, dst_tm=dst_tm)
    tail_ref, refs = refs[0], refs[1:]
    b = pl.program_id(0)
    pl.when(b < n_batch)(lambda: _mixer_tile(h_ref, *refs, tt=tt, src_tm=src_tm, dst_tm=dst_tm))

    @pl.when(b == n_batch)
    def _():
        hout_ref = refs[N_MIXER_INPUTS - 1]
        _store_tokens(hout_ref, tail_ref[...], tt)


def _mixer_tile(h_ref, halo_a_ref, halo_c_ref, g_mix_ref, w_in_ref, dw_w_ref, dw_b_ref, ln_g_ref,
                ln_b_ref, sc_w_ref, w_out_ref, g_ffn_ref, wr_ref, br_ref,
                hout_ref, cls_ref, halo_a_out, halo_c_out,
                aext, ashift, cext, sb_buf, ybuf, *, tt, src_tm, dst_tm):
    j = pl.program_id(1)

    @pl.when(j == 0)
    def _():
        aext[0:HALO_A, :] = halo_a_ref[...]
        cext[0:HALO_C, :] = halo_c_ref[...]

    h = _load_tokens(h_ref, tt) if src_tm else h_ref[...]
    xn = _rms(h, g_mix_ref[...]).astype(jnp.bfloat16)
    ug = jnp.dot(xn, w_in_ref[:, 0:2 * CONF_W], preferred_element_type=jnp.float32)
    aext[HALO_A:HALO_A + tt, :] = ug[:, 0:CONF_W] * jax.nn.sigmoid(ug[:, CONF_W:2 * CONF_W])
    bcv = jnp.dot(xn, w_in_ref[:, 2 * CONF_W:PROJ_W], preferred_element_type=jnp.float32)
    sb_buf[...] = bcv[:, 0:SC_W]
    cext[HALO_C:HALO_C + tt, :] = bcv[:, SC_W:2 * SC_W] * bcv[:, 2 * SC_W:3 * SC_W]

    n_shift_rows = HALO_A + tt - SUBLANES
    for p in range(1, SUBLANES):
        ashift[p - 1, 0:n_shift_rows, :] = aext[p:p + n_shift_rows, :]

    ch = min(tt, CONV_ROWS)
    tiles = (ch // SUBLANES, SUBLANES, CONF_W)
    for c0 in range(0, tt, ch):
        acc = jnp.broadcast_to(dw_b_ref[...], tiles)
        for k in range(CONF_KERNEL):
            row = HALO_A - (CONF_KERNEL - 1) + c0 + k
            phase = row % SUBLANES
            if phase == 0:
                tap = aext[row:row + ch, :]
            else:
                tap = ashift[phase - 1, row - phase:row - phase + ch, :]
            acc = acc + tap.reshape(tiles) * dw_w_ref[k]
        acc = acc.reshape(ch, CONF_W)
        mu = jnp.mean(acc, axis=-1, keepdims=True)
        xc = acc - mu
        ln = xc * lax.rsqrt(jnp.mean(xc * xc, axis=-1, keepdims=True) + EPS) * ln_g_ref[...] + ln_b_ref[...]
        ybuf[c0:c0 + ch, 0:CONF_W] = jax.nn.silu(ln).astype(jnp.bfloat16)
        sacc = None
        for k in range(SC_KERNEL):
            row = HALO_C - (SC_KERNEL - 1) + c0 + k
            term = cext[row:row + ch, :].reshape(tiles) * sc_w_ref[k]
            sacc = term if sacc is None else sacc + term
        ybuf[c0:c0 + ch, CONF_W:CONF_W + SC_W] = (
            sb_buf[c0:c0 + ch, :] * sacc.reshape(ch, SC_W)).astype(jnp.bfloat16)

    new_halo_a = aext[tt:tt + HALO_A, :]
    new_halo_c = cext[tt:tt + HALO_C, :]
    aext[0:HALO_A, :] = new_halo_a
    cext[0:HALO_C, :] = new_halo_c
    halo_a_out[0] = new_halo_a
    halo_c_out[0] = new_halo_c

    h_new = h + jnp.dot(ybuf[...], w_out_ref[...], preferred_element_type=jnp.float32)
    if dst_tm:
        _store_tokens(hout_ref, h_new, tt)
    else:
        hout_ref[...] = h_new

    logits = _router_logits(_rms(h_new, g_ffn_ref[...]).astype(jnp.bfloat16), wr_ref, br_ref)
    lane = lax.broadcasted_iota(jnp.int32, logits.shape, 1)
    neg = jnp.float32(-jnp.inf)
    big = jnp.int32(1 << 20)
    glog = jnp.where(lane < N_GROUPS, logits, neg)
    gmax = jnp.max(glog, axis=-1, keepdims=True)
    g_idx = jnp.min(jnp.where(glog == gmax, lane, big), axis=-1, keepdims=True)
    el = lane - N_GROUPS
    in_group = (el >= 0) & (el < N_EXPERTS) & ((el >> 3) == g_idx)
    l1 = jnp.where(in_group, logits, neg)
    m1 = jnp.max(l1, axis=-1, keepdims=True)
    i1 = jnp.min(jnp.where(l1 == m1, el, big), axis=-1, keepdims=True)
    l2 = jnp.where(el == i1, neg, l1)
    m2 = jnp.max(l2, axis=-1, keepdims=True)
    i2 = jnp.min(jnp.where(l2 == m2, el, big), axis=-1, keepdims=True)
    lo_e = jnp.minimum(i1, i2) & (EXPERTS_PER_GROUP - 1)
    hi_e = jnp.maximum(i1, i2) & (EXPERTS_PER_GROUP - 1)
    cls_ref[...] = g_idx * 64 + lo_e * 8 + hi_e


def _mixer_call(h_src, halo_a, halo_c, wts, *, row0, n_batch, n_tiles, tt, fresh=None, tail=None):
    blk0 = row0 // tt
    n_rows = n_batch * n_tiles * tt
    last = n_batch * n_tiles - 1
    tail_tiles = 0 if tail is None else tail.shape[0] // tt
    const = lambda shape: pl.BlockSpec(shape, lambda b, j: (0,) * len(shape))
    tile = lambda b, j: jnp.minimum(b * n_tiles + j, last)
    tm_spec = pl.BlockSpec((tt * SUBLANES, LANES), lambda b, j: (blk0 + b * n_tiles + j, 0))
    row_spec = pl.BlockSpec((tt, D_MODEL), lambda b, j: (blk0 + tile(b, j), 0))
    if fresh is None:
        in_spec, out_spec, out_struct = tm_spec, tm_spec, jax.ShapeDtypeStruct(h_src.shape, jnp.float32)
    elif fresh == "rows":
        in_spec, out_spec, out_struct = row_spec, row_spec, jax.ShapeDtypeStruct(h_src.shape, jnp.float32)
    else:
        in_spec = row_spec
        tail_at = lambda b, j: jnp.where(b == n_batch, jnp.minimum(j, tail_tiles - 1), j)
        out_spec = pl.BlockSpec((tt * SUBLANES, LANES), lambda b, j: (b * n_tiles + tail_at(b, j), 0))
        out_struct = jax.ShapeDtypeStruct(((n_rows + tail.shape[0]) * SUBLANES, LANES), jnp.float32)
    tail_specs, tail_args = [], []
    if tail is not None:
        tail_specs = [pl.BlockSpec(
            (tt, D_MODEL), lambda b, j: (jnp.where(b == n_batch, jnp.minimum(j, tail_tiles - 1), 0), 0))]
        tail_args = [tail]
    return pl.pallas_call(
        functools.partial(_mixer_kernel, tt=tt, src_tm=fresh is None, dst_tm=fresh != "rows",
                          n_batch=n_batch, has_tail=tail is not None),
        grid=(n_batch + (tail is not None), n_tiles),
        in_specs=[
            in_spec, *tail_specs,
            const((HALO_A, CONF_W)), const((HALO_C, SC_W)),
            const((1, D_MODEL)), const((D_MODEL, PROJ_W)),
            const((CONF_KERNEL, SUBLANES, CONF_W)), const((1, CONF_W)), const((1, CONF_W)), const((1, CONF_W)),
            const((SC_KERNEL, SUBLANES, SC_W)), const((CONF_W + SC_W, D_MODEL)), const((1, D_MODEL)),
            const((D_MODEL, ROUTER_COLS)), const((1, ROUTER_COLS)),
        ],
        out_specs=[
            out_spec,
            pl.BlockSpec((tt, 1), lambda b, j: (tile(b, j), 0)),
            pl.BlockSpec((1, HALO_A, CONF_W), lambda b, j: (jnp.minimum(b, n_batch - 1), 0, 0)),
            pl.BlockSpec((1, HALO_C, SC_W), lambda b, j: (jnp.minimum(b, n_batch - 1), 0, 0)),
        ],
        out_shape=[
            out_struct,
            jax.ShapeDtypeStruct((n_rows, 1), jnp.int32),
            jax.ShapeDtypeStruct((n_batch, HALO_A, CONF_W), jnp.float32),
            jax.ShapeDtypeStruct((n_batch, HALO_C, SC_W), jnp.float32),
        ],
        scratch_shapes=[
            pltpu.VMEM((HALO_A + tt, CONF_W), jnp.float32),
            pltpu.VMEM((SUBLANES - 1, HALO_A + tt, CONF_W), jnp.float32),
            pltpu.VMEM((HALO_C + tt, SC_W), jnp.float32),
            pltpu.VMEM((tt, SC_W), jnp.float32),
            pltpu.VMEM((tt, CONF_W + SC_W), jnp.bfloat16),
        ],
        input_output_aliases={0: 0} if fresh is None else {},
        compiler_params=pltpu.CompilerParams(
            dimension_semantics=("arbitrary", "arbitrary"), vmem_limit_bytes=VMEM_LIMIT),
        name=f"mixer_t{tt}",
    )(h_src, *tail_args, halo_a, halo_c, *wts)


def _moe_kernel(ea_ref, eb_ref, chg_ref, stage_ref, flag_ref, nblk_ref,
                tok0_ref, gtok_ref, stok_ref, h_in, g_ffn_ref, wr_ref, br_ref,
                wg_st, wu_st, wd_st,
                h_out,
                xbuf0, xbuf1, xbuf2, obuf0, obuf1, obuf2, wg_c, wu_c, wd_c, wga, wua, wda, wgb, wub, wdb,
                gsem, ssem):
    i = pl.program_id(0)
    nblk = nblk_ref[0]
    xbuf = (xbuf0, xbuf1, xbuf2)
    obuf = (obuf0, obuf1, obuf2)
    cur_a = (wga, wua, wda)
    cur_b = (wgb, wub, wdb)

    def token_tile(tok_ref, r, blk=0):
        return pl.ds(pl.multiple_of(tok_ref[blk, 0, r], SUBLANES), SUBLANES)

    def gather_row(tok_ref, r, dst_slot, blk=0):
        return pltpu.make_async_copy(
            h_in.at[token_tile(tok_ref, r, blk), :], xbuf[dst_slot].at[pl.ds(r * SUBLANES, SUBLANES), :],
            gsem.at[dst_slot])

    def scatter_row(r, src_slot):
        return pltpu.make_async_copy(
            obuf[src_slot].at[pl.ds(r * SUBLANES, SUBLANES), :], h_out.at[token_tile(stok_ref, r), :],
            ssem.at[src_slot])

    def wait_gather(dst_slot):
        pltpu.make_async_copy(
            h_in.at[pl.ds(0, MOE_ROWS * SUBLANES), :], xbuf[dst_slot], gsem.at[dst_slot]).wait()

    def wait_scatter(src_slot):
        pltpu.make_async_copy(
            obuf[src_slot], h_out.at[pl.ds(0, MOE_ROWS * SUBLANES), :], ssem.at[src_slot]).wait()

    @pl.when(i == 0)
    def _():
        obuf2[...] = jnp.zeros_like(obuf2)

        def start(r, c):
            gather_row(tok0_ref, r, 0, blk=0).start()
            gather_row(tok0_ref, r, 1, blk=1).start()
            return c

        lax.fori_loop(0, MOE_ROWS, start, 0)

    def step(slot):
        ahead = (slot + 2) % MOE_BUFFERS
        behind = (slot - 1) % MOE_BUFFERS
        ea = ea_ref[i]
        eb = eb_ref[i]
        wait_gather(slot)

        @pl.when(i >= 2)
        def _():
            wait_scatter(slot)

        @pl.when(flag_ref[i] != 0)
        def _():
            dst = stage_ref[i] & (EXPERTS_PER_GROUP - 1)
            wg_c[dst] = wg_st[0, 0].astype(jnp.bfloat16)
            wu_c[dst] = wu_st[0, 0].astype(jnp.bfloat16)
            wd_c[dst] = wd_st[0, 0].astype(jnp.bfloat16)

        for e, changed, cur in ((ea, chg_ref[0, i], cur_a), (eb, chg_ref[1, i], cur_b)):
            @pl.when(changed != 0)
            def _(e=e, cur=cur):
                at = e & (EXPERTS_PER_GROUP - 1)
                cur[0][...] = wg_c[at]
                cur[1][...] = wu_c[at]
                cur[2][...] = wd_c[at]

        for r in range(MOE_ROWS):
            gather_row(gtok_ref, r, ahead).start()
        for r in range(MOE_ROWS):
            scatter_row(r, behind).start()

        x = _load_tokens(xbuf[slot], MOE_ROWS)
        xb = _rms(x, g_ffn_ref[...]).astype(jnp.bfloat16)
        logits = _router_logits(xb, wr_ref, br_ref)
        lane = lax.broadcasted_iota(jnp.int32, logits.shape, 1)
        neg = jnp.float32(-jnp.inf)
        glog = jnp.where(lane < N_GROUPS, logits, neg)
        gmax = jnp.max(glog, axis=-1, keepdims=True)
        gsum = jnp.sum(jnp.exp(glog - gmax), axis=-1, keepdims=True)
        grp = ea >> 3
        l_g = jnp.sum(jnp.where(lane == grp, logits, 0.0), axis=-1, keepdims=True)
        p_g = jnp.exp(l_g - gmax) / gsum
        l_a = jnp.sum(jnp.where(lane == N_GROUPS + ea, logits, 0.0), axis=-1, keepdims=True)
        l_b = jnp.sum(jnp.where(lane == N_GROUPS + eb, logits, 0.0), axis=-1, keepdims=True)
        w_a = p_g / (1.0 + jnp.exp(l_b - l_a))
        w_b = p_g / (1.0 + jnp.exp(l_a - l_b))

        def expert(cur):
            gate = jnp.dot(xb, cur[0][...], preferred_element_type=jnp.float32)
            up = jnp.dot(xb, cur[1][...], preferred_element_type=jnp.float32)
            hid = (jax.nn.silu(gate) * up).astype(jnp.bfloat16)
            return jnp.dot(hid, cur[2][...], preferred_element_type=jnp.float32)

        y = w_a * expert(cur_a) + w_b * expert(cur_b)
        _store_tokens(obuf[slot], x + y, MOE_ROWS)

        @pl.when(i == nblk)
        def _():
            wait_gather((slot + 1) % MOE_BUFFERS)
            wait_gather(ahead)
            wait_scatter((slot - 2) % MOE_BUFFERS)
            wait_scatter(behind)

    rotation = lax.rem(i, MOE_BUFFERS)
    for slot in range(MOE_BUFFERS):
        pl.when((i <= nblk) & (rotation == slot))(functools.partial(step, slot))


def _moe_call(h_all, tables, g_ffn, wr, br, w_gate, w_up, w_down, *, layer):
    tok0, gtok, stok, ea, eb, chg, stage, flag, nblk = tables
    const = lambda shape: pl.BlockSpec(shape, lambda i, *_: (0,) * len(shape))
    staged = lambda shape: pl.BlockSpec(
        (1, 1) + shape, lambda i, ea_, eb_, ch_, st_, fl_, nb_: (layer, st_[i], 0, 0))
    tok_spec = lambda imap: pl.BlockSpec((1, 1, MOE_ROWS), imap, memory_space=pltpu.SMEM)
    grid_spec = pltpu.PrefetchScalarGridSpec(
        num_scalar_prefetch=6,
        grid=(nblk[0] + 1,),
        in_specs=[
            pl.BlockSpec((2, 1, MOE_ROWS), lambda i, *_: (0, 0, 0), memory_space=pltpu.SMEM),
            tok_spec(lambda i, *_: (i, 0, 0)), tok_spec(lambda i, *_: (i, 0, 0)),
            pl.BlockSpec(memory_space=pl.ANY),
            const((1, D_MODEL)), const((D_MODEL, ROUTER_COLS)), const((1, ROUTER_COLS)),
            staged((D_MODEL, D_EXPERT)), staged((D_MODEL, D_EXPERT)), staged((D_EXPERT, D_MODEL)),
        ],
        out_specs=pl.BlockSpec(memory_space=pl.ANY),
        scratch_shapes=[
            *[pltpu.VMEM((MOE_ROWS * SUBLANES, LANES), jnp.float32)] * (2 * MOE_BUFFERS),
            pltpu.VMEM((EXPERTS_PER_GROUP, D_MODEL, D_EXPERT), jnp.bfloat16),
            pltpu.VMEM((EXPERTS_PER_GROUP, D_MODEL, D_EXPERT), jnp.bfloat16),
            pltpu.VMEM((EXPERTS_PER_GROUP, D_EXPERT, D_MODEL), jnp.bfloat16),
            pltpu.VMEM((D_MODEL, D_EXPERT), jnp.bfloat16),
            pltpu.VMEM((D_MODEL, D_EXPERT), jnp.bfloat16),
            pltpu.VMEM((D_EXPERT, D_MODEL), jnp.bfloat16),
            pltpu.VMEM((D_MODEL, D_EXPERT), jnp.bfloat16),
            pltpu.VMEM((D_MODEL, D_EXPERT), jnp.bfloat16),
            pltpu.VMEM((D_EXPERT, D_MODEL), jnp.bfloat16),
            pltpu.SemaphoreType.DMA((MOE_BUFFERS,)),
            pltpu.SemaphoreType.DMA((MOE_BUFFERS,)),
        ],
    )
    return pl.pallas_call(
        _moe_kernel,
        grid_spec=grid_spec,
        out_shape=jax.ShapeDtypeStruct(h_all.shape, h_all.dtype),
        input_output_aliases={9: 0},
        compiler_params=pltpu.CompilerParams(
            dimension_semantics=("arbitrary",), vmem_limit_bytes=VMEM_LIMIT),
        name="moe_pairs",
    )(ea, eb, chg, stage, flag, nblk, tok0, gtok, stok, h_all, g_ffn, wr, br, w_gate, w_up, w_down)


def _final_norm_kernel(h_ref, g_ref, o_ref):
    for first in range(0, NORM_TILE, MIX_TILE):
        o_ref[first:first + MIX_TILE, :] = _rms(_load_tokens(h_ref, MIX_TILE, first), g_ref[...])


def _final_norm_call(h_all, g, n_rows):
    return pl.pallas_call(
        _final_norm_kernel,
        grid=(n_rows // NORM_TILE,),
        in_specs=[pl.BlockSpec((NORM_TILE * SUBLANES, LANES), lambda i: (i, 0)),
                  pl.BlockSpec((1, D_MODEL), lambda i: (0, 0))],
        out_specs=pl.BlockSpec((NORM_TILE, D_MODEL), lambda i: (i, 0)),
        out_shape=jax.ShapeDtypeStruct((n_rows, D_MODEL), jnp.float32),
        compiler_params=pltpu.CompilerParams(dimension_semantics=("arbitrary",), vmem_limit_bytes=VMEM_LIMIT),
        name="final_norm",
    )(h_all, g)


def _pick(tables, idx):
    hit = idx[None, :, None] == jnp.arange(tables.shape[1], dtype=jnp.int32)[None, None, :]
    return jnp.sum(jnp.where(hit, tables[:, None, :], 0), axis=-1)


def _dispatch_tables(cls, n_blocks):
    n_all = cls.shape[0]
    tok_bits = (n_all - 1).bit_length()
    key = jnp.sort((cls << tok_bits) | jnp.arange(n_all, dtype=jnp.int32))
    order = key & ((1 << tok_bits) - 1)
    nib = jnp.arange(16, dtype=jnp.int32)[None, :]
    hi_hot = ((cls >> 4)[:, None] == nib).astype(jnp.bfloat16)
    lo_hot = ((cls & 15)[:, None] == nib).astype(jnp.bfloat16)
    counts = jnp.dot(hi_hot.T, lo_hot, preferred_element_type=jnp.float32).reshape(N_CLASS_IDS).astype(jnp.int32)
    start = jnp.cumsum(counts) - counts
    padded = (counts + MOE_ROWS - 1) // MOE_ROWS * MOE_ROWS
    pad_end = jnp.cumsum(padded)
    pad_start = pad_end - padded
    nblk = (pad_end[-1] // MOE_ROWS).astype(jnp.int32)

    kidx = jnp.arange(n_blocks, dtype=jnp.int32)
    real = kidx < nblk
    p0 = kidx * MOE_ROWS
    cb = jnp.sum((pad_end[None, :] <= p0[:, None]).astype(jnp.int32), axis=1)
    cb = jnp.minimum(cb, N_CLASS_IDS - 1)
    cls_pad_start, cls_count, cls_start = _pick(jnp.stack([pad_start, counts, start]), cb)
    r0 = p0 - cls_pad_start
    nv = jnp.where(real, jnp.clip(cls_count - r0, 0, MOE_ROWS), 0).astype(jnp.int32)
    grp = cb >> 6
    blk_a = grp * EXPERTS_PER_GROUP + ((cb >> 3) & 7)
    blk_b = grp * EXPERTS_PER_GROUP + (cb & 7)
    r = jnp.arange(MOE_ROWS, dtype=jnp.int32)[None, :]
    blk_src0 = cls_start + r0

    never = jnp.int32(n_blocks)
    experts = jnp.arange(N_EXPERTS, dtype=jnp.int32)[:, None]
    used = real[None, :] & ((blk_a[None, :] == experts) | (blk_b[None, :] == experts))
    first_use = jnp.min(jnp.where(used, kidx[None, :], never), axis=1)
    new_a = real & (_pick(first_use[None, :], blk_a)[0] == kidx)
    new_b = real & (_pick(first_use[None, :], blk_b)[0] == kidx)
    filler = new_a & new_b
    blk_step = kidx + jnp.cumsum(filler.astype(jnp.int32))
    n_act = nblk + jnp.sum(filler.astype(jnp.int32))

    n_steps = n_blocks + N_EXPERTS // 2 + 1
    step = jnp.arange(n_steps, dtype=jnp.int32)
    k_done = jnp.sum((real[None, :] & (blk_step[None, :] <= step[:, None])).astype(jnp.int32), axis=1) - 1
    kc = jnp.clip(k_done, 0, n_blocks - 1)
    kn = jnp.clip(k_done + 1, 0, n_blocks - 1)
    active = step < n_act
    per_block = jnp.stack([blk_step, nv, blk_src0, blk_a, blk_b, new_a.astype(jnp.int32), new_b.astype(jnp.int32)])
    c_step, c_nv, c_src0, c_a, c_b, c_new_a, c_new_b = _pick(per_block, kc)
    n_a = _pick(blk_a[None, :], kn)[0]
    is_block = active & (k_done >= 0) & (c_step == step)
    is_filler = active & ~is_block

    spare = n_all + (step[:, None] & (PAD_BANKS - 1)) * MOE_ROWS + r
    src = jnp.clip(c_src0[:, None] + r, 0, n_all - 1)
    rows = jnp.where(is_block[:, None] & (r < c_nv[:, None]), order[src], spare).astype(jnp.int32)
    last = jnp.clip(nblk - 1, 0, n_blocks - 1)
    ea = jnp.where(is_block, c_a, jnp.where(is_filler, n_a, blk_a[last]))
    eb = jnp.where(is_block, c_b, jnp.where(is_filler, n_a, blk_b[last]))
    staged_here = jnp.where(
        is_block, jnp.where(c_new_b > 0, c_b, jnp.where(c_new_a > 0, c_a, -1)),
        jnp.where(is_filler, n_a, -1))
    flag = (staged_here >= 0).astype(jnp.int32)
    far = jnp.int32(n_steps)
    next_evt = lax.cummin(jnp.where(flag > 0, step, far), axis=0, reverse=True)
    last_evt = jnp.max(jnp.where(flag > 0, step, 0))
    stage = _pick(staged_here[None, :], jnp.where(next_evt < far, next_evt, last_evt))[0]

    sink = n_all + SINK_BANK * MOE_ROWS + r
    source = n_all + SOURCE_BANK * MOE_ROWS + r
    nxt = jnp.concatenate([rows[2:], rows[-2:]], axis=0)
    prv = jnp.concatenate([rows[:1], rows[:-1]], axis=0)
    gtok = jnp.where((step + 2 < n_act)[:, None], nxt, source)
    stok = jnp.where(((step >= 1) & (step <= n_act))[:, None], prv, sink)
    shape3 = (n_steps, 1, MOE_ROWS)
    rows, gtok, stok = rows * SUBLANES, gtok * SUBLANES, stok * SUBLANES
    chg = jnp.stack([jnp.concatenate([jnp.ones((1,), jnp.int32), (e[1:] != e[:-1]).astype(jnp.int32)])
                     for e in (ea, eb)])
    return (rows[:2].reshape(2, 1, MOE_ROWS), gtok.reshape(shape3), stok.reshape(shape3), ea, eb, chg,
            stage, flag, n_act.reshape(1).astype(jnp.int32))


def _over_sublanes(w):
    return jnp.broadcast_to(w[:, None, :], (w.shape[0], SUBLANES, w.shape[1]))


def kernel(x, meta_tokens, norm_mix_g, w_in, conf_dw_w, conf_dw_b, conf_ln_g, conf_ln_b, sc_conv_w, w_out,
           norm_ffn_g, w_router_group, b_router_group, w_router_expert, b_router_expert, w_exp_gate,
           w_exp_up, w_exp_down, final_norm_g):
    bsz, seq, d = x.shape
    depth = w_in.shape[0]
    n_main = bsz * seq
    n_all = n_main + N_META
    n_blocks = n_all // MOE_ROWS + N_CLASSES
    f32 = jnp.float32

    zero_halo = (jnp.zeros((HALO_A, CONF_W), f32), jnp.zeros((HALO_C, SC_W), f32))
    assert d == SUBLANES * LANES
    h_all = None

    for l in range(depth):
        w_r = jnp.concatenate([w_router_group[l], w_router_expert[l]], axis=1).astype(f32)
        w_r = jnp.pad(w_r, ((0, 0), (0, ROUTER_COLS - w_r.shape[1])))
        w_r = w_r.astype(jnp.bfloat16)
        b_r = jnp.concatenate([b_router_group[l], b_router_expert[l]]).astype(f32)
        b_r = jnp.pad(b_r, (0, ROUTER_COLS - b_r.shape[0])).reshape(1, ROUTER_COLS)
        g_ffn = norm_ffn_g[l].reshape(1, d)
        wts = (norm_mix_g[l].reshape(1, d), w_in[l].astype(jnp.bfloat16), _over_sublanes(conf_dw_w[l]),
               conf_dw_b[l].reshape(1, CONF_W), conf_ln_g[l].reshape(1, CONF_W),
               conf_ln_b[l].reshape(1, CONF_W), _over_sublanes(sc_conv_w[l]), w_out[l].astype(jnp.bfloat16),
               g_ffn, w_r, b_r)

        main = dict(row0=0, n_batch=bsz, n_tiles=seq // MIX_TILE, tt=MIX_TILE)
        if h_all is None:
            h_meta, cls_meta, halo_a, halo_c = _mixer_call(
                meta_tokens.astype(x.dtype), *zero_halo, wts, row0=0, n_batch=1, n_tiles=1, tt=N_META,
                fresh="rows")
            tail = jnp.concatenate([h_meta, jnp.zeros((SPARE_ROWS, d), x.dtype)], axis=0)
            h_all, cls_main, _, _ = _mixer_call(
                x.reshape(n_main, d), halo_a[0], halo_c[0], wts, fresh="tokens", tail=tail, **main)
        else:
            h_all, cls_meta, halo_a, halo_c = _mixer_call(
                h_all, *zero_halo, wts, row0=n_main, n_batch=1, n_tiles=1, tt=N_META)
            h_all, cls_main, _, _ = _mixer_call(h_all, halo_a[0], halo_c[0], wts, **main)

        cls = jnp.concatenate([cls_main[:, 0], cls_meta[:, 0]])
        h_all = _moe_call(h_all, _dispatch_tables(cls, n_blocks), g_ffn, w_r, b_r,
                          w_exp_gate, w_exp_up, w_exp_down, layer=l)

    out = _final_norm_call(h_all, final_norm_g.reshape(1, d), n_main)
    return out.reshape(bsz, seq, d)
```

```python
import functools

import jax
import jax.numpy as jnp
from jax import lax
from jax.experimental import pallas as pl
from jax.experimental.pallas import tpu as pltpu

D_MODEL = 1024
N_META = 16
CONF_W = 512
SC_W = 512
PROJ_W = 2 * CONF_W + 3 * SC_W
CONF_KERNEL = 31
SC_KERNEL = 3
N_GROUPS = 4
EXPERTS_PER_GROUP = 8
N_EXPERTS = N_GROUPS * EXPERTS_PER_GROUP
D_EXPERT = 512
EPS = 1e-6

LANES = 128
SUBLANES = 8
HALO_A = 32
HALO_C = 8
CONV_ROWS = 32
MIX_TILE = 1024
NORM_TILE = 2048
MOE_ROWS = 256
N_CLASS_IDS = N_GROUPS * 64
N_CLASSES = N_GROUPS * (EXPERTS_PER_GROUP * (EXPERTS_PER_GROUP - 1) // 2)
VMEM_LIMIT = 56 * 1024 * 1024
ROUTER_COLS = LANES
MOE_BUFFERS = 3
PAD_BANKS = 8
SINK_BANK = PAD_BANKS
SOURCE_BANK = PAD_BANKS + 1
TAIL_TILES = 3
SPARE_ROWS = TAIL_TILES * MIX_TILE - N_META
assert SPARE_ROWS >= (PAD_BANKS + 2) * MOE_ROWS


def _load_tokens(ref, n, first=0):
    return jnp.concatenate(
        [ref[pl.ds(first * SUBLANES + s, n, stride=SUBLANES), :] for s in range(SUBLANES)], axis=-1)


def _store_tokens(ref, x, n):
    for s in range(SUBLANES):
        ref[pl.ds(s, n, stride=SUBLANES), :] = x[:, s * LANES:(s + 1) * LANES]


def _rms(x, g):
    return x * lax.rsqrt(jnp.mean(x * x, axis=-1, keepdims=True) + EPS) * g


def _router_logits(hn_bf16, w_ref, b_ref):
    return jnp.dot(hn_bf16, w_ref[...], preferred_element_type=jnp.float32) + b_ref[...]


N_MIXER_INPUTS = 14


def _mixer_kernel(h_ref, *refs, tt, src_tm, dst_tm, n_batch, has_tail):
    if not has_tail:
        return _mixer_tile(h_ref, *refs, tt=tt, src_tm=src_tm, dst_tm=dst_tm)
    tail_ref, refs = refs[0], refs[1:]
    b = pl.program_id(0)
    pl.when(b < n_batch)(lambda: _mixer_tile(h_ref, *refs, tt=tt, src_tm=src_tm, dst_tm=dst_tm))

    @pl.when(b == n_batch)
    def _():
        hout_ref = refs[N_MIXER_INPUTS - 1]
        _store_tokens(hout_ref, tail_ref[...], tt)


def _mixer_tile(h_ref, halo_a_ref, halo_c_ref, g_mix_ref, w_in_ref, dw_w_ref, dw_b_ref, ln_g_ref,
                ln_b_ref, sc_w_ref, w_out_ref, g_ffn_ref, wr_ref, br_ref,
                hout_ref, cls_ref, halo_a_out, halo_c_out,
                aext, ashift, cext, sb_buf, ybuf, *, tt, src_tm, dst_tm):
    j = pl.program_id(1)

    @pl.when(j == 0)
    def _():
        aext[0:HALO_A, :] = halo_a_ref[...]
        cext[0:HALO_C, :] = halo_c_ref[...]

    h = _load_tokens(h_ref, tt) if src_tm else h_ref[...]
    xn = _rms(h, g_mix_ref[...]).astype(jnp.bfloat16)
    ug = jnp.dot(xn, w_in_ref[:, 0:2 * CONF_W], preferred_element_type=jnp.float32)
    aext[HALO_A:HALO_A + tt, :] = ug[:, 0:CONF_W] * jax.nn.sigmoid(ug[:, CONF_W:2 * CONF_W])
    bcv = jnp.dot(xn, w_in_ref[:, 2 * CONF_W:PROJ_W], preferred_element_type=jnp.float32)
    sb_buf[...] = bcv[:, 0:SC_W]
    cext[HALO_C:HALO_C + tt, :] = bcv[:, SC_W:2 * SC_W] * bcv[:, 2 * SC_W:3 * SC_W]

    n_shift_rows = HALO_A + tt - SUBLANES
    for p in range(1, SUBLANES):
        ashift[p - 1, 0:n_shift_rows, :] = aext[p:p + n_shift_rows, :]

    ch = min(tt, CONV_ROWS)
    tiles = (ch // SUBLANES, SUBLANES, CONF_W)
    for c0 in range(0, tt, ch):
        acc = jnp.broadcast_to(dw_b_ref[...], tiles)
        for k in range(CONF_KERNEL):
            row = HALO_A - (CONF_KERNEL - 1) + c0 + k
            phase = row % SUBLANES
            if phase == 0:
                tap = aext[row:row + ch, :]
            else:
                tap = ashift[phase - 1, row - phase:row - phase + ch, :]
            acc = acc + tap.reshape(tiles) * dw_w_ref[k]
        acc = acc.reshape(ch, CONF_W)
        mu = jnp.mean(acc, axis=-1, keepdims=True)
        xc = acc - mu
        ln = xc * lax.rsqrt(jnp.mean(xc * xc, axis=-1, keepdims=True) + EPS) * ln_g_ref[...] + ln_b_ref[...]
        ybuf[c0:c0 + ch, 0:CONF_W] = jax.nn.silu(ln).astype(jnp.bfloat16)
        sacc = None
        for k in range(SC_KERNEL):
            row = HALO_C - (SC_KERNEL - 1) + c0 + k
            term = cext[row:row + ch, :].reshape(tiles) * sc_w_ref[k]
            sacc = term if sacc is None else sacc + term
        ybuf[c0:c0 + ch, CONF_W:CONF_W + SC_W] = (
            sb_buf[c0:c0 + ch, :] * sacc.reshape(ch, SC_W)).astype(jnp.bfloat16)

    new_halo_a = aext[tt:tt + HALO_A, :]
    new_halo_c = cext[tt:tt + HALO_C, :]
    aext[0:HALO_A, :] = new_halo_a
    cext[0:HALO_C, :] = new_halo_c
    halo_a_out[0] = new_halo_a
    halo_c_out[0] = new_halo_c

    h_new = h + jnp.dot(ybuf[...], w_out_ref[...], preferred_element_type=jnp.float32)
    if dst_tm:
        _store_tokens(hout_ref, h_new, tt)
    else:
        hout_ref[...] = h_new

    logits = _router_logits(_rms(h_new, g_ffn_ref[...]).astype(jnp.bfloat16), wr_ref, br_ref)
    lane = lax.broadcasted_iota(jnp.int32, logits.shape, 1)
    neg = jnp.float32(-jnp.inf)
    big = jnp.int32(1 << 20)
    glog = jnp.where(lane < N_GROUPS, logits, neg)
    gmax = jnp.max(glog, axis=-1, keepdims=True)
    g_idx = jnp.min(jnp.where(glog == gmax, lane, big), axis=-1, keepdims=True)
    el = lane - N_GROUPS
    in_group = (el >= 0) & (el < N_EXPERTS) & ((el >> 3) == g_idx)
    l1 = jnp.where(in_group, logits, neg)
    m1 = jnp.max(l1, axis=-1, keepdims=True)
    i1 = jnp.min(jnp.where(l1 == m1, el, big), axis=-1, keepdims=True)
    l2 = jnp.where(el == i1, neg, l1)
    m2 = jnp.max(l2, axis=-1, keepdims=True)
    i2 = jnp.min(jnp.where(l2 == m2, el, big), axis=-1, keepdims=True)
    lo_e = jnp.minimum(i1, i2) & (EXPERTS_PER_GROUP - 1)
    hi_e = jnp.maximum(i1, i2) & (EXPERTS_PER_GROUP - 1)
    cls_ref[...] = g_idx * 64 + lo_e * 8 + hi_e


def _mixer_call(h_src, halo_a, halo_c, wts, *, row0, n_batch, n_tiles, tt, fresh=None, tail=None):
    blk0 = row0 // tt
    n_rows = n_batch * n_tiles * tt
    last = n_batch * n_tiles - 1
    tail_tiles = 0 if tail is None else tail.shape[0] // tt
    const = lambda shape: pl.BlockSpec(shape, lambda b, j: (0,) * len(shape))
    tile = lambda b, j: jnp.minimum(b * n_tiles + j, last)
    tm_spec = pl.BlockSpec((tt * SUBLANES, LANES), lambda b, j: (blk0 + b * n_tiles + j, 0))
    row_spec = pl.BlockSpec((tt, D_MODEL), lambda b, j: (blk0 + tile(b, j), 0))
    if fresh is None:
        in_spec, out_spec, out_struct = tm_spec, tm_spec, jax.ShapeDtypeStruct(h_src.shape, jnp.float32)
    elif fresh == "rows":
        in_spec, out_spec, out_struct = row_spec, row_spec, jax.ShapeDtypeStruct(h_src.shape, jnp.float32)
    else:
        in_spec = row_spec
        tail_at = lambda b, j: jnp.where(b == n_batch, jnp.minimum(j, tail_tiles - 1), j)
        out_spec = pl.BlockSpec((tt * SUBLANES, LANES), lambda b, j: (b * n_tiles + tail_at(b, j), 0))
        out_struct = jax.ShapeDtypeStruct(((n_rows + tail.shape[0]) * SUBLANES, LANES), jnp.float32)
    tail_specs, tail_args = [], []
    if tail is not None:
        tail_specs = [pl.BlockSpec(
            (tt, D_MODEL), lambda b, j: (jnp.where(b == n_batch, jnp.minimum(j, tail_tiles - 1), 0), 0))]
        tail_args = [tail]
    return pl.pallas_call(
        functools.partial(_mixer_kernel, tt=tt, src_tm=fresh is None, dst_tm=fresh != "rows",
                          n_batch=n_batch, has_tail=tail is not None),
        grid=(n_batch + (tail is not None), n_tiles),
        in_specs=[
            in_spec, *tail_specs,
            const((HALO_A, CONF_W)), const((HALO_C, SC_W)),
            const((1, D_MODEL)), const((D_MODEL, PROJ_W)),
            const((CONF_KERNEL, SUBLANES, CONF_W)), const((1, CONF_W)), const((1, CONF_W)), const((1, CONF_W)),
            const((SC_KERNEL, SUBLANES, SC_W)), const((CONF_W + SC_W, D_MODEL)), const((1, D_MODEL)),
            const((D_MODEL, ROUTER_COLS)), const((1, ROUTER_COLS)),
        ],
        out_specs=[
            out_spec,
            pl.BlockSpec((tt, 1), lambda b, j: (tile(b, j), 0)),
            pl.BlockSpec((1, HALO_A, CONF_W), lambda b, j: (jnp.minimum(b, n_batch - 1), 0, 0)),
            pl.BlockSpec((1, HALO_C, SC_W), lambda b, j: (jnp.minimum(b, n_batch - 1), 0, 0)),
        ],
        out_shape=[
            out_struct,
            jax.ShapeDtypeStruct((n_rows, 1), jnp.int32),
            jax.ShapeDtypeStruct((n_batch, HALO_A, CONF_W), jnp.float32),
            jax.ShapeDtypeStruct((n_batch, HALO_C, SC_W), jnp.float32),
        ],
        scratch_shapes=[
            pltpu.VMEM((HALO_A + tt, CONF_W), jnp.float32),
            pltpu.VMEM((SUBLANES - 1, HALO_A + tt, CONF_W), jnp.float32),
            pltpu.VMEM((HALO_C + tt, SC_W), jnp.float32),
            pltpu.VMEM((tt, SC_W), jnp.float32),
            pltpu.VMEM((tt, CONF_W + SC_W), jnp.bfloat16),
        ],
        input_output_aliases={0: 0} if fresh is None else {},
        compiler_params=pltpu.CompilerParams(
            dimension_semantics=("arbitrary", "arbitrary"), vmem_limit_bytes=VMEM_LIMIT),
        name=f"mixer_t{tt}",
    )(h_src, *tail_args, halo_a, halo_c, *wts)


def _moe_kernel(ea_ref, eb_ref, chg_ref, stage_ref, flag_ref, nblk_ref,
                tok0_ref, gtok_ref, stok_ref, h_in, g_ffn_ref, wr_ref, br_ref,
                wg_st, wu_st, wd_st,
                h_out,
                xbuf0, xbuf1, xbuf2, obuf0, obuf1, obuf2, wg_c, wu_c, wd_c, wga, wua, wda, wgb, wub, wdb,
                gsem, ssem):
    i = pl.program_id(0)
    nblk = nblk_ref[0]
    xbuf = (xbuf0, xbuf1, xbuf2)
    obuf = (obuf0, obuf1, obuf2)
    cur_a = (wga, wua, wda)
    cur_b = (wgb, wub, wdb)

    def token_tile(tok_ref, r, blk=0):
        return pl.ds(pl.multiple_of(tok_ref[blk, 0, r], SUBLANES), SUBLANES)

    def gather_row(tok_ref, r, dst_slot, blk=0):
        return pltpu.make_async_copy(
            h_in.at[token_tile(tok_ref, r, blk), :], xbuf[dst_slot].at[pl.ds(r * SUBLANES, SUBLANES), :],
            gsem.at[dst_slot])

    def scatter_row(r, src_slot):
        return pltpu.make_async_copy(
            obuf[src_slot].at[pl.ds(r * SUBLANES, SUBLANES), :], h_out.at[token_tile(stok_ref, r), :],
            ssem.at[src_slot])

    def wait_gather(dst_slot):
        pltpu.make_async_copy(
            h_in.at[pl.ds(0, MOE_ROWS * SUBLANES), :], xbuf[dst_slot], gsem.at[dst_slot]).wait()

    def wait_scatter(src_slot):
        pltpu.make_async_copy(
            obuf[src_slot], h_out.at[pl.ds(0, MOE_ROWS * SUBLANES), :], ssem.at[src_slot]).wait()

    @pl.when(i == 0)
    def _():
        obuf2[...] = jnp.zeros_like(obuf2)

        def start(r, c):
            gather_row(tok0_ref, r, 0, blk=0).start()
            gather_row(tok0_ref, r, 1, blk=1).start()
            return c

        lax.fori_loop(0, MOE_ROWS, start, 0)

    def step(slot):
        ahead = (slot + 2) % MOE_BUFFERS
        behind = (slot - 1) % MOE_BUFFERS
        ea = ea_ref[i]
        eb = eb_ref[i]
        wait_gather(slot)

        @pl.when(i >= 2)
        def _():
            wait_scatter(slot)

        @pl.when(flag_ref[i] != 0)
        def _():
            dst = stage_ref[i] & (EXPERTS_PER_GROUP - 1)
            wg_c[dst] = wg_st[0, 0].astype(jnp.bfloat16)
            wu_c[dst] = wu_st[0, 0].astype(jnp.bfloat16)
            wd_c[dst] = wd_st[0, 0].astype(jnp.bfloat16)

        for e, changed, cur in ((ea, chg_ref[0, i], cur_a), (eb, chg_ref[1, i], cur_b)):
            @pl.when(changed != 0)
            def _(e=e, cur=cur):
                at = e & (EXPERTS_PER_GROUP - 1)
                cur[0][...] = wg_c[at]
                cur[1][...] = wu_c[at]
                cur[2][...] = wd_c[at]

        for r in range(MOE_ROWS):
            gather_row(gtok_ref, r, ahead).start()
        for r in range(MOE_ROWS):
            scatter_row(r, behind).start()

        x = _load_tokens(xbuf[slot], MOE_ROWS)
        xb = _rms(x, g_ffn_ref[...]).astype(jnp.bfloat16)
        logits = _router_logits(xb, wr_ref, br_ref)
        lane = lax.broadcasted_iota(jnp.int32, logits.shape, 1)
        neg = jnp.float32(-jnp.inf)
        glog = jnp.where(lane < N_GROUPS, logits, neg)
        gmax = jnp.max(glog, axis=-1, keepdims=True)
        gsum = jnp.sum(jnp.exp(glog - gmax), axis=-1, keepdims=True)
        grp = ea >> 3
        l_g = jnp.sum(jnp.where(lane == grp, logits, 0.0), axis=-1, keepdims=True)
        p_g = jnp.exp(l_g - gmax) / gsum
        l_a = jnp.sum(jnp.where(lane == N_GROUPS + ea, logits, 0.0), axis=-1, keepdims=True)
        l_b = jnp.sum(jnp.where(lane == N_GROUPS + eb, logits, 0.0), axis=-1, keepdims=True)
        w_a = p_g / (1.0 + jnp.exp(l_b - l_a))
        w_b = p_g / (1.0 + jnp.exp(l_a - l_b))

        def expert(cur):
            gate = jnp.dot(xb, cur[0][...], preferred_element_type=jnp.float32)
            up = jnp.dot(xb, cur[1][...], preferred_element_type=jnp.float32)
            hid = (jax.nn.silu(gate) * up).astype(jnp.bfloat16)
            return jnp.dot(hid, cur[2][...], preferred_element_type=jnp.float32)

        y = w_a * expert(cur_a) + w_b * expert(cur_b)
        _store_tokens(obuf[slot], x + y, MOE_ROWS)

        @pl.when(i == nblk)
        def _():
            wait_gather((slot + 1) % MOE_BUFFERS)
            wait_gather(ahead)
            wait_scatter((slot - 2) % MOE_BUFFERS)
            wait_scatter(behind)

    rotation = lax.rem(i, MOE_BUFFERS)
    for slot in range(MOE_BUFFERS):
        pl.when((i <= nblk) & (rotation == slot))(functools.partial(step, slot))


def _moe_call(h_all, tables, g_ffn, wr, br, w_gate, w_up, w_down, *, layer):
    tok0, gtok, stok, ea, eb, chg, stage, flag, nblk = tables
    const = lambda shape: pl.BlockSpec(shape, lambda i, *_: (0,) * len(shape))
    staged = lambda shape: pl.BlockSpec(
        (1, 1) + shape, lambda i, ea_, eb_, ch_, st_, fl_, nb_: (layer, st_[i], 0, 0))
    tok_spec = lambda imap: pl.BlockSpec((1, 1, MOE_ROWS), imap, memory_space=pltpu.SMEM)
    grid_spec = pltpu.PrefetchScalarGridSpec(
        num_scalar_prefetch=6,
        grid=(nblk[0] + 1,),
        in_specs=[
            pl.BlockSpec((2, 1, MOE_ROWS), lambda i, *_: (0, 0, 0), memory_space=pltpu.SMEM),
            tok_spec(lambda i, *_: (i, 0, 0)), tok_spec(lambda i, *_: (i, 0, 0)),
            pl.BlockSpec(memory_space=pl.ANY),
            const((1, D_MODEL)), const((D_MODEL, ROUTER_COLS)), const((1, ROUTER_COLS)),
            staged((D_MODEL, D_EXPERT)), staged((D_MODEL, D_EXPERT)), staged((D_EXPERT, D_MODEL)),
        ],
        out_specs=pl.BlockSpec(memory_space=pl.ANY),
        scratch_shapes=[
            *[pltpu.VMEM((MOE_ROWS * SUBLANES, LANES), jnp.float32)] * (2 * MOE_BUFFERS),
            pltpu.VMEM((EXPERTS_PER_GROUP, D_MODEL, D_EXPERT), jnp.bfloat16),
            pltpu.VMEM((EXPERTS_PER_GROUP, D_MODEL, D_EXPERT), jnp.bfloat16),
            pltpu.VMEM((EXPERTS_PER_GROUP, D_EXPERT, D_MODEL), jnp.bfloat16),
            pltpu.VMEM((D_MODEL, D_EXPERT), jnp.bfloat16),
            pltpu.VMEM((D_MODEL, D_EXPERT), jnp.bfloat16),
            pltpu.VMEM((D_EXPERT, D_MODEL), jnp.bfloat16),
            pltpu.VMEM((D_MODEL, D_EXPERT), jnp.bfloat16),
            pltpu.VMEM((D_MODEL, D_EXPERT), jnp.bfloat16),
            pltpu.VMEM((D_EXPERT, D_MODEL), jnp.bfloat16),
            pltpu.SemaphoreType.DMA((MOE_BUFFERS,)),
            pltpu.SemaphoreType.DMA((MOE_BUFFERS,)),
        ],
    )
    return pl.pallas_call(
        _moe_kernel,
        grid_spec=grid_spec,
        out_shape=jax.ShapeDtypeStruct(h_all.shape, h_all.dtype),
        input_output_aliases={9: 0},
        compiler_params=pltpu.CompilerParams(
            dimension_semantics=("arbitrary",), vmem_limit_bytes=VMEM_LIMIT),
        name="moe_pairs",
    )(ea, eb, chg, stage, flag, nblk, tok0, gtok, stok, h_all, g_ffn, wr, br, w_gate, w_up, w_down)


def _final_norm_kernel(h_ref, g_ref, o_ref):
    for first in range(0, NORM_TILE, MIX_TILE):
        o_ref[first:first + MIX_TILE, :] = _rms(_load_tokens(h_ref, MIX_TILE, first), g_ref[...])


def _final_norm_call(h_all, g, n_rows):
    return pl.pallas_call(
        _final_norm_kernel,
        grid=(n_rows // NORM_TILE,),
        in_specs=[pl.BlockSpec((NORM_TILE * SUBLANES, LANES), lambda i: (i, 0)),
                  pl.BlockSpec((1, D_MODEL), lambda i: (0, 0))],
        out_specs=pl.BlockSpec((NORM_TILE, D_MODEL), lambda i: (i, 0)),
        out_shape=jax.ShapeDtypeStruct((n_rows, D_MODEL), jnp.float32),
        compiler_params=pltpu.CompilerParams(dimension_semantics=("arbitrary",), vmem_limit_bytes=VMEM_LIMIT),
        name="final_norm",
    )(h_all, g)


def _pick(tables, idx):
    hit = idx[None, :, None] == jnp.arange(tables.shape[1], dtype=jnp.int32)[None, None, :]
    return jnp.sum(jnp.where(hit, tables[:, None, :], 0), axis=-1)


def _dispatch_tables(cls, cls_meta, n_blocks):
    n_main, n_meta = cls.shape[0], cls_meta.shape[0]
    n_all = n_main + n_meta
    tok_bits = (n_main - 1).bit_length()
    order = jnp.sort((cls << tok_bits) | jnp.arange(n_main, dtype=jnp.int32)) & ((1 << tok_bits) - 1)
    nib = jnp.arange(16, dtype=jnp.int32)[None, :]
    hi_hot = ((cls >> 4)[:, None] == nib).astype(jnp.bfloat16)
    lo_hot = ((cls & 15)[:, None] == nib).astype(jnp.bfloat16)
    counts_main = jnp.dot(
        hi_hot.T, lo_hot, preferred_element_type=jnp.float32).reshape(N_CLASS_IDS).astype(jnp.int32)
    ids = jnp.arange(N_CLASS_IDS, dtype=jnp.int32)
    counts_meta = jnp.sum((cls_meta[:, None] == ids[None, :]).astype(jnp.int32), axis=0)
    meta_rank = jnp.sum(jnp.tril(cls_meta[:, None] == cls_meta[None, :], -1).astype(jnp.int32), axis=1)
    counts = counts_main + counts_meta
    start = jnp.cumsum(counts_main) - counts_main
    padded = (counts + MOE_ROWS - 1) // MOE_ROWS * MOE_ROWS
    pad_end = jnp.cumsum(padded)
    pad_start = pad_end - padded
    nblk = (pad_end[-1] // MOE_ROWS).astype(jnp.int32)

    kidx = jnp.arange(n_blocks, dtype=jnp.int32)
    real = kidx < nblk
    p0 = kidx * MOE_ROWS
    cb = jnp.sum((pad_end[None, :] <= p0[:, None]).astype(jnp.int32), axis=1)
    cb = jnp.minimum(cb, N_CLASS_IDS - 1)
    cls_pad_start, cls_count, cls_start, cls_main = _pick(jnp.stack([pad_start, counts, start, counts_main]), cb)
    r0 = p0 - cls_pad_start
    nv = jnp.where(real, jnp.clip(cls_count - r0, 0, MOE_ROWS), 0).astype(jnp.int32)
    grp = cb >> 6
    blk_a = grp * EXPERTS_PER_GROUP + ((cb >> 3) & 7)
    blk_b = grp * EXPERTS_PER_GROUP + (cb & 7)
    r = jnp.arange(MOE_ROWS, dtype=jnp.int32)[None, :]
    blk_src0 = cls_start + r0
    blk_n_main = jnp.clip(cls_main - r0, 0, MOE_ROWS)
    blk_rank0 = r0 - cls_main

    never = jnp.int32(n_blocks)
    experts = jnp.arange(N_EXPERTS, dtype=jnp.int32)[:, None]
    used = real[None, :] & ((blk_a[None, :] == experts) | (blk_b[None, :] == experts))
    first_use = jnp.min(jnp.where(used, kidx[None, :], never), axis=1)
    new_a = real & (_pick(first_use[None, :], blk_a)[0] == kidx)
    new_b = real & (_pick(first_use[None, :], blk_b)[0] == kidx)
    filler = new_a & new_b
    blk_step = kidx + jnp.cumsum(filler.astype(jnp.int32))
    n_act = nblk + jnp.sum(filler.astype(jnp.int32))

    n_steps = n_blocks + N_EXPERTS // 2 + 1
    step = jnp.arange(n_steps, dtype=jnp.int32)
    k_done = jnp.sum((real[None, :] & (blk_step[None, :] <= step[:, None])).astype(jnp.int32), axis=1) - 1
    kc = jnp.clip(k_done, 0, n_blocks - 1)
    kn = jnp.clip(k_done + 1, 0, n_blocks - 1)
    active = step < n_act
    per_block = jnp.stack([blk_step, nv, blk_src0, blk_a, blk_b, new_a.astype(jnp.int32), new_b.astype(jnp.int32),
                           blk_n_main, blk_rank0, cb])
    c_step, c_nv, c_src0, c_a, c_b, c_new_a, c_new_b, c_n_main, c_rank0, c_cls = _pick(per_block, kc)
    n_a = _pick(blk_a[None, :], kn)[0]
    is_block = active & (k_done >= 0) & (c_step == step)
    is_filler = active & ~is_block

    spare = n_all + (step[:, None] & (PAD_BANKS - 1)) * MOE_ROWS + r
    src = jnp.clip(c_src0[:, None] + r, 0, n_main - 1)
    tok = order[src]
    rank = c_rank0[:, None] + r
    for j in range(n_meta):
        tok = jnp.where((c_cls[:, None] == cls_meta[j]) & (rank == meta_rank[j]), n_main + j, tok)
    rows = jnp.where(is_block[:, None] & (r < c_nv[:, None]), tok, spare).astype(jnp.int32)
    last = jnp.clip(nblk - 1, 0, n_blocks - 1)
    ea = jnp.where(is_block, c_a, jnp.where(is_filler, n_a, blk_a[last]))
    eb = jnp.where(is_block, c_b, jnp.where(is_filler, n_a, blk_b[last]))
    staged_here = jnp.where(
        is_block, jnp.where(c_new_b > 0, c_b, jnp.where(c_new_a > 0, c_a, -1)),
        jnp.where(is_filler, n_a, -1))
    flag = (staged_here >= 0).astype(jnp.int32)
    far = jnp.int32(n_steps)
    next_evt = lax.cummin(jnp.where(flag > 0, step, far), axis=0, reverse=True)
    last_evt = jnp.max(jnp.where(flag > 0, step, 0))
    stage = _pick(staged_here[None, :], jnp.where(next_evt < far, next_evt, last_evt))[0]

    sink = n_all + SINK_BANK * MOE_ROWS + r
    source = n_all + SOURCE_BANK * MOE_ROWS + r
    nxt = jnp.concatenate([rows[2:], rows[-2:]], axis=0)
    prv = jnp.concatenate([rows[:1], rows[:-1]], axis=0)
    gtok = jnp.where((step + 2 < n_act)[:, None], nxt, source)
    stok = jnp.where(((step >= 1) & (step <= n_act))[:, None], prv, sink)
    shape3 = (n_steps, 1, MOE_ROWS)
    rows, gtok, stok = rows * SUBLANES, gtok * SUBLANES, stok * SUBLANES
    chg = jnp.stack([jnp.concatenate([jnp.ones((1,), jnp.int32), (e[1:] != e[:-1]).astype(jnp.int32)])
                     for e in (ea, eb)])
    return (rows[:2].reshape(2, 1, MOE_ROWS), gtok.reshape(shape3), stok.reshape(shape3), ea, eb, chg,
            stage, flag, n_act.reshape(1).astype(jnp.int32))


def _over_sublanes(w):
    return jnp.broadcast_to(w[:, None, :], (w.shape[0], SUBLANES, w.shape[1]))


def kernel(x, meta_tokens, norm_mix_g, w_in, conf_dw_w, conf_dw_b, conf_ln_g, conf_ln_b, sc_conv_w, w_out,
           norm_ffn_g, w_router_group, b_router_group, w_router_expert, b_router_expert, w_exp_gate,
           w_exp_up, w_exp_down, final_norm_g):
    bsz, seq, d = x.shape
    depth = w_in.shape[0]
    n_main = bsz * seq
    n_all = n_main + N_META
    n_blocks = n_all // MOE_ROWS + N_CLASSES
    f32 = jnp.float32

    zero_halo = (jnp.zeros((HALO_A, CONF_W), f32), jnp.zeros((HALO_C, SC_W), f32))
    assert d == SUBLANES * LANES
    h_all = None

    for l in range(depth):
        w_r = jnp.concatenate([w_router_group[l], w_router_expert[l]], axis=1).astype(f32)
        w_r = jnp.pad(w_r, ((0, 0), (0, ROUTER_COLS - w_r.shape[1])))
        w_r = w_r.astype(jnp.bfloat16)
        b_r = jnp.concatenate([b_router_group[l], b_router_expert[l]]).astype(f32)
        b_r = jnp.pad(b_r, (0, ROUTER_COLS - b_r.shape[0])).reshape(1, ROUTER_COLS)
        g_ffn = norm_ffn_g[l].reshape(1, d)
        wts = (norm_mix_g[l].reshape(1, d), w_in[l].astype(jnp.bfloat16), _over_sublanes(conf_dw_w[l]),
               conf_dw_b[l].reshape(1, CONF_W), conf_ln_g[l].reshape(1, CONF_W),
               conf_ln_b[l].reshape(1, CONF_W), _over_sublanes(sc_conv_w[l]), w_out[l].astype(jnp.bfloat16),
               g_ffn, w_r, b_r)

        main = dict(row0=0, n_batch=bsz, n_tiles=seq // MIX_TILE, tt=MIX_TILE)
        if h_all is None:
            h_meta, cls_meta, halo_a, halo_c = _mixer_call(
                meta_tokens.astype(x.dtype), *zero_halo, wts, row0=0, n_batch=1, n_tiles=1, tt=N_META,
                fresh="rows")
            tail = jnp.concatenate([h_meta, jnp.zeros((SPARE_ROWS, d), x.dtype)], axis=0)
            h_all, cls_main, _, _ = _mixer_call(
                x.reshape(n_main, d), halo_a[0], halo_c[0], wts, fresh="tokens", tail=tail, **main)
        else:
            h_all, cls_meta, halo_a, halo_c = _mixer_call(
                h_all, *zero_halo, wts, row0=n_main, n_batch=1, n_tiles=1, tt=N_META)
            h_all, cls_main, _, _ = _mixer_call(h_all, halo_a[0], halo_c[0], wts, **main)

        cls_seq, cls_met = lax.optimization_barrier((cls_main[:, 0], cls_meta[:, 0]))
        tables = lax.optimization_barrier(_dispatch_tables(cls_seq, cls_met, n_blocks))
        h_all = _moe_call(h_all, tables, g_ffn, w_r, b_r,
                          w_exp_gate, w_exp_up, w_exp_down, layer=l)

    out = _final_norm_call(h_all, final_norm_g.reshape(1, d), n_main)
    return out.reshape(bsz, seq, d)
```

```python
import functools

import jax
import jax.numpy as jnp
from jax import lax
from jax.experimental import pallas as pl
from jax.experimental.pallas import tpu as pltpu

D_MODEL = 1024
N_META = 16
CONF_W = 512
SC_W = 512
PROJ_W = 2 * CONF_W + 3 * SC_W
CONF_KERNEL = 31
SC_KERNEL = 3
N_GROUPS = 4
EXPERTS_PER_GROUP = 8
N_EXPERTS = N_GROUPS * EXPERTS_PER_GROUP
D_EXPERT = 512
EPS = 1e-6

LANES = 128
SUBLANES = 8
HALO_A = 32
HALO_C = 8
CONV_ROWS = 32
MIX_TILE = 1024
NORM_TILE = 2048
MOE_ROWS = 256
N_CLASS_IDS = N_GROUPS * 64
N_CLASSES = N_GROUPS * (EXPERTS_PER_GROUP * (EXPERTS_PER_GROUP - 1) // 2)
VMEM_LIMIT = 56 * 1024 * 1024
ROUTER_COLS = LANES
MOE_BUFFERS = 3
PAD_BANKS = 8
SINK_BANK = PAD_BANKS
SOURCE_BANK = PAD_BANKS + 1
TAIL_TILES = 3
SPARE_ROWS = TAIL_TILES * MIX_TILE - N_META
assert SPARE_ROWS >= (PAD_BANKS + 2) * MOE_ROWS


def _load_tokens(ref, n, first=0):
    return jnp.concatenate(
        [ref[pl.ds(first * SUBLANES + s, n, stride=SUBLANES), :] for s in range(SUBLANES)], axis=-1)


def _store_tokens(ref, x, n):
    for s in range(SUBLANES):
        ref[pl.ds(s, n, stride=SUBLANES), :] = x[:, s * LANES:(s + 1) * LANES]


def _rms(x, g):
    return x * lax.rsqrt(jnp.mean(x * x, axis=-1, keepdims=True) + EPS) * g


def _router_logits(hn_bf16, w_ref, b_ref):
    return jnp.dot(hn_bf16, w_ref[...], preferred_element_type=jnp.float32) + b_ref[...]


N_MIXER_INPUTS = 14


def _mixer_kernel(h_ref, *refs, tt, src_tm, dst_tm, n_batch, has_tail):
    if not has_tail:
        return _mixer_tile(h_ref, *refs, tt=tt, src_tm=src_tm, dst_tm=dst_tm)
    tail_ref, refs = refs[0], refs[1:]
    b = pl.program_id(0)
    pl.when(b < n_batch)(lambda: _mixer_tile(h_ref, *refs, tt=tt, src_tm=src_tm, dst_tm=dst_tm))

    @pl.when(b == n_batch)
    def _():
        hout_ref = refs[N_MIXER_INPUTS - 1]
        _store_tokens(hout_ref, tail_ref[...], tt)


def _mixer_tile(h_ref, halo_a_ref, halo_c_ref, g_mix_ref, w_in_ref, dw_w_ref, dw_b_ref, ln_g_ref,
                ln_b_ref, sc_w_ref, w_out_ref, g_ffn_ref, wr_ref, br_ref,
                hout_ref, cls_ref, halo_a_out, halo_c_out,
                aext, ashift, cext, sb_buf, ybuf, *, tt, src_tm, dst_tm):
    j = pl.program_id(1)

    @pl.when(j == 0)
    def _():
        aext[0:HALO_A, :] = halo_a_ref[...]
        cext[0:HALO_C, :] = halo_c_ref[...]

    h = _load_tokens(h_ref, tt) if src_tm else h_ref[...]
    xn = _rms(h, g_mix_ref[...]).astype(jnp.bfloat16)
    ug = jnp.dot(xn, w_in_ref[:, 0:2 * CONF_W], preferred_element_type=jnp.float32)
    aext[HALO_A:HALO_A + tt, :] = ug[:, 0:CONF_W] * jax.nn.sigmoid(ug[:, CONF_W:2 * CONF_W])
    bcv = jnp.dot(xn, w_in_ref[:, 2 * CONF_W:PROJ_W], preferred_element_type=jnp.float32)
    sb_buf[...] = bcv[:, 0:SC_W]
    cext[HALO_C:HALO_C + tt, :] = bcv[:, SC_W:2 * SC_W] * bcv[:, 2 * SC_W:3 * SC_W]

    n_shift_rows = HALO_A + tt - SUBLANES
    for p in range(1, SUBLANES):
        ashift[p - 1, 0:n_shift_rows, :] = aext[p:p + n_shift_rows, :]

    ch = min(tt, CONV_ROWS)
    tiles = (ch // SUBLANES, SUBLANES, CONF_W)
    for c0 in range(0, tt, ch):
        acc = jnp.broadcast_to(dw_b_ref[...], tiles)
        for k in range(CONF_KERNEL):
            row = HALO_A - (CONF_KERNEL - 1) + c0 + k
            phase = row % SUBLANES
            if phase == 0:
                tap = aext[row:row + ch, :]
            else:
                tap = ashift[phase - 1, row - phase:row - phase + ch, :]
            acc = acc + tap.reshape(tiles) * dw_w_ref[k]
        acc = acc.reshape(ch, CONF_W)
        mu = jnp.mean(acc, axis=-1, keepdims=True)
        xc = acc - mu
        ln = xc * lax.rsqrt(jnp.mean(xc * xc, axis=-1, keepdims=True) + EPS) * ln_g_ref[...] + ln_b_ref[...]
        ybuf[c0:c0 + ch, 0:CONF_W] = jax.nn.silu(ln).astype(jnp.bfloat16)
        sacc = None
        for k in range(SC_KERNEL):
            row = HALO_C - (SC_KERNEL - 1) + c0 + k
            term = cext[row:row + ch, :].reshape(tiles) * sc_w_ref[k]
            sacc = term if sacc is None else sacc + term
        ybuf[c0:c0 + ch, CONF_W:CONF_W + SC_W] = (
            sb_buf[c0:c0 + ch, :] * sacc.reshape(ch, SC_W)).astype(jnp.bfloat16)

    new_halo_a = aext[tt:tt + HALO_A, :]
    new_halo_c = cext[tt:tt + HALO_C, :]
    aext[0:HALO_A, :] = new_halo_a
    cext[0:HALO_C, :] = new_halo_c
    halo_a_out[0] = new_halo_a
    halo_c_out[0] = new_halo_c

    h_new = h + jnp.dot(ybuf[...], w_out_ref[...], preferred_element_type=jnp.float32)
    if dst_tm:
        _store_tokens(hout_ref, h_new, tt)
    else:
        hout_ref[...] = h_new

    logits = _router_logits(_rms(h_new, g_ffn_ref[...]).astype(jnp.bfloat16), wr_ref, br_ref)
    lane = lax.broadcasted_iota(jnp.int32, logits.shape, 1)
    neg = jnp.float32(-jnp.inf)
    big = jnp.int32(1 << 20)
    glog = jnp.where(lane < N_GROUPS, logits, neg)
    gmax = jnp.max(glog, axis=-1, keepdims=True)
    g_idx = jnp.min(jnp.where(glog == gmax, lane, big), axis=-1, keepdims=True)
    el = lane - N_GROUPS
    in_group = (el >= 0) & (el < N_EXPERTS) & ((el >> 3) == g_idx)
    l1 = jnp.where(in_group, logits, neg)
    m1 = jnp.max(l1, axis=-1, keepdims=True)
    i1 = jnp.min(jnp.where(l1 == m1, el, big), axis=-1, keepdims=True)
    l2 = jnp.where(el == i1, neg, l1)
    m2 = jnp.max(l2, axis=-1, keepdims=True)
    i2 = jnp.min(jnp.where(l2 == m2, el, big), axis=-1, keepdims=True)
    lo_e = jnp.minimum(i1, i2) & (EXPERTS_PER_GROUP - 1)
    hi_e = jnp.maximum(i1, i2) & (EXPERTS_PER_GROUP - 1)
    cls_ref[...] = g_idx * 64 + lo_e * 8 + hi_e


def _mixer_call(h_src, halo_a, halo_c, wts, *, row0, n_batch, n_tiles, tt, fresh=None, tail=None):
    blk0 = row0 // tt
    n_rows = n_batch * n_tiles * tt
    last = n_batch * n_tiles - 1
    tail_tiles = 0 if tail is None else tail.shape[0] // tt
    const = lambda shape: pl.BlockSpec(shape, lambda b, j: (0,) * len(shape))
    tile = lambda b, j: jnp.minimum(b * n_tiles + j, last)
    tm_spec = pl.BlockSpec((tt * SUBLANES, LANES), lambda b, j: (blk0 + b * n_tiles + j, 0))
    row_spec = pl.BlockSpec((tt, D_MODEL), lambda b, j: (blk0 + tile(b, j), 0))
    if fresh is None:
        in_spec, out_spec, out_struct = tm_spec, tm_spec, jax.ShapeDtypeStruct(h_src.shape, jnp.float32)
    elif fresh == "rows":
        in_spec, out_spec, out_struct = row_spec, row_spec, jax.ShapeDtypeStruct(h_src.shape, jnp.float32)
    else:
        in_spec = row_spec
        tail_at = lambda b, j: jnp.where(b == n_batch, jnp.minimum(j, tail_tiles - 1), j)
        out_spec = pl.BlockSpec((tt * SUBLANES, LANES), lambda b, j: (b * n_tiles + tail_at(b, j), 0))
        out_struct = jax.ShapeDtypeStruct(((n_rows + tail.shape[0]) * SUBLANES, LANES), jnp.float32)
    tail_specs, tail_args = [], []
    if tail is not None:
        tail_specs = [pl.BlockSpec(
            (tt, D_MODEL), lambda b, j: (jnp.where(b == n_batch, jnp.minimum(j, tail_tiles - 1), 0), 0))]
        tail_args = [tail]
    return pl.pallas_call(
        functools.partial(_mixer_kernel, tt=tt, src_tm=fresh is None, dst_tm=fresh != "rows",
                          n_batch=n_batch, has_tail=tail is not None),
        grid=(n_batch + (tail is not None), n_tiles),
        in_specs=[
            in_spec, *tail_specs,
            const((HALO_A, CONF_W)), const((HALO_C, SC_W)),
            const((1, D_MODEL)), const((D_MODEL, PROJ_W)),
            const((CONF_KERNEL, SUBLANES, CONF_W)), const((1, CONF_W)), const((1, CONF_W)), const((1, CONF_W)),
            const((SC_KERNEL, SUBLANES, SC_W)), const((CONF_W + SC_W, D_MODEL)), const((1, D_MODEL)),
            const((D_MODEL, ROUTER_COLS)), const((1, ROUTER_COLS)),
        ],
        out_specs=[
            out_spec,
            pl.BlockSpec((tt, 1), lambda b, j: (tile(b, j), 0)),
            pl.BlockSpec((1, HALO_A, CONF_W), lambda b, j: (jnp.minimum(b, n_batch - 1), 0, 0)),
            pl.BlockSpec((1, HALO_C, SC_W), lambda b, j: (jnp.minimum(b, n_batch - 1), 0, 0)),
        ],
        out_shape=[
            out_struct,
            jax.ShapeDtypeStruct((n_rows, 1), jnp.int32),
            jax.ShapeDtypeStruct((n_batch, HALO_A, CONF_W), jnp.float32),
            jax.ShapeDtypeStruct((n_batch, HALO_C, SC_W), jnp.float32),
        ],
        scratch_shapes=[
            pltpu.VMEM((HALO_A + tt, CONF_W), jnp.float32),
            pltpu.VMEM((SUBLANES - 1, HALO_A + tt, CONF_W), jnp.float32),
            pltpu.VMEM((HALO_C + tt, SC_W), jnp.float32),
            pltpu.VMEM((tt, SC_W), jnp.float32),
            pltpu.VMEM((tt, CONF_W + SC_W), jnp.bfloat16),
        ],
        input_output_aliases={0: 0} if fresh is None else {},
        compiler_params=pltpu.CompilerParams(
            dimension_semantics=("arbitrary", "arbitrary"), vmem_limit_bytes=VMEM_LIMIT),
        name=f"mixer_t{tt}",
    )(h_src, *tail_args, halo_a, halo_c, *wts)


def _moe_kernel(ea_ref, eb_ref, chg_ref, stage_ref, flag_ref, nblk_ref,
                tok0_ref, gtok_ref, stok_ref, h_in, g_ffn_ref, wr_ref, br_ref,
                wg_st, wu_st, wd_st,
                h_out,
                xbuf0, xbuf1, xbuf2, obuf0, obuf1, obuf2, wg_c, wu_c, wd_c, wga, wua, wda, wgb, wub, wdb,
                gsem, ssem):
    i = pl.program_id(0)
    nblk = nblk_ref[0]
    xbuf = (xbuf0, xbuf1, xbuf2)
    obuf = (obuf0, obuf1, obuf2)
    cur_a = (wga, wua, wda)
    cur_b = (wgb, wub, wdb)

    def token_tile(tok_ref, r, blk=0):
        return pl.ds(pl.multiple_of(tok_ref[blk, 0, r], SUBLANES), SUBLANES)

    def gather_row(tok_ref, r, dst_slot, blk=0):
        return pltpu.make_async_copy(
            h_in.at[token_tile(tok_ref, r, blk), :], xbuf[dst_slot].at[pl.ds(r * SUBLANES, SUBLANES), :],
            gsem.at[dst_slot])

    def scatter_row(r, src_slot):
        return pltpu.make_async_copy(
            obuf[src_slot].at[pl.ds(r * SUBLANES, SUBLANES), :], h_out.at[token_tile(stok_ref, r), :],
            ssem.at[src_slot])

    def wait_gather(dst_slot):
        pltpu.make_async_copy(
            h_in.at[pl.ds(0, MOE_ROWS * SUBLANES), :], xbuf[dst_slot], gsem.at[dst_slot]).wait()

    def wait_scatter(src_slot):
        pltpu.make_async_copy(
            obuf[src_slot], h_out.at[pl.ds(0, MOE_ROWS * SUBLANES), :], ssem.at[src_slot]).wait()

    @pl.when(i == 0)
    def _():
        obuf2[...] = jnp.zeros_like(obuf2)

        def start(r, c):
            gather_row(tok0_ref, r, 0, blk=0).start()
            gather_row(tok0_ref, r, 1, blk=1).start()
            return c

        lax.fori_loop(0, MOE_ROWS, start, 0)

    def step(slot):
        ahead = (slot + 2) % MOE_BUFFERS
        behind = (slot - 1) % MOE_BUFFERS
        ea = ea_ref[i]
        eb = eb_ref[i]
        wait_gather(slot)

        @pl.when(i >= 2)
        def _():
            wait_scatter(slot)

        @pl.when(flag_ref[i] != 0)
        def _():
            dst = stage_ref[i] & (EXPERTS_PER_GROUP - 1)
            wg_c[dst] = wg_st[0, 0].astype(jnp.bfloat16)
            wu_c[dst] = wu_st[0, 0].astype(jnp.bfloat16)
            wd_c[dst] = wd_st[0, 0].astype(jnp.bfloat16)

        for e, changed, cur in ((ea, chg_ref[0, i], cur_a), (eb, chg_ref[1, i], cur_b)):
            @pl.when(changed != 0)
            def _(e=e, cur=cur):
                at = e & (EXPERTS_PER_GROUP - 1)
                cur[0][...] = wg_c[at]
                cur[1][...] = wu_c[at]
                cur[2][...] = wd_c[at]

        for r in range(MOE_ROWS):
            gather_row(gtok_ref, r, ahead).start()
        for r in range(MOE_ROWS):
            scatter_row(r, behind).start()

        x = _load_tokens(xbuf[slot], MOE_ROWS)
        xb = _rms(x, g_ffn_ref[...]).astype(jnp.bfloat16)
        logits = _router_logits(xb, wr_ref, br_ref)
        lane = lax.broadcasted_iota(jnp.int32, logits.shape, 1)
        neg = jnp.float32(-jnp.inf)
        glog = jnp.where(lane < N_GROUPS, logits, neg)
        gmax = jnp.max(glog, axis=-1, keepdims=True)
        gsum = jnp.sum(jnp.exp(glog - gmax), axis=-1, keepdims=True)
        grp = ea >> 3
        l_g = jnp.sum(jnp.where(lane == grp, logits, 0.0), axis=-1, keepdims=True)
        p_g = jnp.exp(l_g - gmax) / gsum
        l_a = jnp.sum(jnp.where(lane == N_GROUPS + ea, logits, 0.0), axis=-1, keepdims=True)
        l_b = jnp.sum(jnp.where(lane == N_GROUPS + eb, logits, 0.0), axis=-1, keepdims=True)
        w_a = p_g / (1.0 + jnp.exp(l_b - l_a))
        w_b = p_g / (1.0 + jnp.exp(l_a - l_b))

        def expert(cur):
            gate = jnp.dot(xb, cur[0][...], preferred_element_type=jnp.float32)
            up = jnp.dot(xb, cur[1][...], preferred_element_type=jnp.float32)
            hid = (jax.nn.silu(gate) * up).astype(jnp.bfloat16)
            return jnp.dot(hid, cur[2][...], preferred_element_type=jnp.float32)

        y = w_a * expert(cur_a) + w_b * expert(cur_b)
        _store_tokens(obuf[slot], x + y, MOE_ROWS)

        @pl.when(i == nblk)
        def _():
            wait_gather((slot + 1) % MOE_BUFFERS)
            wait_gather(ahead)
            wait_scatter((slot - 2) % MOE_BUFFERS)
            wait_scatter(behind)

    rotation = lax.rem(i, MOE_BUFFERS)
    for slot in range(MOE_BUFFERS):
        pl.when((i <= nblk) & (rotation == slot))(functools.partial(step, slot))


def _moe_call(h_all, tables, g_ffn, wr, br, w_gate, w_up, w_down, *, layer):
    tok0, gtok, stok, ea, eb, chg, stage, flag, nblk = tables
    const = lambda shape: pl.BlockSpec(shape, lambda i, *_: (0,) * len(shape))
    staged = lambda shape: pl.BlockSpec(
        (1, 1) + shape, lambda i, ea_, eb_, ch_, st_, fl_, nb_: (layer, st_[i], 0, 0))
    tok_spec = lambda imap: pl.BlockSpec((1, 1, MOE_ROWS), imap, memory_space=pltpu.SMEM)
    grid_spec = pltpu.PrefetchScalarGridSpec(
        num_scalar_prefetch=6,
        grid=(nblk[0] + 1,),
        in_specs=[
            pl.BlockSpec((2, 1, MOE_ROWS), lambda i, *_: (0, 0, 0), memory_space=pltpu.SMEM),
            tok_spec(lambda i, *_: (i, 0, 0)), tok_spec(lambda i, *_: (i, 0, 0)),
            pl.BlockSpec(memory_space=pl.ANY),
            const((1, D_MODEL)), const((D_MODEL, ROUTER_COLS)), const((1, ROUTER_COLS)),
            staged((D_MODEL, D_EXPERT)), staged((D_MODEL, D_EXPERT)), staged((D_EXPERT, D_MODEL)),
        ],
        out_specs=pl.BlockSpec(memory_space=pl.ANY),
        scratch_shapes=[
            *[pltpu.VMEM((MOE_ROWS * SUBLANES, LANES), jnp.float32)] * (2 * MOE_BUFFERS),
            pltpu.VMEM((EXPERTS_PER_GROUP, D_MODEL, D_EXPERT), jnp.bfloat16),
            pltpu.VMEM((EXPERTS_PER_GROUP, D_MODEL, D_EXPERT), jnp.bfloat16),
            pltpu.VMEM((EXPERTS_PER_GROUP, D_EXPERT, D_MODEL), jnp.bfloat16),
            pltpu.VMEM((D_MODEL, D_EXPERT), jnp.bfloat16),
            pltpu.VMEM((D_MODEL, D_EXPERT), jnp.bfloat16),
            pltpu.VMEM((D_EXPERT, D_MODEL), jnp.bfloat16),
            pltpu.VMEM((D_MODEL, D_EXPERT), jnp.bfloat16),
            pltpu.VMEM((D_MODEL, D_EXPERT), jnp.bfloat16),
            pltpu.VMEM((D_EXPERT, D_MODEL), jnp.bfloat16),
            pltpu.SemaphoreType.DMA((MOE_BUFFERS,)),
            pltpu.SemaphoreType.DMA((MOE_BUFFERS,)),
        ],
    )
    return pl.pallas_call(
        _moe_kernel,
        grid_spec=grid_spec,
        out_shape=jax.ShapeDtypeStruct(h_all.shape, h_all.dtype),
        input_output_aliases={9: 0},
        compiler_params=pltpu.CompilerParams(
            dimension_semantics=("arbitrary",), vmem_limit_bytes=VMEM_LIMIT),
        name="moe_pairs",
    )(ea, eb, chg, stage, flag, nblk, tok0, gtok, stok, h_all, g_ffn, wr, br, w_gate, w_up, w_down)


def _final_norm_kernel(h_ref, g_ref, o_ref):
    for first in range(0, NORM_TILE, MIX_TILE):
        o_ref[first:first + MIX_TILE, :] = _rms(_load_tokens(h_ref, MIX_TILE, first), g_ref[...])


def _final_norm_call(h_all, g, n_rows):
    return pl.pallas_call(
        _final_norm_kernel,
        grid=(n_rows // NORM_TILE,),
        in_specs=[pl.BlockSpec((NORM_TILE * SUBLANES, LANES), lambda i: (i, 0)),
                  pl.BlockSpec((1, D_MODEL), lambda i: (0, 0))],
        out_specs=pl.BlockSpec((NORM_TILE, D_MODEL), lambda i: (i, 0)),
        out_shape=jax.ShapeDtypeStruct((n_rows, D_MODEL), jnp.float32),
        compiler_params=pltpu.CompilerParams(dimension_semantics=("arbitrary",), vmem_limit_bytes=VMEM_LIMIT),
        name="final_norm",
    )(h_all, g)


def _pick(tables, idx):
    hit = idx[None, :, None] == jnp.arange(tables.shape[1], dtype=jnp.int32)[None, None, :]
    return jnp.sum(jnp.where(hit, tables[:, None, :], 0), axis=-1)


def _dispatch_tables(cls, cls_meta, n_blocks):
    n_main, n_meta = cls.shape[0], cls_meta.shape[0]
    n_all = n_main + n_meta
    tok_bits = (n_main - 1).bit_length()
    order = jnp.sort((cls << tok_bits) | jnp.arange(n_main, dtype=jnp.int32)) & ((1 << tok_bits) - 1)
    nib = jnp.arange(16, dtype=jnp.int32)[None, :]
    hi_hot = ((cls >> 4)[:, None] == nib).astype(jnp.bfloat16)
    lo_hot = ((cls & 15)[:, None] == nib).astype(jnp.bfloat16)
    counts_main = jnp.dot(
        hi_hot.T, lo_hot, preferred_element_type=jnp.float32).reshape(N_CLASS_IDS).astype(jnp.int32)
    ids = jnp.arange(N_CLASS_IDS, dtype=jnp.int32)
    counts_meta = jnp.sum((cls_meta[:, None] == ids[None, :]).astype(jnp.int32), axis=0)
    meta_rank = jnp.sum(jnp.tril(cls_meta[:, None] == cls_meta[None, :], -1).astype(jnp.int32), axis=1)
    counts = counts_main + counts_meta
    start = jnp.cumsum(counts_main) - counts_main
    padded = (counts + MOE_ROWS - 1) // MOE_ROWS * MOE_ROWS
    pad_end = jnp.cumsum(padded)
    pad_start = pad_end - padded
    nblk = (pad_end[-1] // MOE_ROWS).astype(jnp.int32)

    kidx = jnp.arange(n_blocks, dtype=jnp.int32)
    real = kidx < nblk
    p0 = kidx * MOE_ROWS
    cb = jnp.sum((pad_end[None, :] <= p0[:, None]).astype(jnp.int32), axis=1)
    cb = jnp.minimum(cb, N_CLASS_IDS - 1)
    cls_pad_start, cls_count, cls_start, cls_main = _pick(jnp.stack([pad_start, counts, start, counts_main]), cb)
    r0 = p0 - cls_pad_start
    nv = jnp.where(real, jnp.clip(cls_count - r0, 0, MOE_ROWS), 0).astype(jnp.int32)
    grp = cb >> 6
    blk_a = grp * EXPERTS_PER_GROUP + ((cb >> 3) & 7)
    blk_b = grp * EXPERTS_PER_GROUP + (cb & 7)
    r = jnp.arange(MOE_ROWS, dtype=jnp.int32)[None, :]
    blk_src0 = cls_start + r0
    blk_n_main = jnp.clip(cls_main - r0, 0, MOE_ROWS)
    blk_rank0 = r0 - cls_main

    never = jnp.int32(n_blocks)
    experts = jnp.arange(N_EXPERTS, dtype=jnp.int32)[:, None]
    used = real[None, :] & ((blk_a[None, :] == experts) | (blk_b[None, :] == experts))
    first_use = jnp.min(jnp.where(used, kidx[None, :], never), axis=1)
    new_a = real & (_pick(first_use[None, :], blk_a)[0] == kidx)
    new_b = real & (_pick(first_use[None, :], blk_b)[0] == kidx)
    filler = new_a & new_b
    blk_step = kidx + jnp.cumsum(filler.astype(jnp.int32))
    n_act = nblk + jnp.sum(filler.astype(jnp.int32))

    n_steps = n_blocks + N_EXPERTS // 2 + 1
    step = jnp.arange(n_steps, dtype=jnp.int32)
    k_done = jnp.sum((real[None, :] & (blk_step[None, :] <= step[:, None])).astype(jnp.int32), axis=1) - 1
    kc = jnp.clip(k_done, 0, n_blocks - 1)
    kn = jnp.clip(k_done + 1, 0, n_blocks - 1)
    active = step < n_act
    per_block = jnp.stack([blk_step, nv, blk_src0, blk_a, blk_b, new_a.astype(jnp.int32), new_b.astype(jnp.int32),
                           blk_n_main, blk_rank0, cb])
    c_step, c_nv, c_src0, c_a, c_b, c_new_a, c_new_b, c_n_main, c_rank0, c_cls = _pick(per_block, kc)
    n_a = _pick(blk_a[None, :], kn)[0]
    is_block = active & (k_done >= 0) & (c_step == step)
    is_filler = active & ~is_block

    spare = n_all + (step[:, None] & (PAD_BANKS - 1)) * MOE_ROWS + r
    src = jnp.clip(c_src0[:, None] + r, 0, n_main - 1)
    tok = order[src]
    rank = c_rank0[:, None] + r
    for j in range(n_meta):
        tok = jnp.where((c_cls[:, None] == cls_meta[j]) & (rank == meta_rank[j]), n_main + j, tok)
    rows = jnp.where(is_block[:, None] & (r < c_nv[:, None]), tok, spare).astype(jnp.int32)
    last = jnp.clip(nblk - 1, 0, n_blocks - 1)
    ea = jnp.where(is_block, c_a, jnp.where(is_filler, n_a, blk_a[last]))
    eb = jnp.where(is_block, c_b, jnp.where(is_filler, n_a, blk_b[last]))
    staged_here = jnp.where(
        is_block, jnp.where(c_new_b > 0, c_b, jnp.where(c_new_a > 0, c_a, -1)),
        jnp.where(is_filler, n_a, -1))
    flag = (staged_here >= 0).astype(jnp.int32)
    far = jnp.int32(n_steps)
    next_evt = lax.cummin(jnp.where(flag > 0, step, far), axis=0, reverse=True)
    last_evt = jnp.max(jnp.where(flag > 0, step, 0))
    stage = _pick(staged_here[None, :], jnp.where(next_evt < far, next_evt, last_evt))[0]

    sink = n_all + SINK_BANK * MOE_ROWS + r
    source = n_all + SOURCE_BANK * MOE_ROWS + r
    nxt = jnp.concatenate([rows[2:], rows[-2:]], axis=0)
    prv = jnp.concatenate([rows[:1], rows[:-1]], axis=0)
    gtok = jnp.where((step + 2 < n_act)[:, None], nxt, source)
    stok = jnp.where(((step >= 1) & (step <= n_act))[:, None], prv, sink)
    shape3 = (n_steps, 1, MOE_ROWS)
    rows, gtok, stok = rows * SUBLANES, gtok * SUBLANES, stok * SUBLANES
    chg = jnp.stack([jnp.concatenate([jnp.ones((1,), jnp.int32), (e[1:] != e[:-1]).astype(jnp.int32)])
                     for e in (ea, eb)])
    return (rows[:2].reshape(2, 1, MOE_ROWS), gtok.reshape(shape3), stok.reshape(shape3), ea, eb, chg,
            stage, flag, n_act.reshape(1).astype(jnp.int32))


def _over_sublanes(w):
    return jnp.broadcast_to(w[:, None, :], (w.shape[0], SUBLANES, w.shape[1]))


def kernel(x, meta_tokens, norm_mix_g, w_in, conf_dw_w, conf_dw_b, conf_ln_g, conf_ln_b, sc_conv_w, w_out,
           norm_ffn_g, w_router_group, b_router_group, w_router_expert, b_router_expert, w_exp_gate,
           w_exp_up, w_exp_down, final_norm_g):
    bsz, seq, d = x.shape
    depth = w_in.shape[0]
    n_main = bsz * seq
    n_all = n_main + N_META
    n_blocks = n_all // MOE_ROWS + N_CLASSES
    f32 = jnp.float32

    zero_halo = (jnp.zeros((HALO_A, CONF_W), f32), jnp.zeros((HALO_C, SC_W), f32))
    assert d == SUBLANES * LANES
    h_all = None

    for l in range(depth):
        w_r = jnp.concatenate([w_router_group[l], w_router_expert[l]], axis=1).astype(f32)
        w_r = jnp.pad(w_r, ((0, 0), (0, ROUTER_COLS - w_r.shape[1])))
        w_r = w_r.astype(jnp.bfloat16)
        b_r = jnp.concatenate([b_router_group[l], b_router_expert[l]]).astype(f32)
        b_r = jnp.pad(b_r, (0, ROUTER_COLS - b_r.shape[0])).reshape(1, ROUTER_COLS)
        g_ffn = norm_ffn_g[l].reshape(1, d)
        wts = (norm_mix_g[l].reshape(1, d), w_in[l].astype(jnp.bfloat16), _over_sublanes(conf_dw_w[l]),
               conf_dw_b[l].reshape(1, CONF_W), conf_ln_g[l].reshape(1, CONF_W),
               conf_ln_b[l].reshape(1, CONF_W), _over_sublanes(sc_conv_w[l]), w_out[l].astype(jnp.bfloat16),
               g_ffn, w_r, b_r)

        main = dict(row0=0, n_batch=bsz, n_tiles=seq // MIX_TILE, tt=MIX_TILE)
        if h_all is None:
            h_meta, cls_meta, halo_a, halo_c = _mixer_call(
                meta_tokens.astype(x.dtype), *zero_halo, wts, row0=0, n_batch=1, n_tiles=1, tt=N_META,
                fresh="rows")
            tail = jnp.concatenate([h_meta, jnp.zeros((SPARE_ROWS, d), x.dtype)], axis=0)
            h_all, cls_main, _, _ = _mixer_call(
                x.reshape(n_main, d), halo_a[0], halo_c[0], wts, fresh="tokens", tail=tail, **main)
        else:
            h_all, cls_meta, halo_a, halo_c = _mixer_call(
                h_all, *zero_halo, wts, row0=n_main, n_batch=1, n_tiles=1, tt=N_META)
            h_all, cls_main, _, _ = _mixer_call(h_all, halo_a[0], halo_c[0], wts, **main)

        cls_seq, cls_met = lax.optimization_barrier((cls_main[:, 0], cls_meta[:, 0]))
        tables = _dispatch_tables(cls_seq, cls_met, n_blocks)
        h_all = _moe_call(h_all, tables, g_ffn, w_r, b_r,
                          w_exp_gate, w_exp_up, w_exp_down, layer=l)

    out = _final_norm_call(h_all, final_norm_g.reshape(1, d), n_main)
    return out.reshape(bsz, seq, d)
```

```python
import functools

import jax
import jax.numpy as jnp
from jax import lax
from jax.experimental import pallas as pl
from jax.experimental.pallas import tpu as pltpu

D_MODEL = 1024
N_META = 16
CONF_W = 512
SC_W = 512
PROJ_W = 2 * CONF_W + 3 * SC_W
CONF_KERNEL = 31
SC_KERNEL = 3
N_GROUPS = 4
EXPERTS_PER_GROUP = 8
N_EXPERTS = N_GROUPS * EXPERTS_PER_GROUP
D_EXPERT = 512
EPS = 1e-6

LANES = 128
SUBLANES = 8
HALO_A = 32
HALO_C = 8
CONF_HALF = 256
CONV_ROWS = 64
MIX_TILE = 1024
NORM_TILE = 2048
MOE_ROWS = 256
N_CLASS_IDS = N_GROUPS * 64
N_CLASSES = N_GROUPS * (EXPERTS_PER_GROUP * (EXPERTS_PER_GROUP - 1) // 2)
VMEM_LIMIT = 56 * 1024 * 1024
ROUTER_COLS = LANES
MOE_BUFFERS = 3
PAD_BANKS = 8
SINK_BANK = PAD_BANKS
SOURCE_BANK = PAD_BANKS + 1
TAIL_TILES = 3
SPARE_ROWS = TAIL_TILES * MIX_TILE - N_META
assert SPARE_ROWS >= (PAD_BANKS + 2) * MOE_ROWS


def _load_tokens(ref, n, first=0):
    return jnp.concatenate(
        [ref[pl.ds(first * SUBLANES + s, n, stride=SUBLANES), :] for s in range(SUBLANES)], axis=-1)


def _store_tokens(ref, x, n):
    for s in range(SUBLANES):
        ref[pl.ds(s, n, stride=SUBLANES), :] = x[:, s * LANES:(s + 1) * LANES]


def _rms(x, g):
    return x * lax.rsqrt(jnp.mean(x * x, axis=-1, keepdims=True) + EPS) * g


def _router_logits(hn_bf16, w_ref, b_ref):
    return jnp.dot(hn_bf16, w_ref[...], preferred_element_type=jnp.float32) + b_ref[...]


N_MIXER_INPUTS = 14


def _mixer_kernel(h_ref, *refs, tt, src_tm, dst_tm, n_batch, has_tail):
    if not has_tail:
        return _mixer_tile(h_ref, *refs, tt=tt, src_tm=src_tm, dst_tm=dst_tm)
    tail_ref, refs = refs[0], refs[1:]
    b = pl.program_id(0)
    pl.when(b < n_batch)(lambda: _mixer_tile(h_ref, *refs, tt=tt, src_tm=src_tm, dst_tm=dst_tm))

    @pl.when(b == n_batch)
    def _():
        hout_ref = refs[N_MIXER_INPUTS - 1]
        _store_tokens(hout_ref, tail_ref[...], tt)


def _mixer_tile(h_ref, halo_a_ref, halo_c_ref, g_mix_ref, w_in_ref, dw_w_ref, dw_b_ref, ln_g_ref,
                ln_b_ref, sc_w_ref, w_out_ref, g_ffn_ref, wr_ref, br_ref,
                hout_ref, cls_ref, halo_a_out, halo_c_out,
                aext, ashift, cext, sb_buf, ybuf, *, tt, src_tm, dst_tm):
    j = pl.program_id(1)

    @pl.when(j == 0)
    def _():
        aext[0:HALO_A, :] = halo_a_ref[...]
        cext[0:HALO_C, :] = halo_c_ref[...]

    h = _load_tokens(h_ref, tt) if src_tm else h_ref[...]
    xn = _rms(h, g_mix_ref[...]).astype(jnp.bfloat16)
    ug = jnp.dot(xn, w_in_ref[:, 0:2 * CONF_W], preferred_element_type=jnp.float32)
    aext[HALO_A:HALO_A + tt, :] = ug[:, 0:CONF_W] * jax.nn.sigmoid(ug[:, CONF_W:2 * CONF_W])
    bcv = jnp.dot(xn, w_in_ref[:, 2 * CONF_W:PROJ_W], preferred_element_type=jnp.float32)
    sb_buf[...] = bcv[:, 0:SC_W]
    cext[HALO_C:HALO_C + tt, :] = bcv[:, SC_W:2 * SC_W] * bcv[:, 2 * SC_W:3 * SC_W]

    n_shift_rows = HALO_A + tt - SUBLANES
    for p in range(1, SUBLANES):
        ashift[p - 1, 0:n_shift_rows, :] = aext[p:p + n_shift_rows, :]

    ch = min(tt, CONV_ROWS)
    tiles = (ch // SUBLANES, SUBLANES, CONF_W)
    for c0 in range(0, tt, ch):
        halves = []
        for w0 in range(0, CONF_W, CONF_HALF):
            cols = slice(w0, w0 + CONF_HALF)
            half_tiles = (ch // SUBLANES, SUBLANES, CONF_HALF)
            acc = jnp.broadcast_to(dw_b_ref[:, cols], half_tiles)
            for k in range(CONF_KERNEL):
                row = HALO_A - (CONF_KERNEL - 1) + c0 + k
                phase = row % SUBLANES
                if phase == 0:
                    tap = aext[row:row + ch, cols]
                else:
                    tap = ashift[phase - 1, row - phase:row - phase + ch, cols]
                acc = acc + tap.reshape(half_tiles) * dw_w_ref[k, :, cols]
            halves.append(acc.reshape(ch, CONF_HALF))
        acc = jnp.concatenate(halves, axis=-1)
        mu = jnp.mean(acc, axis=-1, keepdims=True)
        xc = acc - mu
        ln = xc * lax.rsqrt(jnp.mean(xc * xc, axis=-1, keepdims=True) + EPS) * ln_g_ref[...] + ln_b_ref[...]
        ybuf[c0:c0 + ch, 0:CONF_W] = jax.nn.silu(ln).astype(jnp.bfloat16)
        sacc = None
        for k in range(SC_KERNEL):
            row = HALO_C - (SC_KERNEL - 1) + c0 + k
            term = cext[row:row + ch, :].reshape(tiles) * sc_w_ref[k]
            sacc = term if sacc is None else sacc + term
        ybuf[c0:c0 + ch, CONF_W:CONF_W + SC_W] = (
            sb_buf[c0:c0 + ch, :] * sacc.reshape(ch, SC_W)).astype(jnp.bfloat16)

    new_halo_a = aext[tt:tt + HALO_A, :]
    new_halo_c = cext[tt:tt + HALO_C, :]
    aext[0:HALO_A, :] = new_halo_a
    cext[0:HALO_C, :] = new_halo_c
    halo_a_out[0] = new_halo_a
    halo_c_out[0] = new_halo_c

    h_new = h + jnp.dot(ybuf[...], w_out_ref[...], preferred_element_type=jnp.float32)
    if dst_tm:
        _store_tokens(hout_ref, h_new, tt)
    else:
        hout_ref[...] = h_new

    logits = _router_logits(_rms(h_new, g_ffn_ref[...]).astype(jnp.bfloat16), wr_ref, br_ref)
    lane = lax.broadcasted_iota(jnp.int32, logits.shape, 1)
    neg = jnp.float32(-jnp.inf)
    big = jnp.int32(1 << 20)
    glog = jnp.where(lane < N_GROUPS, logits, neg)
    gmax = jnp.max(glog, axis=-1, keepdims=True)
    g_idx = jnp.min(jnp.where(glog == gmax, lane, big), axis=-1, keepdims=True)
    el = lane - N_GROUPS
    in_group = (el >= 0) & (el < N_EXPERTS) & ((el >> 3) == g_idx)
    l1 = jnp.where(in_group, logits, neg)
    m1 = jnp.max(l1, axis=-1, keepdims=True)
    i1 = jnp.min(jnp.where(l1 == m1, el, big), axis=-1, keepdims=True)
    l2 = jnp.where(el == i1, neg, l1)
    m2 = jnp.max(l2, axis=-1, keepdims=True)
    i2 = jnp.min(jnp.where(l2 == m2, el, big), axis=-1, keepdims=True)
    lo_e = jnp.minimum(i1, i2) & (EXPERTS_PER_GROUP - 1)
    hi_e = jnp.maximum(i1, i2) & (EXPERTS_PER_GROUP - 1)
    cls_ref[...] = g_idx * 64 + lo_e * 8 + hi_e


def _mixer_call(h_src, halo_a, halo_c, wts, *, row0, n_batch, n_tiles, tt, fresh=None, tail=None):
    blk0 = row0 // tt
    n_rows = n_batch * n_tiles * tt
    last = n_batch * n_tiles - 1
    tail_tiles = 0 if tail is None else tail.shape[0] // tt
    const = lambda shape: pl.BlockSpec(shape, lambda b, j: (0,) * len(shape))
    tile = lambda b, j: jnp.minimum(b * n_tiles + j, last)
    tm_spec = pl.BlockSpec((tt * SUBLANES, LANES), lambda b, j: (blk0 + b * n_tiles + j, 0))
    row_spec = pl.BlockSpec((tt, D_MODEL), lambda b, j: (blk0 + tile(b, j), 0))
    if fresh is None:
        in_spec, out_spec, out_struct = tm_spec, tm_spec, jax.ShapeDtypeStruct(h_src.shape, jnp.float32)
    elif fresh == "rows":
        in_spec, out_spec, out_struct = row_spec, row_spec, jax.ShapeDtypeStruct(h_src.shape, jnp.float32)
    else:
        in_spec = row_spec
        tail_at = lambda b, j: jnp.where(b == n_batch, jnp.minimum(j, tail_tiles - 1), j)
        out_spec = pl.BlockSpec((tt * SUBLANES, LANES), lambda b, j: (b * n_tiles + tail_at(b, j), 0))
        out_struct = jax.ShapeDtypeStruct(((n_rows + tail.shape[0]) * SUBLANES, LANES), jnp.float32)
    tail_specs, tail_args = [], []
    if tail is not None:
        tail_specs = [pl.BlockSpec(
            (tt, D_MODEL), lambda b, j: (jnp.where(b == n_batch, jnp.minimum(j, tail_tiles - 1), 0), 0))]
        tail_args = [tail]
    return pl.pallas_call(
        functools.partial(_mixer_kernel, tt=tt, src_tm=fresh is None, dst_tm=fresh != "rows",
                          n_batch=n_batch, has_tail=tail is not None),
        grid=(n_batch + (tail is not None), n_tiles),
        in_specs=[
            in_spec, *tail_specs,
            const((HALO_A, CONF_W)), const((HALO_C, SC_W)),
            const((1, D_MODEL)), const((D_MODEL, PROJ_W)),
            const((CONF_KERNEL, SUBLANES, CONF_W)), const((1, CONF_W)), const((1, CONF_W)), const((1, CONF_W)),
            const((SC_KERNEL, SUBLANES, SC_W)), const((CONF_W + SC_W, D_MODEL)), const((1, D_MODEL)),
            const((D_MODEL, ROUTER_COLS)), const((1, ROUTER_COLS)),
        ],
        out_specs=[
            out_spec,
            pl.BlockSpec((tt, 1), lambda b, j: (tile(b, j), 0)),
            pl.BlockSpec((1, HALO_A, CONF_W), lambda b, j: (jnp.minimum(b, n_batch - 1), 0, 0)),
            pl.BlockSpec((1, HALO_C, SC_W), lambda b, j: (jnp.minimum(b, n_batch - 1), 0, 0)),
        ],
        out_shape=[
            out_struct,
            jax.ShapeDtypeStruct((n_rows, 1), jnp.int32),
            jax.ShapeDtypeStruct((n_batch, HALO_A, CONF_W), jnp.float32),
            jax.ShapeDtypeStruct((n_batch, HALO_C, SC_W), jnp.float32),
        ],
        scratch_shapes=[
            pltpu.VMEM((HALO_A + tt, CONF_W), jnp.float32),
            pltpu.VMEM((SUBLANES - 1, HALO_A + tt, CONF_W), jnp.float32),
            pltpu.VMEM((HALO_C + tt, SC_W), jnp.float32),
            pltpu.VMEM((tt, SC_W), jnp.float32),
            pltpu.VMEM((tt, CONF_W + SC_W), jnp.bfloat16),
        ],
        input_output_aliases={0: 0} if fresh is None else {},
        compiler_params=pltpu.CompilerParams(
            dimension_semantics=("arbitrary", "arbitrary"), vmem_limit_bytes=VMEM_LIMIT),
        name=f"mixer_t{tt}",
    )(h_src, *tail_args, halo_a, halo_c, *wts)


def _moe_kernel(ea_ref, eb_ref, chg_ref, stage_ref, flag_ref, nblk_ref,
                tok0_ref, gtok_ref, stok_ref, h_in, g_ffn_ref, wr_ref, br_ref,
                wg_st, wu_st, wd_st,
                h_out,
                xbuf0, xbuf1, xbuf2, obuf0, obuf1, obuf2, wg_c, wu_c, wd_c, wga, wua, wda, wgb, wub, wdb,
                gsem, ssem):
    i = pl.program_id(0)
    nblk = nblk_ref[0]
    xbuf = (xbuf0, xbuf1, xbuf2)
    obuf = (obuf0, obuf1, obuf2)
    cur_a = (wga, wua, wda)
    cur_b = (wgb, wub, wdb)

    def token_tile(tok_ref, r, blk=0):
        return pl.ds(pl.multiple_of(tok_ref[blk, 0, r], SUBLANES), SUBLANES)

    def gather_row(tok_ref, r, dst_slot, blk=0):
        return pltpu.make_async_copy(
            h_in.at[token_tile(tok_ref, r, blk), :], xbuf[dst_slot].at[pl.ds(r * SUBLANES, SUBLANES), :],
            gsem.at[dst_slot])

    def scatter_row(r, src_slot):
        return pltpu.make_async_copy(
            obuf[src_slot].at[pl.ds(r * SUBLANES, SUBLANES), :], h_out.at[token_tile(stok_ref, r), :],
            ssem.at[src_slot])

    def wait_gather(dst_slot):
        pltpu.make_async_copy(
            h_in.at[pl.ds(0, MOE_ROWS * SUBLANES), :], xbuf[dst_slot], gsem.at[dst_slot]).wait()

    def wait_scatter(src_slot):
        pltpu.make_async_copy(
            obuf[src_slot], h_out.at[pl.ds(0, MOE_ROWS * SUBLANES), :], ssem.at[src_slot]).wait()

    @pl.when(i == 0)
    def _():
        obuf2[...] = jnp.zeros_like(obuf2)

        def start(r, c):
            gather_row(tok0_ref, r, 0, blk=0).start()
            gather_row(tok0_ref, r, 1, blk=1).start()
            return c

        lax.fori_loop(0, MOE_ROWS, start, 0)

    def step(slot):
        ahead = (slot + 2) % MOE_BUFFERS
        behind = (slot - 1) % MOE_BUFFERS
        ea = ea_ref[i]
        eb = eb_ref[i]
        wait_gather(slot)

        @pl.when(i >= 2)
        def _():
            wait_scatter(slot)

        @pl.when(flag_ref[i] != 0)
        def _():
            dst = stage_ref[i] & (EXPERTS_PER_GROUP - 1)
            wg_c[dst] = wg_st[0, 0].astype(jnp.bfloat16)
            wu_c[dst] = wu_st[0, 0].astype(jnp.bfloat16)
            wd_c[dst] = wd_st[0, 0].astype(jnp.bfloat16)

        for e, changed, cur in ((ea, chg_ref[0, i], cur_a), (eb, chg_ref[1, i], cur_b)):
            @pl.when(changed != 0)
            def _(e=e, cur=cur):
                at = e & (EXPERTS_PER_GROUP - 1)
                cur[0][...] = wg_c[at]
                cur[1][...] = wu_c[at]
                cur[2][...] = wd_c[at]

        for r in range(MOE_ROWS):
            gather_row(gtok_ref, r, ahead).start()
        for r in range(MOE_ROWS):
            scatter_row(r, behind).start()

        x = _load_tokens(xbuf[slot], MOE_ROWS)
        xb = _rms(x, g_ffn_ref[...]).astype(jnp.bfloat16)
        logits = _router_logits(xb, wr_ref, br_ref)
        lane = lax.broadcasted_iota(jnp.int32, logits.shape, 1)
        neg = jnp.float32(-jnp.inf)
        glog = jnp.where(lane < N_GROUPS, logits, neg)
        gmax = jnp.max(glog, axis=-1, keepdims=True)
        gsum = jnp.sum(jnp.exp(glog - gmax), axis=-1, keepdims=True)
        grp = ea >> 3
        l_g = jnp.sum(jnp.where(lane == grp, logits, 0.0), axis=-1, keepdims=True)
        p_g = jnp.exp(l_g - gmax) / gsum
        l_a = jnp.sum(jnp.where(lane == N_GROUPS + ea, logits, 0.0), axis=-1, keepdims=True)
        l_b = jnp.sum(jnp.where(lane == N_GROUPS + eb, logits, 0.0), axis=-1, keepdims=True)
        w_a = p_g / (1.0 + jnp.exp(l_b - l_a))
        w_b = p_g / (1.0 + jnp.exp(l_a - l_b))

        def expert(cur):
            gate = jnp.dot(xb, cur[0][...], preferred_element_type=jnp.float32)
            up = jnp.dot(xb, cur[1][...], preferred_element_type=jnp.float32)
            hid = (jax.nn.silu(gate) * up).astype(jnp.bfloat16)
            return jnp.dot(hid, cur[2][...], preferred_element_type=jnp.float32)

        y = w_a * expert(cur_a) + w_b * expert(cur_b)
        _store_tokens(obuf[slot], x + y, MOE_ROWS)

        @pl.when(i == nblk)
        def _():
            wait_gather((slot + 1) % MOE_BUFFERS)
            wait_gather(ahead)
            wait_scatter((slot - 2) % MOE_BUFFERS)
            wait_scatter(behind)

    rotation = lax.rem(i, MOE_BUFFERS)
    for slot in range(MOE_BUFFERS):
        pl.when((i <= nblk) & (rotation == slot))(functools.partial(step, slot))


def _moe_call(h_all, tables, g_ffn, wr, br, w_gate, w_up, w_down, *, layer):
    tok0, gtok, stok, ea, eb, chg, stage, flag, nblk = tables
    const = lambda shape: pl.BlockSpec(shape, lambda i, *_: (0,) * len(shape))
    staged = lambda shape: pl.BlockSpec(
        (1, 1) + shape, lambda i, ea_, eb_, ch_, st_, fl_, nb_: (layer, st_[i], 0, 0))
    tok_spec = lambda imap: pl.BlockSpec((1, 1, MOE_ROWS), imap, memory_space=pltpu.SMEM)
    grid_spec = pltpu.PrefetchScalarGridSpec(
        num_scalar_prefetch=6,
        grid=(nblk[0] + 1,),
        in_specs=[
            pl.BlockSpec((2, 1, MOE_ROWS), lambda i, *_: (0, 0, 0), memory_space=pltpu.SMEM),
            tok_spec(lambda i, *_: (i, 0, 0)), tok_spec(lambda i, *_: (i, 0, 0)),
            pl.BlockSpec(memory_space=pl.ANY),
            const((1, D_MODEL)), const((D_MODEL, ROUTER_COLS)), const((1, ROUTER_COLS)),
            staged((D_MODEL, D_EXPERT)), staged((D_MODEL, D_EXPERT)), staged((D_EXPERT, D_MODEL)),
        ],
        out_specs=pl.BlockSpec(memory_space=pl.ANY),
        scratch_shapes=[
            *[pltpu.VMEM((MOE_ROWS * SUBLANES, LANES), jnp.float32)] * (2 * MOE_BUFFERS),
            pltpu.VMEM((EXPERTS_PER_GROUP, D_MODEL, D_EXPERT), jnp.bfloat16),
            pltpu.VMEM((EXPERTS_PER_GROUP, D_MODEL, D_EXPERT), jnp.bfloat16),
            pltpu.VMEM((EXPERTS_PER_GROUP, D_EXPERT, D_MODEL), jnp.bfloat16),
            pltpu.VMEM((D_MODEL, D_EXPERT), jnp.bfloat16),
            pltpu.VMEM((D_MODEL, D_EXPERT), jnp.bfloat16),
            pltpu.VMEM((D_EXPERT, D_MODEL), jnp.bfloat16),
            pltpu.VMEM((D_MODEL, D_EXPERT), jnp.bfloat16),
            pltpu.VMEM((D_MODEL, D_EXPERT), jnp.bfloat16),
            pltpu.VMEM((D_EXPERT, D_MODEL), jnp.bfloat16),
            pltpu.SemaphoreType.DMA((MOE_BUFFERS,)),
            pltpu.SemaphoreType.DMA((MOE_BUFFERS,)),
        ],
    )
    return pl.pallas_call(
        _moe_kernel,
        grid_spec=grid_spec,
        out_shape=jax.ShapeDtypeStruct(h_all.shape, h_all.dtype),
        input_output_aliases={9: 0},
        compiler_params=pltpu.CompilerParams(
            dimension_semantics=("arbitrary",), vmem_limit_bytes=VMEM_LIMIT),
        name="moe_pairs",
    )(ea, eb, chg, stage, flag, nblk, tok0, gtok, stok, h_all, g_ffn, wr, br, w_gate, w_up, w_down)


def _final_norm_kernel(h_ref, g_ref, o_ref):
    for first in range(0, NORM_TILE, MIX_TILE):
        o_ref[first:first + MIX_TILE, :] = _rms(_load_tokens(h_ref, MIX_TILE, first), g_ref[...])


def _final_norm_call(h_all, g, n_rows):
    return pl.pallas_call(
        _final_norm_kernel,
        grid=(n_rows // NORM_TILE,),
        in_specs=[pl.BlockSpec((NORM_TILE * SUBLANES, LANES), lambda i: (i, 0)),
                  pl.BlockSpec((1, D_MODEL), lambda i: (0, 0))],
        out_specs=pl.BlockSpec((NORM_TILE, D_MODEL), lambda i: (i, 0)),
        out_shape=jax.ShapeDtypeStruct((n_rows, D_MODEL), jnp.float32),
        compiler_params=pltpu.CompilerParams(dimension_semantics=("arbitrary",), vmem_limit_bytes=VMEM_LIMIT),
        name="final_norm",
    )(h_all, g)


def _pick(tables, idx):
    hit = idx[None, :, None] == jnp.arange(tables.shape[1], dtype=jnp.int32)[None, None, :]
    return jnp.sum(jnp.where(hit, tables[:, None, :], 0), axis=-1)


def _dispatch_tables(cls, cls_meta, n_blocks):
    n_main, n_meta = cls.shape[0], cls_meta.shape[0]
    n_all = n_main + n_meta
    tok_bits = (n_main - 1).bit_length()
    order = jnp.sort((cls << tok_bits) | jnp.arange(n_main, dtype=jnp.int32)) & ((1 << tok_bits) - 1)
    nib = jnp.arange(16, dtype=jnp.int32)[None, :]
    hi_hot = ((cls >> 4)[:, None] == nib).astype(jnp.bfloat16)
    lo_hot = ((cls & 15)[:, None] == nib).astype(jnp.bfloat16)
    counts_main = jnp.dot(
        hi_hot.T, lo_hot, preferred_element_type=jnp.float32).reshape(N_CLASS_IDS).astype(jnp.int32)
    ids = jnp.arange(N_CLASS_IDS, dtype=jnp.int32)
    counts_meta = jnp.sum((cls_meta[:, None] == ids[None, :]).astype(jnp.int32), axis=0)
    meta_rank = jnp.sum(jnp.tril(cls_meta[:, None] == cls_meta[None, :], -1).astype(jnp.int32), axis=1)
    counts = counts_main + counts_meta
    start = jnp.cumsum(counts_main) - counts_main
    padded = (counts + MOE_ROWS - 1) // MOE_ROWS * MOE_ROWS
    pad_end = jnp.cumsum(padded)
    pad_start = pad_end - padded
    nblk = (pad_end[-1] // MOE_ROWS).astype(jnp.int32)

    kidx = jnp.arange(n_blocks, dtype=jnp.int32)
    real = kidx < nblk
    p0 = kidx * MOE_ROWS
    cb = jnp.sum((pad_end[None, :] <= p0[:, None]).astype(jnp.int32), axis=1)
    cb = jnp.minimum(cb, N_CLASS_IDS - 1)
    cls_pad_start, cls_count, cls_start, cls_main = _pick(jnp.stack([pad_start, counts, start, counts_main]), cb)
    r0 = p0 - cls_pad_start
    nv = jnp.where(real, jnp.clip(cls_count - r0, 0, MOE_ROWS), 0).astype(jnp.int32)
    grp = cb >> 6
    blk_a = grp * EXPERTS_PER_GROUP + ((cb >> 3) & 7)
    blk_b = grp * EXPERTS_PER_GROUP + (cb & 7)
    r = jnp.arange(MOE_ROWS, dtype=jnp.int32)[None, :]
    blk_src0 = cls_start + r0
    blk_n_main = jnp.clip(cls_main - r0, 0, MOE_ROWS)
    blk_rank0 = r0 - cls_main

    never = jnp.int32(n_blocks)
    experts = jnp.arange(N_EXPERTS, dtype=jnp.int32)[:, None]
    used = real[None, :] & ((blk_a[None, :] == experts) | (blk_b[None, :] == experts))
    first_use = jnp.min(jnp.where(used, kidx[None, :], never), axis=1)
    new_a = real & (_pick(first_use[None, :], blk_a)[0] == kidx)
    new_b = real & (_pick(first_use[None, :], blk_b)[0] == kidx)
    filler = new_a & new_b
    blk_step = kidx + jnp.cumsum(filler.astype(jnp.int32))
    n_act = nblk + jnp.sum(filler.astype(jnp.int32))

    n_steps = n_blocks + N_EXPERTS // 2 + 1
    step = jnp.arange(n_steps, dtype=jnp.int32)
    k_done = jnp.sum((real[None, :] & (blk_step[None, :] <= step[:, None])).astype(jnp.int32), axis=1) - 1
    kc = jnp.clip(k_done, 0, n_blocks - 1)
    kn = jnp.clip(k_done + 1, 0, n_blocks - 1)
    active = step < n_act
    per_block = jnp.stack([blk_step, nv, blk_src0, blk_a, blk_b, new_a.astype(jnp.int32), new_b.astype(jnp.int32),
                           blk_n_main, blk_rank0, cb])
    c_step, c_nv, c_src0, c_a, c_b, c_new_a, c_new_b, c_n_main, c_rank0, c_cls = _pick(per_block, kc)
    n_a = _pick(blk_a[None, :], kn)[0]
    is_block = active & (k_done >= 0) & (c_step == step)
    is_filler = active & ~is_block

    spare = n_all + (step[:, None] & (PAD_BANKS - 1)) * MOE_ROWS + r
    src = jnp.clip(c_src0[:, None] + r, 0, n_main - 1)
    tok = order[src]
    rank = c_rank0[:, None] + r
    for j in range(n_meta):
        tok = jnp.where((c_cls[:, None] == cls_meta[j]) & (rank == meta_rank[j]), n_main + j, tok)
    rows = jnp.where(is_block[:, None] & (r < c_nv[:, None]), tok, spare).astype(jnp.int32)
    last = jnp.clip(nblk - 1, 0, n_blocks - 1)
    ea = jnp.where(is_block, c_a, jnp.where(is_filler, n_a, blk_a[last]))
    eb = jnp.where(is_block, c_b, jnp.where(is_filler, n_a, blk_b[last]))
    staged_here = jnp.where(
        is_block, jnp.where(c_new_b > 0, c_b, jnp.where(c_new_a > 0, c_a, -1)),
        jnp.where(is_filler, n_a, -1))
    flag = (staged_here >= 0).astype(jnp.int32)
    far = jnp.int32(n_steps)
    next_evt = lax.cummin(jnp.where(flag > 0, step, far), axis=0, reverse=True)
    last_evt = jnp.max(jnp.where(flag > 0, step, 0))
    stage = _pick(staged_here[None, :], jnp.where(next_evt < far, next_evt, last_evt))[0]

    sink = n_all + SINK_BANK * MOE_ROWS + r
    source = n_all + SOURCE_BANK * MOE_ROWS + r
    nxt = jnp.concatenate([rows[2:], rows[-2:]], axis=0)
    prv = jnp.concatenate([rows[:1], rows[:-1]], axis=0)
    gtok = jnp.where((step + 2 < n_act)[:, None], nxt, source)
    stok = jnp.where(((step >= 1) & (step <= n_act))[:, None], prv, sink)
    shape3 = (n_steps, 1, MOE_ROWS)
    rows, gtok, stok = rows * SUBLANES, gtok * SUBLANES, stok * SUBLANES
    chg = jnp.stack([jnp.concatenate([jnp.ones((1,), jnp.int32), (e[1:] != e[:-1]).astype(jnp.int32)])
                     for e in (ea, eb)])
    return (rows[:2].reshape(2, 1, MOE_ROWS), gtok.reshape(shape3), stok.reshape(shape3), ea, eb, chg,
            stage, flag, n_act.reshape(1).astype(jnp.int32))


def _over_sublanes(w):
    return jnp.broadcast_to(w[:, None, :], (w.shape[0], SUBLANES, w.shape[1]))


def kernel(x, meta_tokens, norm_mix_g, w_in, conf_dw_w, conf_dw_b, conf_ln_g, conf_ln_b, sc_conv_w, w_out,
           norm_ffn_g, w_router_group, b_router_group, w_router_expert, b_router_expert, w_exp_gate,
           w_exp_up, w_exp_down, final_norm_g):
    bsz, seq, d = x.shape
    depth = w_in.shape[0]
    n_main = bsz * seq
    n_all = n_main + N_META
    n_blocks = n_all // MOE_ROWS + N_CLASSES
    f32 = jnp.float32

    zero_halo = (jnp.zeros((HALO_A, CONF_W), f32), jnp.zeros((HALO_C, SC_W), f32))
    assert d == SUBLANES * LANES
    h_all = None

    for l in range(depth):
        w_r = jnp.concatenate([w_router_group[l], w_router_expert[l]], axis=1).astype(f32)
        w_r = jnp.pad(w_r, ((0, 0), (0, ROUTER_COLS - w_r.shape[1])))
        w_r = w_r.astype(jnp.bfloat16)
        b_r = jnp.concatenate([b_router_group[l], b_router_expert[l]]).astype(f32)
        b_r = jnp.pad(b_r, (0, ROUTER_COLS - b_r.shape[0])).reshape(1, ROUTER_COLS)
        g_ffn = norm_ffn_g[l].reshape(1, d)
        wts = (norm_mix_g[l].reshape(1, d), w_in[l].astype(jnp.bfloat16), _over_sublanes(conf_dw_w[l]),
               conf_dw_b[l].reshape(1, CONF_W), conf_ln_g[l].reshape(1, CONF_W),
               conf_ln_b[l].reshape(1, CONF_W), _over_sublanes(sc_conv_w[l]), w_out[l].astype(jnp.bfloat16),
               g_ffn, w_r, b_r)

        main = dict(row0=0, n_batch=bsz, n_tiles=seq // MIX_TILE, tt=MIX_TILE)
        if h_all is None:
            h_meta, cls_meta, halo_a, halo_c = _mixer_call(
                meta_tokens.astype(x.dtype), *zero_halo, wts, row0=0, n_batch=1, n_tiles=1, tt=N_META,
                fresh="rows")
            tail = jnp.concatenate([h_meta, jnp.zeros((SPARE_ROWS, d), x.dtype)], axis=0)
            h_all, cls_main, _, _ = _mixer_call(
                x.reshape(n_main, d), halo_a[0], halo_c[0], wts, fresh="tokens", tail=tail, **main)
        else:
            h_all, cls_meta, halo_a, halo_c = _mixer_call(
                h_all, *zero_halo, wts, row0=n_main, n_batch=1, n_tiles=1, tt=N_META)
            h_all, cls_main, _, _ = _mixer_call(h_all, halo_a[0], halo_c[0], wts, **main)

        cls_seq, cls_met = lax.optimization_barrier((cls_main[:, 0], cls_meta[:, 0]))
        tables = lax.optimization_barrier(_dispatch_tables(cls_seq, cls_met, n_blocks))
        h_all = _moe_call(h_all, tables, g_ffn, w_r, b_r,
                          w_exp_gate, w_exp_up, w_exp_down, layer=l)

    out = _final_norm_call(h_all, final_norm_g.reshape(1, d), n_main)
    return out.reshape(bsz, seq, d)
```
